```python
import jax, jax.numpy as jnp
from jax import lax
import numpy as np

D_MODEL = 1024
BATCH = 4
SEQ = 8192
DEPTH = 1

ATT_HEADS = 8
ATT_KV_HEADS = 2
ATT_HEAD_DIM = 64
WINDOW = 128
ATT_BLOCK = 128
RET_HEADS = 4
RET_QK_DIM = 64
RET_V_DIM = 128
RET_CHUNK = 128
ROT_BASE = 10000.0

ATT_WIDTH = ATT_HEADS * ATT_HEAD_DIM
ATT_KV_WIDTH = ATT_KV_HEADS * ATT_HEAD_DIM
RET_QK_WIDTH = RET_HEADS * RET_QK_DIM
RET_WIDTH = RET_HEADS * RET_V_DIM
MIX_WIDTH = ATT_WIDTH + RET_WIDTH
IN_SPLITS = (ATT_WIDTH, ATT_KV_WIDTH, ATT_KV_WIDTH, ATT_WIDTH,
             RET_QK_WIDTH, RET_QK_WIDTH, RET_WIDTH, RET_WIDTH)
IN_WIDTH = sum(IN_SPLITS)
RMS_EPS = 1e-6
GN_EPS = 1e-6
NEG_INF = -1e30

kernel_name = "hymba_swa_sink_retention_hybrid"


def rmsnorm(x, g):
    xf = x.astype(jnp.float32)
    y = xf * lax.rsqrt(jnp.mean(xf * xf, axis=-1, keepdims=True) + RMS_EPS)
    return (y * g.astype(jnp.float32)).astype(x.dtype)


def sliding_window_attention(q, k, v, sinks):
    B, S = q.shape[0], q.shape[1]
    N = S // ATT_BLOCK
    G = ATT_HEADS // ATT_KV_HEADS
    q = q.reshape(B, N, ATT_BLOCK, ATT_KV_HEADS, G, ATT_HEAD_DIM)
    k = k.reshape(B, N, ATT_BLOCK, ATT_KV_HEADS, ATT_HEAD_DIM)
    v = v.reshape(B, N, ATT_BLOCK, ATT_KV_HEADS, ATT_HEAD_DIM)

    def with_prev(t):
        prev = jnp.concatenate([jnp.zeros_like(t[:, :1]), t[:, :-1]], axis=1)
        return jnp.concatenate([prev, t], axis=2)

    kk, vv = with_prev(k), with_prev(v)
    scale = ATT_HEAD_DIM ** -0.5
    s = jnp.einsum('bnqhgd,bnkhd->bnhgqk', q, kk).astype(jnp.float32) * scale
    qi = jnp.arange(ATT_BLOCK)[:, None]
    kj = jnp.arange(2 * ATT_BLOCK)[None, :]
    diff = qi + ATT_BLOCK - kj
    band = (diff >= 0) & (diff < WINDOW)
    in_cur = kj >= ATT_BLOCK
    blk = jnp.arange(N)[:, None, None]
    valid = band[None] & ((blk > 0) | in_cur[None])
    s = jnp.where(valid[None, :, None, None], s, NEG_INF)
    sink = jnp.broadcast_to(
        sinks.astype(jnp.float32).reshape(ATT_KV_HEADS, G)[None, None, :, :, None, None],
        s.shape[:-1] + (1,))
    p = jax.nn.softmax(jnp.concatenate([s, sink], axis=-1), axis=-1)[..., :-1]
    o = jnp.einsum('bnhgqk,bnkhd->bnqhgd', p.astype(v.dtype), vv)
    return o.reshape(B, S, ATT_WIDTH)


def rotate_pairs(t, cos, sin):
    B, S, H, D = t.shape
    tf = t.astype(jnp.float32).reshape(B, S, H, D // 2, 2)
    a, b = tf[..., 0], tf[..., 1]
    c, s = cos[None, :, None, :], sin[None, :, None, :]
    out = jnp.stack([a * c - b * s, a * s + b * c], axis=-1)
    return out.reshape(B, S, H, D).astype(t.dtype)


def retention(q, k, v, gn_gain):
    B, S = q.shape[0], q.shape[1]
    C = RET_CHUNK
    N = S // C
    pos = jnp.arange(S, dtype=jnp.float32)
    theta = 1.0 / (ROT_BASE ** jnp.linspace(0.0, 1.0, RET_QK_DIM // 2, dtype=jnp.float32))
    ang = pos[:, None] * theta[None, :]
    cos, sin = jnp.cos(ang), jnp.sin(ang)
    q = rotate_pairs(q, cos, sin)
    k = rotate_pairs(k, cos, sin) * (RET_QK_DIM ** -0.5)

    log_gamma = jnp.log(1.0 - 2.0 ** (-5.0 - jnp.arange(RET_HEADS, dtype=jnp.float32)))
    idx = jnp.arange(C, dtype=jnp.float32)
    rel = idx[:, None] - idx[None, :]
    decay_in = jnp.where(rel >= 0, jnp.exp(log_gamma[:, None, None] * jnp.maximum(rel, 0.0)), 0.0)
    k_dec = jnp.exp(log_gamma[:, None] * (C - 1 - idx)[None, :])
    q_dec = jnp.exp(log_gamma[:, None] * (idx + 1)[None, :])
    chunk_decay = jnp.exp(log_gamma * C)

    q = q.reshape(B, N, C, RET_HEADS, RET_QK_DIM)
    k = k.reshape(B, N, C, RET_HEADS, RET_QK_DIM)
    v = v.reshape(B, N, C, RET_HEADS, RET_V_DIM)

    sc = jnp.einsum('bnchd,bnmhd->bnhcm', q, k) * decay_in[None, None]
    o_inner = jnp.einsum('bnhcm,bnmhe->bnche', sc, v)
    kv_chunk = jnp.einsum('bnmhd,bnmhe,hm->bnhde', k, v, k_dec).astype(jnp.float32)

    def step(state, kv):
        return state * chunk_decay[None, :, None, None] + kv, state

    init = jnp.zeros((B, RET_HEADS, RET_QK_DIM, RET_V_DIM), jnp.float32)
    _, prev = lax.scan(step, init, jnp.moveaxis(kv_chunk, 1, 0))
    prev = jnp.moveaxis(prev, 0, 1)
    o_cross = jnp.einsum('bnchd,bnhde,hc->bnche', q, prev, q_dec)
    o = (o_inner + o_cross).astype(jnp.float32).reshape(B, S, RET_HEADS, RET_V_DIM)

    mu = jnp.mean(o, axis=-1, keepdims=True)
    var = jnp.mean(jnp.square(o - mu), axis=-1, keepdims=True)
    o = (o - mu) * lax.rsqrt(var + GN_EPS)
    o = o.reshape(B, S, RET_WIDTH) * gn_gain.astype(jnp.float32)
    return o.astype(v.dtype)


def setup_inputs(seed: int = 0) -> dict:
    key = jax.random.key(seed)
    ks = jax.random.split(key, 8)
    x = jax.random.normal(ks[0], (BATCH, SEQ, D_MODEL), jnp.float32)
    norm_g = 1.0 + 0.02 * jax.random.normal(ks[1], (DEPTH, D_MODEL), jnp.float32)
    w_in = jax.random.normal(ks[2], (DEPTH, D_MODEL, IN_WIDTH), jnp.float32) * D_MODEL ** -0.5
    att_sinks = 0.5 * jax.random.normal(ks[3], (DEPTH, ATT_HEADS), jnp.float32)
    ret_gn_g = 1.0 + 0.02 * jax.random.normal(ks[4], (DEPTH, RET_WIDTH), jnp.float32)
    w_out = jax.random.normal(ks[5], (DEPTH, MIX_WIDTH, D_MODEL), jnp.float32) * MIX_WIDTH ** -0.5
    final_g = 1.0 + 0.02 * jax.random.normal(ks[6], (D_MODEL,), jnp.float32)
    return {"x": x, "norm_g": norm_g, "w_in": w_in, "att_sinks": att_sinks,
            "ret_gn_g": ret_gn_g, "w_out": w_out, "final_g": final_g}


def reference(x, norm_g, w_in, att_sinks, ret_gn_g, w_out, final_g):
    B, S = x.shape[0], x.shape[1]
    cuts = np.cumsum(IN_SPLITS)[:-1].tolist()
    for l in range(DEPTH):
        h = rmsnorm(x, norm_g[l])
        proj = jnp.einsum('bsd,de->bse', h, w_in[l])
        aq, ak, av, az, rq, rk, rv, rz = jnp.split(proj, cuts, axis=-1)
        a = sliding_window_attention(
            aq.reshape(B, S, ATT_HEADS, ATT_HEAD_DIM),
            ak.reshape(B, S, ATT_KV_HEADS, ATT_HEAD_DIM),
            av.reshape(B, S, ATT_KV_HEADS, ATT_HEAD_DIM),
            att_sinks[l]) * jax.nn.silu(az)
        r = retention(
            rq.reshape(B, S, RET_HEADS, RET_QK_DIM),
            rk.reshape(B, S, RET_HEADS, RET_QK_DIM),
            rv.reshape(B, S, RET_HEADS, RET_V_DIM),
            ret_gn_g[l]) * jax.nn.silu(rz)
        mix = jnp.concatenate([a, r], axis=-1)
        x = x + jnp.einsum('bse,ed->bsd', mix, w_out[l])
    return rmsnorm(x, final_g)
```

```python
import functools

import jax
import jax.numpy as jnp
import numpy as np
from jax import lax
from jax.experimental import pallas as pl
from jax.experimental.pallas import tpu as pltpu

D_MODEL = 1024
ATT_HEADS = 8
ATT_KV_HEADS = 2
ATT_HEAD_DIM = 64
WINDOW = 128
BLK = 128
RET_HEADS = 4
RET_QK_DIM = 64
RET_V_DIM = 128
ROT_BASE = 10000.0
RMS_EPS = 1e-6
GN_EPS = 1e-6
NEG_INF = -1e30

ATT_WIDTH = ATT_HEADS * ATT_HEAD_DIM
ATT_KV_WIDTH = ATT_KV_HEADS * ATT_HEAD_DIM
RET_QK_WIDTH = RET_HEADS * RET_QK_DIM
RET_WIDTH = RET_HEADS * RET_V_DIM
MIX_WIDTH = ATT_WIDTH + RET_WIDTH
IN_WIDTH = 2 * ATT_WIDTH + 2 * ATT_KV_WIDTH + 2 * RET_QK_WIDTH + 2 * RET_WIDTH

OFF_AQ = 0
OFF_AK = OFF_AQ + ATT_WIDTH
OFF_AV = OFF_AK + ATT_KV_WIDTH
OFF_AZ = OFF_AV + ATT_KV_WIDTH
OFF_RQ = OFF_AZ + ATT_WIDTH
OFF_RK = OFF_RQ + RET_QK_WIDTH
OFF_RV = OFF_RK + RET_QK_WIDTH
OFF_RZ = OFF_RV + RET_WIDTH

LANES = 128
HEADS_PER_TILE = LANES // ATT_HEAD_DIM
ATT_PAIRS = ATT_HEADS // HEADS_PER_TILE
RET_PAIRS = RET_HEADS // HEADS_PER_TILE
GROUP = ATT_HEADS // ATT_KV_HEADS

SEQ_TILE = 256
VMEM_LIMIT_BYTES = 56 * 1024 * 1024

F32 = jnp.float32
BF16 = jnp.bfloat16


def _silu(z):
    return z * (1.0 / (1.0 + jnp.exp(-z)))


def _layer_kernel(sinks_ref, x_ref, ng_ref, win_ref, gng_ref, wout_ref, fg_ref,
                  cos_ref, sin_ref, din_ref, qdec_ref, kdec_ref, cdec_ref,
                  o_ref, state_ref, kprev_ref, vprev_ref, mix_ref):
    j = pl.program_id(1)

    @pl.when(j == 0)
    def _():
        state_ref[...] = jnp.zeros_like(state_ref)
        kprev_ref[...] = jnp.zeros_like(kprev_ref)
        vprev_ref[...] = jnp.zeros_like(vprev_ref)

    x = x_ref[0]
    ms = jnp.mean(x * x, axis=-1, keepdims=True)
    h = (x * lax.rsqrt(ms + RMS_EPS) * ng_ref[...]).astype(BF16)
    proj = jnp.dot(h, win_ref[...], preferred_element_type=F32)

    lane = lax.broadcasted_iota(jnp.int32, (BLK, LANES), 1)
    lo_half = lane < ATT_HEAD_DIM
    even_lane = (lane & 1) == 0
    lane2 = lax.broadcasted_iota(jnp.int32, (2 * BLK, LANES), 1)
    lo_half2 = lane2 < ATT_HEAD_DIM
    qi = lax.broadcasted_iota(jnp.int32, (BLK, 2 * BLK), 0)
    kj = lax.broadcasted_iota(jnp.int32, (BLK, 2 * BLK), 1)
    diff = qi + BLK - kj
    band = (diff >= 0) & (diff < WINDOW)
    in_cur = kj >= BLK
    ones_lo = jnp.where(lo_half2, 1.0, 0.0).astype(BF16)
    ones_hi = jnp.where(lo_half2, 0.0, 1.0).astype(BF16)
    zeros_bf = jnp.zeros((BLK, LANES), BF16)

    n_blocks = x.shape[0] // BLK
    for i in range(n_blocks):
        r0 = i * BLK
        rows = slice(r0, r0 + BLK)

        if i == 0:
            has_prev = j > 0
            valid = band & (in_cur | has_prev)
            kdup_prev = [kprev_ref[g] for g in range(ATT_KV_HEADS)]
            vdup_prev = [vprev_ref[g] for g in range(ATT_KV_HEADS)]
        else:
            valid = band
        k_cur = proj[rows, OFF_AK:OFF_AK + LANES]
        v_cur = proj[rows, OFF_AV:OFF_AV + LANES]
        k_sw = pltpu.roll(k_cur, ATT_HEAD_DIM, 1)
        v_sw = pltpu.roll(v_cur, ATT_HEAD_DIM, 1)
        kdup_cur = [jnp.where(lo_half, k_cur, k_sw).astype(BF16),
                    jnp.where(lo_half, k_sw, k_cur).astype(BF16)]
        vdup_cur = [jnp.where(lo_half, v_cur, v_sw).astype(BF16),
                    jnp.where(lo_half, v_sw, v_cur).astype(BF16)]
        for g in range(ATT_KV_HEADS):
            kdup = jnp.concatenate([kdup_prev[g], kdup_cur[g]], axis=0)
            vdup = jnp.concatenate([vdup_prev[g], vdup_cur[g]], axis=0)
            zero2 = jnp.zeros_like(vdup)
            rhs_v = jnp.concatenate([
                jnp.concatenate([jnp.where(lo_half2, vdup, zero2), ones_lo], axis=1),
                jnp.concatenate([jnp.where(lo_half2, zero2, vdup), ones_hi], axis=1),
            ], axis=0)
            for pp in range(GROUP // HEADS_PER_TILE):
                p = g * (GROUP // HEADS_PER_TILE) + pp
                c0 = OFF_AQ + p * LANES
                q2 = (proj[rows, c0:c0 + LANES] * (ATT_HEAD_DIM ** -0.5)).astype(BF16)
                probs = []
                sink_terms = []
                for e in range(HEADS_PER_TILE):
                    head = p * HEADS_PER_TILE + e
                    sel = lo_half if e == 0 else jnp.logical_not(lo_half)
                    qm = jnp.where(sel, q2, zeros_bf)
                    s = lax.dot_general(qm, kdup, (((1,), (1,)), ((), ())),
                                        preferred_element_type=F32)
                    s = jnp.where(valid, s, NEG_INF)
                    sink = sinks_ref[head]
                    m = jnp.maximum(jnp.max(s, axis=-1, keepdims=True), sink)
                    probs.append(jnp.exp(s - m).astype(BF16))
                    sink_terms.append(jnp.exp(sink - m))
                lhs = jnp.concatenate(probs, axis=1)
                res = jnp.dot(lhs, rhs_v, preferred_element_type=F32)
                den = res[:, LANES:] + jnp.where(lo_half, sink_terms[0], sink_terms[1])
                a2 = res[:, :LANES] / den
                z2 = proj[rows, OFF_AZ + p * LANES:OFF_AZ + (p + 1) * LANES]
                mix_ref[rows, p * LANES:(p + 1) * LANES] = (a2 * _silu(z2)).astype(BF16)
        kdup_prev, vdup_prev = kdup_cur, vdup_cur

        cos = cos_ref[rows, :]
        sin = sin_ref[rows, :]

        def rotate(t):
            swapped = jnp.where(even_lane, pltpu.roll(t, LANES - 1, 1), pltpu.roll(t, 1, 1))
            return t * cos + swapped * sin

        for p in range(RET_PAIRS):
            q2 = rotate(proj[rows, OFF_RQ + p * LANES:OFF_RQ + (p + 1) * LANES])
            k2 = rotate(proj[rows, OFF_RK + p * LANES:OFF_RK + (p + 1) * LANES]) * (RET_QK_DIM ** -0.5)
            q2b = q2.astype(BF16)
            k2b = k2.astype(BF16)
            qd2b = (q2 * qdec_ref[p]).astype(BF16)
            kd2b = (k2 * kdec_ref[p]).astype(BF16)
            st = state_ref[p * LANES:(p + 1) * LANES, :]
            st_b = st.astype(BF16)
            v_pair = proj[rows, OFF_RV + p * 2 * LANES:OFF_RV + (p + 1) * 2 * LANES].astype(BF16)
            for e in range(HEADS_PER_TILE):
                head = p * HEADS_PER_TILE + e
                sel = lo_half if e == 0 else jnp.logical_not(lo_half)
                sc = lax.dot_general(jnp.where(sel, q2b, zeros_bf), k2b,
                                     (((1,), (1,)), ((), ())),
                                     preferred_element_type=F32)
                sc = sc * din_ref[head]
                lhs = jnp.concatenate([sc.astype(BF16), jnp.where(sel, qd2b, zeros_bf)], axis=1)
                v_h = v_pair[:, e * LANES:(e + 1) * LANES]
                rhs = jnp.concatenate([v_h, st_b], axis=0)
                o = jnp.dot(lhs, rhs, preferred_element_type=F32)
                mu = jnp.mean(o, axis=-1, keepdims=True)
                oc = o - mu
                var = jnp.mean(oc * oc, axis=-1, keepdims=True)
                on = oc * lax.rsqrt(var + GN_EPS) * gng_ref[:, head * LANES:(head + 1) * LANES]
                zc = OFF_RZ + head * LANES
                rz = proj[rows, zc:zc + LANES]
                mc = ATT_WIDTH + head * LANES
                mix_ref[rows, mc:mc + LANES] = (on * _silu(rz)).astype(BF16)
            kv = lax.dot_general(kd2b, v_pair, (((0,), (0,)), ((), ())),
                                 preferred_element_type=F32)
            row_lo = lax.broadcasted_iota(jnp.int32, (LANES, LANES), 0) < RET_QK_DIM
            kv_sel = jnp.where(row_lo, kv[:, :LANES], kv[:, LANES:])
            state_ref[p * LANES:(p + 1) * LANES, :] = st * cdec_ref[p * LANES:(p + 1) * LANES, :] + kv_sel

    for g in range(ATT_KV_HEADS):
        kprev_ref[g] = kdup_prev[g]
        vprev_ref[g] = vdup_prev[g]

    y = x + jnp.dot(mix_ref[...], wout_ref[...], preferred_element_type=F32)
    ms2 = jnp.mean(y * y, axis=-1, keepdims=True)
    o_ref[0] = y * lax.rsqrt(ms2 + RMS_EPS) * fg_ref[...]


def _retention_tables(seq):
    pos = jnp.arange(seq, dtype=F32)
    theta = 1.0 / (ROT_BASE ** jnp.linspace(0.0, 1.0, RET_QK_DIM // 2, dtype=F32))
    ang = pos[:, None] * theta[None, :]
    cos, sin = jnp.cos(ang), jnp.sin(ang)
    cos_l = jnp.tile(jnp.repeat(cos, 2, axis=1), (1, HEADS_PER_TILE))
    sign = jnp.tile(jnp.array([-1.0, 1.0], F32), RET_QK_DIM // 2)
    sin_l = jnp.tile(jnp.repeat(sin, 2, axis=1) * sign[None, :], (1, HEADS_PER_TILE))

    log_gamma = jnp.log(1.0 - 2.0 ** (-5.0 - jnp.arange(RET_HEADS, dtype=F32)))
    idx = jnp.arange(BLK, dtype=F32)
    rel = idx[:, None] - idx[None, :]
    decay_in = jnp.where(rel >= 0, jnp.exp(log_gamma[:, None, None] * jnp.maximum(rel, 0.0)), 0.0)
    k_dec = jnp.exp(log_gamma[:, None] * (BLK - 1 - idx)[None, :])
    q_dec = jnp.exp(log_gamma[:, None] * (idx + 1)[None, :])
    chunk_decay = jnp.exp(log_gamma * BLK)

    def per_pair(dec):
        t = jnp.repeat(dec.T[:, :, None], RET_QK_DIM, axis=2)
        t = t.reshape(BLK, RET_PAIRS, LANES)
        return jnp.transpose(t, (1, 0, 2))

    cdec = jnp.broadcast_to(jnp.repeat(chunk_decay, RET_QK_DIM)[:, None],
                            (RET_QK_WIDTH, RET_V_DIM))
    return cos_l, sin_l, decay_in, per_pair(q_dec), per_pair(k_dec), cdec


def _full(shape):
    return pl.BlockSpec(shape, lambda b, j: (0,) * len(shape))


@jax.jit
def kernel(x, norm_g, w_in, att_sinks, ret_gn_g, w_out, final_g):
    batch, seq, d = x.shape
    depth = w_in.shape[0]
    assert depth == 1 and d == D_MODEL and seq % SEQ_TILE == 0
    cos_l, sin_l, decay_in, qdec, kdec, cdec = _retention_tables(seq)
    t = SEQ_TILE
    call = pl.pallas_call(
        _layer_kernel,
        grid=(batch, seq // t),
        in_specs=[
            pl.BlockSpec(memory_space=pltpu.SMEM),
            pl.BlockSpec((1, t, D_MODEL), lambda b, j: (b, j, 0)),
            _full((1, D_MODEL)),
            _full((D_MODEL, IN_WIDTH)),
            _full((1, RET_WIDTH)),
            _full((MIX_WIDTH, D_MODEL)),
            _full((1, D_MODEL)),
            pl.BlockSpec((t, LANES), lambda b, j: (j, 0)),
            pl.BlockSpec((t, LANES), lambda b, j: (j, 0)),
            _full((RET_HEADS, BLK, BLK)),
            _full((RET_PAIRS, BLK, LANES)),
            _full((RET_PAIRS, BLK, LANES)),
            _full((RET_QK_WIDTH, RET_V_DIM)),
        ],
        out_specs=pl.BlockSpec((1, t, D_MODEL), lambda b, j: (b, j, 0)),
        out_shape=jax.ShapeDtypeStruct(x.shape, x.dtype),
        scratch_shapes=[
            pltpu.VMEM((RET_QK_WIDTH, RET_V_DIM), F32),
            pltpu.VMEM((ATT_KV_HEADS, BLK, LANES), BF16),
            pltpu.VMEM((ATT_KV_HEADS, BLK, LANES), BF16),
            pltpu.VMEM((t, MIX_WIDTH), BF16),
        ],
        compiler_params=pltpu.CompilerParams(
            dimension_semantics=("arbitrary", "arbitrary"),
            vmem_limit_bytes=VMEM_LIMIT_BYTES),
        name="hymba_layer",
    )
    return call(att_sinks[0], x, norm_g[0][None, :], w_in[0].astype(BF16),
                ret_gn_g[0][None, :], w_out[0].astype(BF16), final_g[None, :],
                cos_l, sin_l, decay_in, qdec, kdec, cdec)
```

```python
import jax
import jax.numpy as jnp
from jax import lax
from jax.experimental import pallas as pl
from jax.experimental.pallas import tpu as pltpu

D_MODEL = 1024
ATT_HEADS = 8
ATT_KV_HEADS = 2
ATT_HEAD_DIM = 64
WINDOW = 128
BLK = 128
RET_HEADS = 4
RET_QK_DIM = 64
RET_V_DIM = 128
ROT_BASE = 10000.0
RMS_EPS = 1e-6
GN_EPS = 1e-6
NEG_INF = -1e30
LOG2E = 1.4426950408889634

ATT_WIDTH = ATT_HEADS * ATT_HEAD_DIM
ATT_KV_WIDTH = ATT_KV_HEADS * ATT_HEAD_DIM
RET_QK_WIDTH = RET_HEADS * RET_QK_DIM
RET_WIDTH = RET_HEADS * RET_V_DIM
MIX_WIDTH = ATT_WIDTH + RET_WIDTH
IN_WIDTH = 2 * ATT_WIDTH + 2 * ATT_KV_WIDTH + 2 * RET_QK_WIDTH + 2 * RET_WIDTH

OFF_AQ = 0
OFF_AK = OFF_AQ + ATT_WIDTH
OFF_AV = OFF_AK + ATT_KV_WIDTH
OFF_AZ = OFF_AV + ATT_KV_WIDTH
OFF_RQ = OFF_AZ + ATT_WIDTH
OFF_RK = OFF_RQ + RET_QK_WIDTH
OFF_RV = OFF_RK + RET_QK_WIDTH
OFF_RZ = OFF_RV + RET_WIDTH

LANES = 128
MXU_COLS = 256
N_MXU = 2
PROJ_CHUNK = N_MXU * MXU_COLS
HEADS_PER_TILE = LANES // ATT_HEAD_DIM
ATT_PAIRS = ATT_HEADS // HEADS_PER_TILE
RET_PAIRS = RET_HEADS // HEADS_PER_TILE
GROUP = ATT_HEADS // ATT_KV_HEADS

SEQ_TILE = 512
SUB_TILE = 256
N_SUB = SEQ_TILE // SUB_TILE
BLKS_PER_SUB = SUB_TILE // BLK
IN_CHUNKS = -(-IN_WIDTH // PROJ_CHUNK)
OUT_CHUNKS = D_MODEL // PROJ_CHUNK
VMEM_LIMIT_BYTES = 56 * 1024 * 1024

F32 = jnp.float32
BF16 = jnp.bfloat16


def _silu(z):
    hz = 0.5 * z
    return hz + hz * jnp.tanh(hz)


def _interleave(units, fillers):
    done = 0
    for k, unit in enumerate(units):
        unit()
        due = -(-(k + 1) * len(fillers) // len(units))
        while done < due:
            fillers[done]()
            done += 1
    for f in fillers[done:]:
        f()


def _layer_kernel(sinks_ref, x_ref, ng_ref, win_ref, gng_ref, wout_ref, fg_ref,
                  cos_ref, sin_ref, din_ref, qdec_ref, kdec_ref, cdec_ref,
                  o_ref, state_ref, kprev_ref, vprev_ref, *mix_refs):
    j = pl.program_id(1)

    @pl.when(j == 0)
    def _():
        state_ref[...] = jnp.zeros_like(state_ref)
        kprev_ref[...] = jnp.zeros_like(kprev_ref)
        vprev_ref[...] = jnp.zeros_like(vprev_ref)

    lane = lax.broadcasted_iota(jnp.int32, (BLK, LANES), 1)
    lo_half = lane < ATT_HEAD_DIM
    hi_half = jnp.logical_not(lo_half)
    even_lane = (lane & 1) == 0
    lane2 = lax.broadcasted_iota(jnp.int32, (2 * BLK, LANES), 1)
    lo_half2 = lane2 < ATT_HEAD_DIM
    qi = lax.broadcasted_iota(jnp.int32, (BLK, 2 * BLK), 0)
    kj = lax.broadcasted_iota(jnp.int32, (BLK, 2 * BLK), 1)
    diff = qi + BLK - kj
    band = (diff >= 0) & (diff < WINDOW)
    in_cur = kj >= BLK
    ones_lo = jnp.where(lo_half2, 1.0, 0.0).astype(BF16)
    ones_hi = jnp.where(lo_half2, 0.0, 1.0).astype(BF16)
    zeros_bf = jnp.zeros((BLK, LANES), BF16)
    row_lo = lax.broadcasted_iota(jnp.int32, (LANES, LANES), 0) < RET_QK_DIM

    proj_chunks = [dict() for _ in range(N_SUB)]
    out_chunks = [dict() for _ in range(N_SUB)]
    normed = {}
    carry = {}
    att_rhs = {}

    def proj_tile(s, i, col):
        chunk, off = divmod(col, PROJ_CHUNK)
        return proj_chunks[s][chunk][i * BLK:(i + 1) * BLK, off:off + LANES]

    def in_proj_chunk(s, c):
        def run():
            if s not in normed:
                xs = x_ref[0, s * SUB_TILE:(s + 1) * SUB_TILE, :]
                ms = jnp.mean(xs * xs, axis=-1, keepdims=True)
                normed[s] = (xs * lax.rsqrt(ms + RMS_EPS) * ng_ref[...]).astype(BF16)
            proj_chunks[s][c] = jnp.dot(normed[s], win_ref[:, c * PROJ_CHUNK:min((c + 1) * PROJ_CHUNK, IN_WIDTH)],
                                        preferred_element_type=F32)
        return run

    def out_proj_chunk(s, c):
        def run():
            out_chunks[s][c] = jnp.dot(mix_refs[s][...], wout_ref[:, c * PROJ_CHUNK:(c + 1) * PROJ_CHUNK],
                                       preferred_element_type=F32)
        return run

    def finish(s):
        def run():
            rows = slice(s * SUB_TILE, (s + 1) * SUB_TILE)
            y = x_ref[0, rows, :] + jnp.concatenate([out_chunks[s][c] for c in range(OUT_CHUNKS)], axis=1)
            ms = jnp.mean(y * y, axis=-1, keepdims=True)
            o_ref[0, rows, :] = y * lax.rsqrt(ms + RMS_EPS) * fg_ref[...]
        return run

    def kv_prep(s, i):
        if s == 0 and i == 0:
            carry["k"] = [kprev_ref[g] for g in range(ATT_KV_HEADS)]
            carry["v"] = [vprev_ref[g] for g in range(ATT_KV_HEADS)]
        k_cur = proj_tile(s, i, OFF_AK)
        v_cur = proj_tile(s, i, OFF_AV)
        v_sw = pltpu.roll(v_cur, ATT_HEAD_DIM, 1)
        k_t = k_cur.T
        kdup_cur = [jnp.concatenate([k_t[g * ATT_HEAD_DIM:(g + 1) * ATT_HEAD_DIM]] * HEADS_PER_TILE,
                                    axis=0).astype(BF16) for g in range(ATT_KV_HEADS)]
        vdup_cur = [jnp.where(lo_half, v_cur, v_sw).astype(BF16),
                    jnp.where(lo_half, v_sw, v_cur).astype(BF16)]
        per_head = []
        for g in range(ATT_KV_HEADS):
            kdup = jnp.concatenate([carry["k"][g], kdup_cur[g]], axis=1)
            vdup = jnp.concatenate([carry["v"][g], vdup_cur[g]], axis=0)
            zero2 = jnp.zeros_like(vdup)
            rhs_v = jnp.concatenate([
                jnp.concatenate([jnp.where(lo_half2, vdup, zero2), ones_lo], axis=1),
                jnp.concatenate([jnp.where(lo_half2, zero2, vdup), ones_hi], axis=1),
            ], axis=0)
            per_head.append((kdup, rhs_v))
        att_rhs[(s, i)] = per_head
        carry["k"], carry["v"] = kdup_cur, vdup_cur

    stash = {}

    def att_scores(s, i):
        def run():
            kv_prep(s, i)
            valid = band & (in_cur | (j > 0)) if (s == 0 and i == 0) else band
            scores = []
            for p in range(ATT_PAIRS):
                kdup, _ = att_rhs[(s, i)][p // (GROUP // HEADS_PER_TILE)]
                q2 = (proj_tile(s, i, OFF_AQ + p * LANES) * (ATT_HEAD_DIM ** -0.5 * LOG2E)).astype(BF16)
                for e in range(HEADS_PER_TILE):
                    qm = jnp.where(lo_half if e == 0 else hi_half, q2, zeros_bf)
                    sc = jnp.dot(qm, kdup, preferred_element_type=F32)
                    scores.append(jnp.where(valid, sc, NEG_INF))
            stash[("sc", s, i)] = scores
        return run

    def att_softmax(s, i):
        def run():
            probs, sink_terms = [], []
            for head, sc in enumerate(stash.pop(("sc", s, i))):
                sink = sinks_ref[head] * LOG2E
                m = jnp.maximum(jnp.max(sc, axis=-1, keepdims=True), sink)
                probs.append(jnp.exp2(sc - m).astype(BF16))
                sink_terms.append(jnp.exp2(sink - m))
            stash[("p", s, i)] = (probs, sink_terms)
        return run

    def att_values(s, i):
        def run():
            probs, sink_terms = stash.pop(("p", s, i))
            for p in range(ATT_PAIRS):
                _, rhs_v = att_rhs[(s, i)][p // (GROUP // HEADS_PER_TILE)]
                h0 = p * HEADS_PER_TILE
                lhs = jnp.concatenate(probs[h0:h0 + HEADS_PER_TILE], axis=1)
                res = jnp.dot(lhs, rhs_v, preferred_element_type=F32)
                den = res[:, LANES:] + jnp.where(lo_half, sink_terms[h0], sink_terms[h0 + 1])
                a2 = res[:, :LANES] / den
                z2 = proj_tile(s, i, OFF_AZ + p * LANES)
                mix_refs[s][i * BLK:(i + 1) * BLK, p * LANES:(p + 1) * LANES] = (a2 * _silu(z2)).astype(BF16)
        return run

    def ret_scores(s, i):
        def run():
            r0 = (s * BLKS_PER_SUB + i) * BLK
            cos = cos_ref[r0:r0 + BLK, :]
            sin = sin_ref[r0:r0 + BLK, :]

            def rotate(t):
                swapped = jnp.where(even_lane, pltpu.roll(t, LANES - 1, 1), pltpu.roll(t, 1, 1))
                return t * cos + swapped * sin

            per_pair = []
            for p in range(RET_PAIRS):
                q2 = rotate(proj_tile(s, i, OFF_RQ + p * LANES))
                k2 = rotate(proj_tile(s, i, OFF_RK + p * LANES)) * (RET_QK_DIM ** -0.5)
                q2b = q2.astype(BF16)
                k2b = k2.astype(BF16)
                qd2b = (q2 * qdec_ref[p]).astype(BF16)
                kd2b = (k2 * kdec_ref[p]).astype(BF16)
                lhs = []
                for e in range(HEADS_PER_TILE):
                    sel = lo_half if e == 0 else hi_half
                    sc = lax.dot_general(jnp.where(sel, q2b, zeros_bf), k2b,
                                         (((1,), (1,)), ((), ())),
                                         preferred_element_type=F32)
                    sc = sc * din_ref[p * HEADS_PER_TILE + e]
                    lhs.append(jnp.concatenate([sc.astype(BF16), jnp.where(sel, qd2b, zeros_bf)], axis=1))
                per_pair.append((lhs, kd2b))
            stash[("ret", s, i)] = per_pair
        return run

    def ret_outputs(s, i):
        def run():
            per_pair = stash.pop(("ret", s, i))
            for p in range(RET_PAIRS):
                lhs, kd2b = per_pair[p]
                st = state_ref[p * LANES:(p + 1) * LANES, :]
                st_b = st.astype(BF16)
                v_heads = [proj_tile(s, i, OFF_RV + (p * HEADS_PER_TILE + e) * LANES).astype(BF16)
                           for e in range(HEADS_PER_TILE)]
                for e in range(HEADS_PER_TILE):
                    head = p * HEADS_PER_TILE + e
                    rhs = jnp.concatenate([v_heads[e], st_b], axis=0)
                    o = jnp.dot(lhs[e], rhs, preferred_element_type=F32)
                    mu = jnp.mean(o, axis=-1, keepdims=True)
                    oc = o - mu
                    var = jnp.mean(oc * oc, axis=-1, keepdims=True)
                    on = oc * lax.rsqrt(var + GN_EPS) * gng_ref[:, head * LANES:(head + 1) * LANES]
                    rz = proj_tile(s, i, OFF_RZ + head * LANES)
                    mc = ATT_WIDTH + head * LANES
                    mix_refs[s][i * BLK:(i + 1) * BLK, mc:mc + LANES] = (on * _silu(rz)).astype(BF16)
                kv = lax.dot_general(kd2b, jnp.concatenate(v_heads, axis=1), (((0,), (0,)), ((), ())),
                                     preferred_element_type=F32)
                kv_sel = jnp.where(row_lo, kv[:, :LANES], kv[:, LANES:])
                state_ref[p * LANES:(p + 1) * LANES, :] = st * cdec_ref[p * LANES:(p + 1) * LANES, :] + kv_sel
        return run

    def mixer_units(s):
        units = []
        for stage in (att_scores, ret_scores, att_softmax, ret_outputs, att_values):
            units += [stage(s, i) for i in range(BLKS_PER_SUB)]
        return units

    for c in range(IN_CHUNKS):
        in_proj_chunk(0, c)()
    for s in range(N_SUB):
        fillers = []
        if s + 1 < N_SUB:
            fillers += [in_proj_chunk(s + 1, c) for c in range(IN_CHUNKS)]
        if s >= 1:
            fillers += [out_proj_chunk(s - 1, c) for c in range(OUT_CHUNKS)] + [finish(s - 1)]
        _interleave(mixer_units(s), fillers)
    for c in range(OUT_CHUNKS):
        out_proj_chunk(N_SUB - 1, c)()
    finish(N_SUB - 1)()

    for g in range(ATT_KV_HEADS):
        kprev_ref[g] = carry["k"][g]
        vprev_ref[g] = carry["v"][g]


def _retention_tables(seq):
    pos = jnp.arange(seq, dtype=F32)
    theta = 1.0 / (ROT_BASE ** jnp.linspace(0.0, 1.0, RET_QK_DIM // 2, dtype=F32))
    ang = pos[:, None] * theta[None, :]
    cos, sin = jnp.cos(ang), jnp.sin(ang)
    cos_l = jnp.tile(jnp.repeat(cos, 2, axis=1), (1, HEADS_PER_TILE))
    sign = jnp.tile(jnp.array([-1.0, 1.0], F32), RET_QK_DIM // 2)
    sin_l = jnp.tile(jnp.repeat(sin, 2, axis=1) * sign[None, :], (1, HEADS_PER_TILE))

    log_gamma = jnp.log(1.0 - 2.0 ** (-5.0 - jnp.arange(RET_HEADS, dtype=F32)))
    idx = jnp.arange(BLK, dtype=F32)
    rel = idx[:, None] - idx[None, :]
    decay_in = jnp.where(rel >= 0, jnp.exp(log_gamma[:, None, None] * jnp.maximum(rel, 0.0)), 0.0)
    k_dec = jnp.exp(log_gamma[:, None] * (BLK - 1 - idx)[None, :])
    q_dec = jnp.exp(log_gamma[:, None] * (idx + 1)[None, :])
    chunk_decay = jnp.exp(log_gamma * BLK)

    def per_pair(dec):
        t = jnp.repeat(dec.T[:, :, None], RET_QK_DIM, axis=2)
        t = t.reshape(BLK, RET_PAIRS, LANES)
        return jnp.transpose(t, (1, 0, 2))

    cdec = jnp.broadcast_to(jnp.repeat(chunk_decay, RET_QK_DIM)[:, None],
                            (RET_QK_WIDTH, RET_V_DIM))
    return cos_l, sin_l, decay_in, per_pair(q_dec), per_pair(k_dec), cdec


def _full(shape):
    return pl.BlockSpec(shape, lambda b, j: (0,) * len(shape))


@jax.jit
def kernel(x, norm_g, w_in, att_sinks, ret_gn_g, w_out, final_g):
    batch, seq, d = x.shape
    depth = w_in.shape[0]
    assert depth == 1 and d == D_MODEL and seq % SEQ_TILE == 0
    cos_l, sin_l, decay_in, qdec, kdec, cdec = _retention_tables(seq)
    t = SEQ_TILE
    call = pl.pallas_call(
        _layer_kernel,
        grid=(batch, seq // t),
        in_specs=[
            pl.BlockSpec(memory_space=pltpu.SMEM),
            pl.BlockSpec((1, t, D_MODEL), lambda b, j: (b, j, 0)),
            _full((1, D_MODEL)),
            _full((D_MODEL, IN_WIDTH)),
            _full((1, RET_WIDTH)),
            _full((MIX_WIDTH, D_MODEL)),
            _full((1, D_MODEL)),
            pl.BlockSpec((t, LANES), lambda b, j: (j, 0)),
            pl.BlockSpec((t, LANES), lambda b, j: (j, 0)),
            _full((RET_HEADS, BLK, BLK)),
            _full((RET_PAIRS, BLK, LANES)),
            _full((RET_PAIRS, BLK, LANES)),
            _full((RET_QK_WIDTH, RET_V_DIM)),
        ],
        out_specs=pl.BlockSpec((1, t, D_MODEL), lambda b, j: (b, j, 0)),
        out_shape=jax.ShapeDtypeStruct(x.shape, x.dtype),
        scratch_shapes=[
            pltpu.VMEM((RET_QK_WIDTH, RET_V_DIM), F32),
            pltpu.VMEM((ATT_KV_HEADS, BLK, LANES), BF16),
            pltpu.VMEM((ATT_KV_HEADS, BLK, LANES), BF16),
        ] + [pltpu.VMEM((SUB_TILE, MIX_WIDTH), BF16)] * N_SUB,
        compiler_params=pltpu.CompilerParams(
            dimension_semantics=("arbitrary", "arbitrary"),
            vmem_limit_bytes=VMEM_LIMIT_BYTES),
        name="hymba_layer",
    )
    return call(att_sinks[0], x, norm_g[0][None, :], w_in[0].astype(BF16),
                ret_gn_g[0][None, :], w_out[0].astype(BF16), final_g[None, :],
                cos_l, sin_l, decay_in, qdec, kdec, cdec)
```

```python
import functools

import jax
import jax.numpy as jnp
import numpy as np
from jax import lax
from jax.experimental import pallas as pl
from jax.experimental.pallas import tpu as pltpu

D_MODEL = 1024
ATT_HEADS = 8
ATT_KV_HEADS = 2
ATT_HEAD_DIM = 64
WINDOW = 128
BLK = 128
RET_HEADS = 4
RET_QK_DIM = 64
RET_V_DIM = 128
ROT_BASE = 10000.0
RMS_EPS = 1e-6
GN_EPS = 1e-6
NEG_INF = -1e30
LOG2E = 1.4426950408889634

ATT_WIDTH = ATT_HEADS * ATT_HEAD_DIM
ATT_KV_WIDTH = ATT_KV_HEADS * ATT_HEAD_DIM
RET_QK_WIDTH = RET_HEADS * RET_QK_DIM
RET_WIDTH = RET_HEADS * RET_V_DIM
MIX_WIDTH = ATT_WIDTH + RET_WIDTH
IN_WIDTH = 2 * ATT_WIDTH + 2 * ATT_KV_WIDTH + 2 * RET_QK_WIDTH + 2 * RET_WIDTH

OFF_AQ = 0
OFF_AK = OFF_AQ + ATT_WIDTH
OFF_AV = OFF_AK + ATT_KV_WIDTH
OFF_AZ = OFF_AV + ATT_KV_WIDTH
OFF_RQ = OFF_AZ + ATT_WIDTH
OFF_RK = OFF_RQ + RET_QK_WIDTH
OFF_RV = OFF_RK + RET_QK_WIDTH
OFF_RZ = OFF_RV + RET_WIDTH

LANES = 128
MXU_COLS = 256
N_MXU = 2
PROJ_CHUNK = N_MXU * MXU_COLS
HEADS_PER_TILE = LANES // ATT_HEAD_DIM
ATT_PAIRS = ATT_HEADS // HEADS_PER_TILE
RET_PAIRS = RET_HEADS // HEADS_PER_TILE
GROUP = ATT_HEADS // ATT_KV_HEADS

SEQ_TILE = 1024
SUB_TILE = 256
N_SUB = SEQ_TILE // SUB_TILE
BLKS_PER_SUB = SUB_TILE // BLK
IN_CHUNKS = -(-IN_WIDTH // PROJ_CHUNK)
OUT_CHUNKS = D_MODEL // PROJ_CHUNK
VMEM_LIMIT_BYTES = 56 * 1024 * 1024

F32 = jnp.float32
BF16 = jnp.bfloat16


def _silu(z):
    hz = 0.5 * z
    return hz + hz * jnp.tanh(hz)


def _interleave(units, fillers):
    done = 0
    for k, unit in enumerate(units):
        unit()
        due = -(-(k + 1) * len(fillers) // len(units))
        while done < due:
            fillers[done]()
            done += 1
    for f in fillers[done:]:
        f()


def _layer_kernel(tiles_per_seq, sinks_ref, x_ref, xnext_ref, ng_ref, win_ref, gng_ref, wout_ref, fg_ref,
                  cos_ref, sin_ref, din_ref, qdec_ref, kdec_ref, cdec_ref,
                  o_ref, state_ref, kprev_ref, vprev_ref, pfirst_ref, *mix_refs):
    t = pl.program_id(0)
    j = lax.rem(t, tiles_per_seq)

    def project_first_subtile(src_ref):
        held = {}

        def chunk(c):
            def run():
                if not held:
                    xs = src_ref[0]
                    ms = jnp.mean(xs * xs, axis=-1, keepdims=True)
                    held["h"] = (xs * lax.rsqrt(ms + RMS_EPS) * ng_ref[...]).astype(BF16)
                c0, c1 = c * PROJ_CHUNK, min((c + 1) * PROJ_CHUNK, IN_WIDTH)
                pfirst_ref[:, c0:c1] = jnp.dot(held["h"], win_ref[:, c0:c1], preferred_element_type=F32)
            return run
        return [chunk(c) for c in range(IN_CHUNKS)]

    @pl.when(t == 0)
    def _():
        for f in project_first_subtile(x_ref.at[:, 0:SUB_TILE, :]):
            f()

    @pl.when(j == 0)
    def _():
        state_ref[...] = jnp.zeros_like(state_ref)
        kprev_ref[...] = jnp.zeros_like(kprev_ref)
        vprev_ref[...] = jnp.zeros_like(vprev_ref)

    lane = lax.broadcasted_iota(jnp.int32, (BLK, LANES), 1)
    lo_half = lane < ATT_HEAD_DIM
    hi_half = jnp.logical_not(lo_half)
    even_lane = (lane & 1) == 0
    lane2 = lax.broadcasted_iota(jnp.int32, (2 * BLK, LANES), 1)
    lo_half2 = lane2 < ATT_HEAD_DIM
    qi = lax.broadcasted_iota(jnp.int32, (BLK, 2 * BLK), 0)
    kj = lax.broadcasted_iota(jnp.int32, (BLK, 2 * BLK), 1)
    diff = qi + BLK - kj
    band = (diff >= 0) & (diff < WINDOW)
    in_cur = kj >= BLK
    ones_lo = jnp.where(lo_half2, 1.0, 0.0).astype(BF16)
    ones_hi = jnp.where(lo_half2, 0.0, 1.0).astype(BF16)
    zeros_bf = jnp.zeros((BLK, LANES), BF16)
    row_lo = lax.broadcasted_iota(jnp.int32, (LANES, LANES), 0) < RET_QK_DIM

    proj_chunks = [dict() for _ in range(N_SUB)]
    out_chunks = [dict() for _ in range(N_SUB)]
    normed = {}
    carry = {}
    att_rhs = {}

    def proj_tile(s, i, col):
        if s == 0:
            return pfirst_ref[i * BLK:(i + 1) * BLK, col:col + LANES]
        chunk, off = divmod(col, PROJ_CHUNK)
        return proj_chunks[s][chunk][i * BLK:(i + 1) * BLK, off:off + LANES]

    def in_proj_chunk(s, c):
        def run():
            if s not in normed:
                xs = x_ref[0, s * SUB_TILE:(s + 1) * SUB_TILE, :]
                ms = jnp.mean(xs * xs, axis=-1, keepdims=True)
                normed[s] = (xs * lax.rsqrt(ms + RMS_EPS) * ng_ref[...]).astype(BF16)
            proj_chunks[s][c] = jnp.dot(normed[s], win_ref[:, c * PROJ_CHUNK:min((c + 1) * PROJ_CHUNK, IN_WIDTH)],
                                        preferred_element_type=F32)
        return run

    def out_proj_chunk(s, c):
        def run():
            out_chunks[s][c] = jnp.dot(mix_refs[s][...], wout_ref[:, c * PROJ_CHUNK:(c + 1) * PROJ_CHUNK],
                                       preferred_element_type=F32)
        return run

    def finish(s):
        def run():
            rows = slice(s * SUB_TILE, (s + 1) * SUB_TILE)
            y = x_ref[0, rows, :] + jnp.concatenate([out_chunks[s][c] for c in range(OUT_CHUNKS)], axis=1)
            ms = jnp.mean(y * y, axis=-1, keepdims=True)
            o_ref[0, rows, :] = y * lax.rsqrt(ms + RMS_EPS) * fg_ref[...]
        return run

    def kv_prep(s, i):
        if s == 0 and i == 0:
            carry["k"] = [kprev_ref[g] for g in range(ATT_KV_HEADS)]
            carry["v"] = [vprev_ref[g] for g in range(ATT_KV_HEADS)]
        k_cur = proj_tile(s, i, OFF_AK)
        v_cur = proj_tile(s, i, OFF_AV)
        v_sw = pltpu.roll(v_cur, ATT_HEAD_DIM, 1)
        k_t = k_cur.T
        kdup_cur = [jnp.concatenate([k_t[g * ATT_HEAD_DIM:(g + 1) * ATT_HEAD_DIM]] * HEADS_PER_TILE,
                                    axis=0).astype(BF16) for g in range(ATT_KV_HEADS)]
        vdup_cur = [jnp.where(lo_half, v_cur, v_sw).astype(BF16),
                    jnp.where(lo_half, v_sw, v_cur).astype(BF16)]
        per_head = []
        for g in range(ATT_KV_HEADS):
            kdup = jnp.concatenate([carry["k"][g], kdup_cur[g]], axis=1)
            vdup = jnp.concatenate([carry["v"][g], vdup_cur[g]], axis=0)
            zero2 = jnp.zeros_like(vdup)
            rhs_v = jnp.concatenate([
                jnp.concatenate([jnp.where(lo_half2, vdup, zero2), ones_lo], axis=1),
                jnp.concatenate([jnp.where(lo_half2, zero2, vdup), ones_hi], axis=1),
            ], axis=0)
            per_head.append((kdup, rhs_v))
        att_rhs[(s, i)] = per_head
        carry["k"], carry["v"] = kdup_cur, vdup_cur

    stash = {}

    def att_scores(s, i):
        def run():
            kv_prep(s, i)
            valid = band & (in_cur | (j > 0)) if (s == 0 and i == 0) else band
            scores = []
            for p in range(ATT_PAIRS):
                kdup, _ = att_rhs[(s, i)][p // (GROUP // HEADS_PER_TILE)]
                q2 = (proj_tile(s, i, OFF_AQ + p * LANES) * (ATT_HEAD_DIM ** -0.5 * LOG2E)).astype(BF16)
                for e in range(HEADS_PER_TILE):
                    qm = jnp.where(lo_half if e == 0 else hi_half, q2, zeros_bf)
                    sc = jnp.dot(qm, kdup, preferred_element_type=F32)
                    scores.append(jnp.where(valid, sc, NEG_INF))
            stash[("sc", s, i)] = scores
        return run

    def att_softmax(s, i):
        def run():
            probs, sink_terms = [], []
            for head, sc in enumerate(stash.pop(("sc", s, i))):
                sink = sinks_ref[head] * LOG2E
                m = jnp.maximum(jnp.max(sc, axis=-1, keepdims=True), sink)
                probs.append(jnp.exp2(sc - m).astype(BF16))
                sink_terms.append(jnp.exp2(sink - m))
            stash[("p", s, i)] = (probs, sink_terms)
        return run

    def att_values(s, i):
        def run():
            probs, sink_terms = stash.pop(("p", s, i))
            for p in range(ATT_PAIRS):
                _, rhs_v = att_rhs[(s, i)][p // (GROUP // HEADS_PER_TILE)]
                h0 = p * HEADS_PER_TILE
                lhs = jnp.concatenate(probs[h0:h0 + HEADS_PER_TILE], axis=1)
                res = jnp.dot(lhs, rhs_v, preferred_element_type=F32)
                den = res[:, LANES:] + jnp.where(lo_half, sink_terms[h0], sink_terms[h0 + 1])
                a2 = res[:, :LANES] / den
                z2 = proj_tile(s, i, OFF_AZ + p * LANES)
                mix_refs[s][i * BLK:(i + 1) * BLK, p * LANES:(p + 1) * LANES] = (a2 * _silu(z2)).astype(BF16)
        return run

    def ret_scores(s, i):
        def run():
            r0 = (s * BLKS_PER_SUB + i) * BLK
            cos = cos_ref[r0:r0 + BLK, :]
            sin = sin_ref[r0:r0 + BLK, :]

            def rotate(t):
                swapped = jnp.where(even_lane, pltpu.roll(t, LANES - 1, 1), pltpu.roll(t, 1, 1))
                return t * cos + swapped * sin

            per_pair = []
            for p in range(RET_PAIRS):
                q2 = rotate(proj_tile(s, i, OFF_RQ + p * LANES))
                k2 = rotate(proj_tile(s, i, OFF_RK + p * LANES)) * (RET_QK_DIM ** -0.5)
                q2b = q2.astype(BF16)
                k2b = k2.astype(BF16)
                qd2b = (q2 * qdec_ref[p]).astype(BF16)
                kd2b = (k2 * kdec_ref[p]).astype(BF16)
                lhs = []
                for e in range(HEADS_PER_TILE):
                    sel = lo_half if e == 0 else hi_half
                    sc = lax.dot_general(jnp.where(sel, q2b, zeros_bf), k2b,
                                         (((1,), (1,)), ((), ())),
                                         preferred_element_type=F32)
                    sc = sc * din_ref[p * HEADS_PER_TILE + e]
                    lhs.append(jnp.concatenate([sc.astype(BF16), jnp.where(sel, qd2b, zeros_bf)], axis=1))
                per_pair.append((lhs, kd2b))
            stash[("ret", s, i)] = per_pair
        return run

    def ret_outputs(s, i):
        def run():
            per_pair = stash.pop(("ret", s, i))
            for p in range(RET_PAIRS):
                lhs, kd2b = per_pair[p]
                st = state_ref[p * LANES:(p + 1) * LANES, :]
                st_b = st.astype(BF16)
                v_heads = [proj_tile(s, i, OFF_RV + (p * HEADS_PER_TILE + e) * LANES).astype(BF16)
                           for e in range(HEADS_PER_TILE)]
                for e in range(HEADS_PER_TILE):
                    head = p * HEADS_PER_TILE + e
                    rhs = jnp.concatenate([v_heads[e], st_b], axis=0)
                    o = jnp.dot(lhs[e], rhs, preferred_element_type=F32)
                    mu = jnp.mean(o, axis=-1, keepdims=True)
                    oc = o - mu
                    var = jnp.mean(oc * oc, axis=-1, keepdims=True)
                    on = oc * lax.rsqrt(var + GN_EPS) * gng_ref[:, head * LANES:(head + 1) * LANES]
                    rz = proj_tile(s, i, OFF_RZ + head * LANES)
                    mc = ATT_WIDTH + head * LANES
                    mix_refs[s][i * BLK:(i + 1) * BLK, mc:mc + LANES] = (on * _silu(rz)).astype(BF16)
                kv = lax.dot_general(kd2b, jnp.concatenate(v_heads, axis=1), (((0,), (0,)), ((), ())),
                                     preferred_element_type=F32)
                kv_sel = jnp.where(row_lo, kv[:, :LANES], kv[:, LANES:])
                state_ref[p * LANES:(p + 1) * LANES, :] = st * cdec_ref[p * LANES:(p + 1) * LANES, :] + kv_sel
        return run

    def mixer_units(s):
        units = []
        for stage in (att_scores, ret_scores, att_softmax, ret_outputs, att_values):
            units += [stage(s, i) for i in range(BLKS_PER_SUB)]
        return units

    for s in range(N_SUB):
        fillers = []
        if s + 1 < N_SUB:
            fillers += [in_proj_chunk(s + 1, c) for c in range(IN_CHUNKS)]
        else:
            fillers += project_first_subtile(xnext_ref)
        if s >= 1:
            fillers += [out_proj_chunk(s - 1, c) for c in range(OUT_CHUNKS)] + [finish(s - 1)]
        _interleave(mixer_units(s), fillers)
    for c in range(OUT_CHUNKS):
        out_proj_chunk(N_SUB - 1, c)()
    finish(N_SUB - 1)()

    for g in range(ATT_KV_HEADS):
        kprev_ref[g] = carry["k"][g]
        vprev_ref[g] = carry["v"][g]


def _retention_tables(seq):
    pos = np.arange(seq, dtype=np.float64)
    theta = 1.0 / (ROT_BASE ** np.linspace(0.0, 1.0, RET_QK_DIM // 2))
    ang = pos[:, None] * theta[None, :]
    cos, sin = np.cos(ang), np.sin(ang)
    cos_l = np.tile(np.repeat(cos, 2, axis=1), (1, HEADS_PER_TILE))
    sign = np.tile(np.array([-1.0, 1.0]), RET_QK_DIM // 2)
    sin_l = np.tile(np.repeat(sin, 2, axis=1) * sign[None, :], (1, HEADS_PER_TILE))

    log_gamma = np.log(1.0 - 2.0 ** (-5.0 - np.arange(RET_HEADS, dtype=np.float64)))
    idx = np.arange(BLK, dtype=np.float64)
    rel = idx[:, None] - idx[None, :]
    decay_in = np.where(rel >= 0, np.exp(log_gamma[:, None, None] * np.maximum(rel, 0.0)), 0.0)
    k_dec = np.exp(log_gamma[:, None] * (BLK - 1 - idx)[None, :])
    q_dec = np.exp(log_gamma[:, None] * (idx + 1)[None, :])
    chunk_decay = np.exp(log_gamma * BLK)

    def per_pair(dec):
        t = np.repeat(dec.T[:, :, None], RET_QK_DIM, axis=2)
        t = t.reshape(BLK, RET_PAIRS, LANES)
        return np.transpose(t, (1, 0, 2))

    cdec = np.broadcast_to(np.repeat(chunk_decay, RET_QK_DIM)[:, None],
                           (RET_QK_WIDTH, RET_V_DIM))
    tables = (cos_l, sin_l, decay_in, per_pair(q_dec), per_pair(k_dec), cdec)
    return tuple(jnp.asarray(np.ascontiguousarray(a, dtype=np.float32)) for a in tables)


def _full(shape):
    return pl.BlockSpec(shape, lambda t: (0,) * len(shape))


@jax.jit
def kernel(x, norm_g, w_in, att_sinks, ret_gn_g, w_out, final_g):
    batch, seq, d = x.shape
    depth = w_in.shape[0]
    assert depth == 1 and d == D_MODEL and seq % SEQ_TILE == 0
    cos_l, sin_l, decay_in, qdec, kdec, cdec = _retention_tables(seq)
    tile = SEQ_TILE
    n_seq = seq // tile
    n_tiles = batch * n_seq

    def this_tile(t):
        return (t // n_seq, t % n_seq, 0)

    def next_first_subtile(t):
        u = jnp.minimum(t + 1, n_tiles - 1)
        return (u // n_seq, (u % n_seq) * N_SUB, 0)

    call = pl.pallas_call(
        functools.partial(_layer_kernel, n_seq),
        grid=(n_tiles,),
        in_specs=[
            pl.BlockSpec(memory_space=pltpu.SMEM),
            pl.BlockSpec((1, tile, D_MODEL), this_tile),
            pl.BlockSpec((1, SUB_TILE, D_MODEL), next_first_subtile),
            _full((1, D_MODEL)),
            _full((D_MODEL, IN_WIDTH)),
            _full((1, RET_WIDTH)),
            _full((MIX_WIDTH, D_MODEL)),
            _full((1, D_MODEL)),
            pl.BlockSpec((tile, LANES), lambda t: (t % n_seq, 0)),
            pl.BlockSpec((tile, LANES), lambda t: (t % n_seq, 0)),
            _full((RET_HEADS, BLK, BLK)),
            _full((RET_PAIRS, BLK, LANES)),
            _full((RET_PAIRS, BLK, LANES)),
            _full((RET_QK_WIDTH, RET_V_DIM)),
        ],
        out_specs=pl.BlockSpec((1, tile, D_MODEL), this_tile),
        out_shape=jax.ShapeDtypeStruct(x.shape, x.dtype),
        scratch_shapes=[
            pltpu.VMEM((RET_QK_WIDTH, RET_V_DIM), F32),
            pltpu.VMEM((ATT_KV_HEADS, BLK, LANES), BF16),
            pltpu.VMEM((ATT_KV_HEADS, BLK, LANES), BF16),
            pltpu.VMEM((SUB_TILE, IN_WIDTH), F32),
        ] + [pltpu.VMEM((SUB_TILE, MIX_WIDTH), BF16)] * N_SUB,
        compiler_params=pltpu.CompilerParams(
            dimension_semantics=("arbitrary",),
            vmem_limit_bytes=VMEM_LIMIT_BYTES),
        name="hymba_layer",
    )
    return call(att_sinks[0], x, x, norm_g[0][None, :], w_in[0].astype(BF16),
                ret_gn_g[0][None, :], w_out[0].astype(BF16), final_g[None, :],
                cos_l, sin_l, decay_in, qdec, kdec, cdec)
```

```python
import functools

import jax
import jax.numpy as jnp
import numpy as np
from jax import lax
from jax.experimental import pallas as pl
from jax.experimental.pallas import tpu as pltpu

D_MODEL = 1024
ATT_HEADS = 8
ATT_KV_HEADS = 2
ATT_HEAD_DIM = 64
WINDOW = 128
BLK = 128
RET_HEADS = 4
RET_QK_DIM = 64
RET_V_DIM = 128
ROT_BASE = 10000.0
RMS_EPS = 1e-6
GN_EPS = 1e-6
NEG_INF = -1e30
LOG2E = 1.4426950408889634

ATT_WIDTH = ATT_HEADS * ATT_HEAD_DIM
ATT_KV_WIDTH = ATT_KV_HEADS * ATT_HEAD_DIM
RET_QK_WIDTH = RET_HEADS * RET_QK_DIM
RET_WIDTH = RET_HEADS * RET_V_DIM
MIX_WIDTH = ATT_WIDTH + RET_WIDTH
IN_WIDTH = 2 * ATT_WIDTH + 2 * ATT_KV_WIDTH + 2 * RET_QK_WIDTH + 2 * RET_WIDTH

OFF_AQ = 0
OFF_AK = OFF_AQ + ATT_WIDTH
OFF_AV = OFF_AK + ATT_KV_WIDTH
OFF_AZ = OFF_AV + ATT_KV_WIDTH
OFF_RQ = OFF_AZ + ATT_WIDTH
OFF_RK = OFF_RQ + RET_QK_WIDTH
OFF_RV = OFF_RK + RET_QK_WIDTH
OFF_RZ = OFF_RV + RET_WIDTH

LANES = 128
MXU_COLS = 256
N_MXU = 2
PROJ_CHUNK = N_MXU * MXU_COLS
HEADS_PER_TILE = LANES // ATT_HEAD_DIM
ATT_PAIRS = ATT_HEADS // HEADS_PER_TILE
RET_PAIRS = RET_HEADS // HEADS_PER_TILE
GROUP = ATT_HEADS // ATT_KV_HEADS

SEQ_TILE = 1024
SUB_TILE = 256
N_SUB = SEQ_TILE // SUB_TILE
BLKS_PER_SUB = SUB_TILE // BLK
IN_CHUNKS = -(-IN_WIDTH // PROJ_CHUNK)
OUT_CHUNKS = D_MODEL // PROJ_CHUNK
VMEM_LIMIT_BYTES = 56 * 1024 * 1024

F32 = jnp.float32
BF16 = jnp.bfloat16


def _silu(z):
    hz = 0.5 * z
    return hz + hz * jnp.tanh(hz)


def _interleave(units, fillers):
    done = 0
    for k, unit in enumerate(units):
        unit()
        due = -(-(k + 1) * len(fillers) // len(units))
        while done < due:
            fillers[done]()
            done += 1
    for f in fillers[done:]:
        f()


def _layer_kernel(tiles_per_seq, sinks_ref, x_ref, xnext_ref, ng_ref, win_ref, gng_ref, wout_ref, fg_ref,
                  cos_ref, sin_ref, din_ref, qdec_ref, kdec_ref, cdec_ref,
                  o_ref, state_ref, kprev_ref, vprev_ref, pfirst_ref, *mix_refs):
    t = pl.program_id(0)
    j = lax.rem(t, tiles_per_seq)

    def project_first_subtile(src_ref):
        held = {}

        def chunk(c):
            def run():
                if not held:
                    xs = src_ref[0]
                    ms = jnp.mean(xs * xs, axis=-1, keepdims=True)
                    held["h"] = (xs * lax.rsqrt(ms + RMS_EPS) * ng_ref[...]).astype(BF16)
                c0, c1 = c * PROJ_CHUNK, min((c + 1) * PROJ_CHUNK, IN_WIDTH)
                pfirst_ref[:, c0:c1] = jnp.dot(held["h"], win_ref[:, c0:c1], preferred_element_type=F32)
            return run
        return [chunk(c) for c in range(IN_CHUNKS)]

    @pl.when(t == 0)
    def _():
        for f in project_first_subtile(x_ref.at[:, 0:SUB_TILE, :]):
            f()

    @pl.when(j == 0)
    def _():
        state_ref[...] = jnp.zeros_like(state_ref)
        kprev_ref[...] = jnp.zeros_like(kprev_ref)
        vprev_ref[...] = jnp.zeros_like(vprev_ref)

    lane = lax.broadcasted_iota(jnp.int32, (BLK, LANES), 1)
    lo_half = lane < ATT_HEAD_DIM
    hi_half = jnp.logical_not(lo_half)
    even_lane = (lane & 1) == 0
    lane2 = lax.broadcasted_iota(jnp.int32, (2 * BLK, LANES), 1)
    lo_half2 = lane2 < ATT_HEAD_DIM
    qi = lax.broadcasted_iota(jnp.int32, (BLK, 2 * BLK), 0)
    kj = lax.broadcasted_iota(jnp.int32, (BLK, 2 * BLK), 1)
    diff = qi + BLK - kj
    band = (diff >= 0) & (diff < WINDOW)
    in_cur = kj >= BLK
    ones_lo = jnp.where(lo_half2, 1.0, 0.0).astype(BF16)
    ones_hi = jnp.where(lo_half2, 0.0, 1.0).astype(BF16)
    zeros_bf = jnp.zeros((BLK, LANES), BF16)
    row_lo = lax.broadcasted_iota(jnp.int32, (LANES, LANES), 0) < RET_QK_DIM

    proj_chunks = [dict() for _ in range(N_SUB)]
    out_chunks = [dict() for _ in range(N_SUB)]
    normed = {}
    carry = {}
    att_rhs = {}

    def proj_tile(s, i, col):
        if s == 0:
            return pfirst_ref[i * BLK:(i + 1) * BLK, col:col + LANES]
        chunk, off = divmod(col, PROJ_CHUNK)
        return proj_chunks[s][chunk][i * BLK:(i + 1) * BLK, off:off + LANES]

    def in_proj_chunk(s, c):
        def run():
            if s not in normed:
                xs = x_ref[0, s * SUB_TILE:(s + 1) * SUB_TILE, :]
                ms = jnp.mean(xs * xs, axis=-1, keepdims=True)
                normed[s] = (xs * lax.rsqrt(ms + RMS_EPS) * ng_ref[...]).astype(BF16)
            proj_chunks[s][c] = jnp.dot(normed[s], win_ref[:, c * PROJ_CHUNK:min((c + 1) * PROJ_CHUNK, IN_WIDTH)],
                                        preferred_element_type=F32)
        return run

    def out_proj_chunk(s, c):
        def run():
            out_chunks[s][c] = jnp.dot(mix_refs[s][...], wout_ref[:, c * PROJ_CHUNK:(c + 1) * PROJ_CHUNK],
                                       preferred_element_type=F32)
        return run

    def finish(s):
        def run():
            rows = slice(s * SUB_TILE, (s + 1) * SUB_TILE)
            y = x_ref[0, rows, :] + jnp.concatenate([out_chunks[s][c] for c in range(OUT_CHUNKS)], axis=1)
            ms = jnp.mean(y * y, axis=-1, keepdims=True)
            o_ref[0, rows, :] = y * lax.rsqrt(ms + RMS_EPS) * fg_ref[...]
        return run

    def kv_prep(s, i):
        if s == 0 and i == 0:
            carry["k"] = [kprev_ref[g] for g in range(ATT_KV_HEADS)]
            carry["v"] = [vprev_ref[g] for g in range(ATT_KV_HEADS)]
        k_cur = proj_tile(s, i, OFF_AK)
        v_cur = proj_tile(s, i, OFF_AV)
        v_sw = pltpu.roll(v_cur, ATT_HEAD_DIM, 1)
        k_t = k_cur.T
        kdup_cur = [jnp.concatenate([k_t[g * ATT_HEAD_DIM:(g + 1) * ATT_HEAD_DIM]] * HEADS_PER_TILE,
                                    axis=0).astype(BF16) for g in range(ATT_KV_HEADS)]
        vdup_cur = [jnp.where(lo_half, v_cur, v_sw).astype(BF16),
                    jnp.where(lo_half, v_sw, v_cur).astype(BF16)]
        per_head = []
        for g in range(ATT_KV_HEADS):
            kdup = jnp.concatenate([carry["k"][g], kdup_cur[g]], axis=1)
            vdup = jnp.concatenate([carry["v"][g], vdup_cur[g]], axis=0)
            zero2 = jnp.zeros_like(vdup)
            rhs_v = jnp.concatenate([
                jnp.concatenate([jnp.where(lo_half2, vdup, zero2), ones_lo], axis=1),
                jnp.concatenate([jnp.where(lo_half2, zero2, vdup), ones_hi], axis=1),
            ], axis=0)
            per_head.append((kdup, rhs_v))
        att_rhs[(s, i)] = per_head
        carry["k"], carry["v"] = kdup_cur, vdup_cur

    stash = {}

    def att_scores(s, i):
        def run():
            kv_prep(s, i)
            valid = band & (in_cur | (j > 0)) if (s == 0 and i == 0) else band
            scores = []
            for p in range(ATT_PAIRS):
                kdup, _ = att_rhs[(s, i)][p // (GROUP // HEADS_PER_TILE)]
                q2 = (proj_tile(s, i, OFF_AQ + p * LANES) * (ATT_HEAD_DIM ** -0.5 * LOG2E)).astype(BF16)
                for e in range(HEADS_PER_TILE):
                    qm = jnp.where(lo_half if e == 0 else hi_half, q2, zeros_bf)
                    sc = jnp.dot(qm, kdup, preferred_element_type=F32)
                    scores.append(jnp.where(valid, sc, NEG_INF))
            stash[("sc", s, i)] = scores
        return run

    def att_softmax(s, i):
        def run():
            probs, sink_terms = [], []
            for head, sc in enumerate(stash.pop(("sc", s, i))):
                sink = sinks_ref[head] * LOG2E
                m = jnp.maximum(jnp.max(sc, axis=-1, keepdims=True), sink)
                probs.append(jnp.exp2(sc - m).astype(BF16))
                sink_terms.append(jnp.exp2(sink - m))
            stash[("p", s, i)] = (probs, sink_terms)
        return run

    def att_values(s, i):
        def run():
            probs, sink_terms = stash.pop(("p", s, i))
            for p in range(ATT_PAIRS):
                _, rhs_v = att_rhs[(s, i)][p // (GROUP // HEADS_PER_TILE)]
                h0 = p * HEADS_PER_TILE
                lhs = jnp.concatenate(probs[h0:h0 + HEADS_PER_TILE], axis=1)
                res = jnp.dot(lhs, rhs_v, preferred_element_type=F32)
                den = res[:, LANES:] + jnp.where(lo_half, sink_terms[h0], sink_terms[h0 + 1])
                a2 = res[:, :LANES] / den
                z2 = proj_tile(s, i, OFF_AZ + p * LANES)
                mix_refs[s][i * BLK:(i + 1) * BLK, p * LANES:(p + 1) * LANES] = (a2 * _silu(z2)).astype(BF16)
        return run

    def ret_scores(s, i):
        def run():
            r0 = (s * BLKS_PER_SUB + i) * BLK
            cos = cos_ref[r0:r0 + BLK, :]
            sin = sin_ref[r0:r0 + BLK, :]

            def rotate(t):
                swapped = jnp.where(even_lane, pltpu.roll(t, LANES - 1, 1), pltpu.roll(t, 1, 1))
                return t * cos + swapped * sin

            per_pair = []
            for p in range(RET_PAIRS):
                q2 = rotate(proj_tile(s, i, OFF_RQ + p * LANES))
                k2 = rotate(proj_tile(s, i, OFF_RK + p * LANES)) * (RET_QK_DIM ** -0.5)
                q2b = q2.astype(BF16)
                qd2b = (q2 * qdec_ref[p]).astype(BF16)
                k2t = k2.T
                k2tb = k2t.astype(BF16)
                kd2tb = (k2t * kdec_ref[p]).astype(BF16)
                rhs_qk = jnp.concatenate([jnp.where(row_lo, k2tb, zeros_bf),
                                          jnp.where(row_lo, zeros_bf, k2tb)], axis=1)
                sc2 = jnp.dot(q2b, rhs_qk, preferred_element_type=F32)
                lhs = []
                for e in range(HEADS_PER_TILE):
                    sel = lo_half if e == 0 else hi_half
                    sc = sc2[:, e * BLK:(e + 1) * BLK] * din_ref[p * HEADS_PER_TILE + e]
                    lhs.append(jnp.concatenate([sc.astype(BF16), jnp.where(sel, qd2b, zeros_bf)], axis=1))
                per_pair.append((lhs, kd2tb))
            stash[("ret", s, i)] = per_pair
        return run

    def ret_outputs(s, i):
        def run():
            per_pair = stash.pop(("ret", s, i))
            for p in range(RET_PAIRS):
                lhs, kd2tb = per_pair[p]
                st = state_ref[p * LANES:(p + 1) * LANES, :]
                st_b = st.astype(BF16)
                v_heads = [proj_tile(s, i, OFF_RV + (p * HEADS_PER_TILE + e) * LANES).astype(BF16)
                           for e in range(HEADS_PER_TILE)]
                for e in range(HEADS_PER_TILE):
                    head = p * HEADS_PER_TILE + e
                    rhs = jnp.concatenate([v_heads[e], st_b], axis=0)
                    o = jnp.dot(lhs[e], rhs, preferred_element_type=F32)
                    mu = jnp.mean(o, axis=-1, keepdims=True)
                    oc = o - mu
                    var = jnp.mean(oc * oc, axis=-1, keepdims=True)
                    on = oc * lax.rsqrt(var + GN_EPS) * gng_ref[:, head * LANES:(head + 1) * LANES]
                    rz = proj_tile(s, i, OFF_RZ + head * LANES)
                    mc = ATT_WIDTH + head * LANES
                    mix_refs[s][i * BLK:(i + 1) * BLK, mc:mc + LANES] = (on * _silu(rz)).astype(BF16)
                kv = jnp.dot(kd2tb, jnp.concatenate(v_heads, axis=1),
                             preferred_element_type=F32)
                kv_sel = jnp.where(row_lo, kv[:, :LANES], kv[:, LANES:])
                state_ref[p * LANES:(p + 1) * LANES, :] = st * cdec_ref[p * LANES:(p + 1) * LANES, :] + kv_sel
        return run

    def mixer_units(s):
        units = []
        for stage in (att_scores, ret_scores, att_softmax, ret_outputs, att_values):
            units += [stage(s, i) for i in range(BLKS_PER_SUB)]
        return units

    for s in range(N_SUB):
        fillers = []
        if s >= 1:
            fillers += [out_proj_chunk(s - 1, c) for c in range(OUT_CHUNKS)] + [finish(s - 1)]
        if s + 1 < N_SUB:
            fillers += [in_proj_chunk(s + 1, c) for c in range(IN_CHUNKS)]
        else:
            fillers += project_first_subtile(xnext_ref)
        _interleave(mixer_units(s), fillers)
    for c in range(OUT_CHUNKS):
        out_proj_chunk(N_SUB - 1, c)()
    finish(N_SUB - 1)()

    for g in range(ATT_KV_HEADS):
        kprev_ref[g] = carry["k"][g]
        vprev_ref[g] = carry["v"][g]


def _retention_tables(seq):
    pos = np.arange(seq, dtype=np.float64)
    theta = 1.0 / (ROT_BASE ** np.linspace(0.0, 1.0, RET_QK_DIM // 2))
    ang = pos[:, None] * theta[None, :]
    cos, sin = np.cos(ang), np.sin(ang)
    cos_l = np.tile(np.repeat(cos, 2, axis=1), (1, HEADS_PER_TILE))
    sign = np.tile(np.array([-1.0, 1.0]), RET_QK_DIM // 2)
    sin_l = np.tile(np.repeat(sin, 2, axis=1) * sign[None, :], (1, HEADS_PER_TILE))

    log_gamma = np.log(1.0 - 2.0 ** (-5.0 - np.arange(RET_HEADS, dtype=np.float64)))
    idx = np.arange(BLK, dtype=np.float64)
    rel = idx[:, None] - idx[None, :]
    decay_in = np.where(rel >= 0, np.exp(log_gamma[:, None, None] * np.maximum(rel, 0.0)), 0.0)
    k_dec = np.exp(log_gamma[:, None] * (BLK - 1 - idx)[None, :])
    q_dec = np.exp(log_gamma[:, None] * (idx + 1)[None, :])
    chunk_decay = np.exp(log_gamma * BLK)

    def per_pair(dec):
        t = np.repeat(dec.T[:, :, None], RET_QK_DIM, axis=2)
        t = t.reshape(BLK, RET_PAIRS, LANES)
        return np.transpose(t, (1, 0, 2))

    cdec = np.broadcast_to(np.repeat(chunk_decay, RET_QK_DIM)[:, None],
                           (RET_QK_WIDTH, RET_V_DIM))
    k_dec_t = np.transpose(per_pair(k_dec), (0, 2, 1))
    tables = (cos_l, sin_l, decay_in, per_pair(q_dec), k_dec_t, cdec)
    return tuple(jnp.asarray(np.ascontiguousarray(a, dtype=np.float32)) for a in tables)


def _full(shape):
    return pl.BlockSpec(shape, lambda t: (0,) * len(shape))


@jax.jit
def kernel(x, norm_g, w_in, att_sinks, ret_gn_g, w_out, final_g):
    batch, seq, d = x.shape
    depth = w_in.shape[0]
    assert depth == 1 and d == D_MODEL and seq % SEQ_TILE == 0
    cos_l, sin_l, decay_in, qdec, kdec, cdec = _retention_tables(seq)
    tile = SEQ_TILE
    n_seq = seq // tile
    n_tiles = batch * n_seq

    def this_tile(t):
        return (t // n_seq, t % n_seq, 0)

    def next_first_subtile(t):
        u = jnp.minimum(t + 1, n_tiles - 1)
        return (u // n_seq, (u % n_seq) * N_SUB, 0)

    call = pl.pallas_call(
        functools.partial(_layer_kernel, n_seq),
        grid=(n_tiles,),
        in_specs=[
            pl.BlockSpec(memory_space=pltpu.SMEM),
            pl.BlockSpec((1, tile, D_MODEL), this_tile),
            pl.BlockSpec((1, SUB_TILE, D_MODEL), next_first_subtile),
            _full((1, D_MODEL)),
            _full((D_MODEL, IN_WIDTH)),
            _full((1, RET_WIDTH)),
            _full((MIX_WIDTH, D_MODEL)),
            _full((1, D_MODEL)),
            pl.BlockSpec((tile, LANES), lambda t: (t % n_seq, 0)),
            pl.BlockSpec((tile, LANES), lambda t: (t % n_seq, 0)),
            _full((RET_HEADS, BLK, BLK)),
            _full((RET_PAIRS, BLK, LANES)),
            _full((RET_PAIRS, BLK, LANES)),
            _full((RET_QK_WIDTH, RET_V_DIM)),
        ],
        out_specs=pl.BlockSpec((1, tile, D_MODEL), this_tile),
        out_shape=jax.ShapeDtypeStruct(x.shape, x.dtype),
        scratch_shapes=[
            pltpu.VMEM((RET_QK_WIDTH, RET_V_DIM), F32),
            pltpu.VMEM((ATT_KV_HEADS, BLK, LANES), BF16),
            pltpu.VMEM((ATT_KV_HEADS, BLK, LANES), BF16),
            pltpu.VMEM((SUB_TILE, IN_WIDTH), F32),
        ] + [pltpu.VMEM((SUB_TILE, MIX_WIDTH), BF16)] * N_SUB,
        compiler_params=pltpu.CompilerParams(
            dimension_semantics=("arbitrary",),
            vmem_limit_bytes=VMEM_LIMIT_BYTES),
        name="hymba_layer",
    )
    return call(att_sinks[0], x, x, norm_g[0][None, :], w_in[0].astype(BF16),
                ret_gn_g[0][None, :], w_out[0].astype(BF16), final_g[None, :],
                cos_l, sin_l, decay_in, qdec, kdec, cdec)
```

```python
import functools
import math

import jax
import jax.numpy as jnp
import numpy as np
from jax import lax
from jax.experimental import pallas as pl
from jax.experimental.pallas import tpu as pltpu

D_MODEL = 1024
ATT_HEADS = 8
ATT_KV_HEADS = 2
ATT_HEAD_DIM = 64
WINDOW = 128
BLK = 128
RET_HEADS = 4
RET_QK_DIM = 64
RET_V_DIM = 128
ROT_BASE = 10000.0
RMS_EPS = 1e-6
GN_EPS = 1e-6
NEG_INF = -1e30
LOG2E = 1.4426950408889634

ATT_WIDTH = ATT_HEADS * ATT_HEAD_DIM
ATT_KV_WIDTH = ATT_KV_HEADS * ATT_HEAD_DIM
RET_QK_WIDTH = RET_HEADS * RET_QK_DIM
RET_WIDTH = RET_HEADS * RET_V_DIM
MIX_WIDTH = ATT_WIDTH + RET_WIDTH
IN_WIDTH = 2 * ATT_WIDTH + 2 * ATT_KV_WIDTH + 2 * RET_QK_WIDTH + 2 * RET_WIDTH

OFF_AQ = 0
OFF_AK = OFF_AQ + ATT_WIDTH
OFF_AV = OFF_AK + ATT_KV_WIDTH
OFF_AZ = OFF_AV + ATT_KV_WIDTH
OFF_RQ = OFF_AZ + ATT_WIDTH
OFF_RK = OFF_RQ + RET_QK_WIDTH
OFF_RV = OFF_RK + RET_QK_WIDTH
OFF_RZ = OFF_RV + RET_WIDTH

LANES = 128
MXU_COLS = 256
N_MXU = 2
PROJ_CHUNK = N_MXU * MXU_COLS
HEADS_PER_TILE = LANES // ATT_HEAD_DIM
ATT_PAIRS = ATT_HEADS // HEADS_PER_TILE
RET_PAIRS = RET_HEADS // HEADS_PER_TILE
GROUP = ATT_HEADS // ATT_KV_HEADS

SEQ_TILE = 1024
SUB_TILE = 256
OUT_TILE = 256
N_SUB = SEQ_TILE // SUB_TILE
BLKS_PER_SUB = SUB_TILE // BLK
OUT_PARTS = SUB_TILE // OUT_TILE
IN_CHUNKS = -(-IN_WIDTH // PROJ_CHUNK)
OUT_CHUNKS = D_MODEL // PROJ_CHUNK
VMEM_LIMIT_BYTES = 56 * 1024 * 1024
W_ATT_SCORES, W_RET_SCORES, W_ATT_SOFTMAX, W_RET_OUTPUTS, W_ATT_VALUES = 1.0, 1.0, 1.0, 1.0, 1.0
FILLERS_AHEAD = 0

F32 = jnp.float32
BF16 = jnp.bfloat16


def _silu(z):
    hz = 0.5 * z
    return hz + hz * jnp.tanh(hz)


def _interleave(units, fillers):
    total = sum(w for _, w in units)
    done, acc = 0, 0.0
    for f in fillers[:FILLERS_AHEAD]:
        f()
        done += 1
    for unit, w in units:
        unit()
        acc += w
        due = FILLERS_AHEAD + math.ceil(acc * (len(fillers) - FILLERS_AHEAD) / total - 1e-9)
        while done < due:
            fillers[done]()
            done += 1
    for f in fillers[done:]:
        f()


def _layer_kernel(tiles_per_seq, sinks_ref, x_ref, xnext_ref, ng_ref, win_ref, gng_ref, wout_ref, fg_ref,
                  cos_ref, sin_ref, din_ref, qdec_ref, kdec_ref, cdec_ref,
                  o_ref, state_ref, kprev_ref, vprev_ref, pfirst_ref, *mix_refs):
    t = pl.program_id(0)
    j = lax.rem(t, tiles_per_seq)

    def project_first_subtile(src_ref):
        held = {}

        def chunk(c):
            def run():
                if not held:
                    xs = src_ref[0]
                    ms = jnp.mean(xs * xs, axis=-1, keepdims=True)
                    held["h"] = (xs * lax.rsqrt(ms + RMS_EPS) * ng_ref[...]).astype(BF16)
                c0, c1 = c * PROJ_CHUNK, min((c + 1) * PROJ_CHUNK, IN_WIDTH)
                pfirst_ref[:, c0:c1] = jnp.dot(held["h"], win_ref[:, c0:c1], preferred_element_type=F32)
            return run
        return [chunk(c) for c in range(IN_CHUNKS)]

    @pl.when(t == 0)
    def _():
        for f in project_first_subtile(x_ref.at[:, 0:SUB_TILE, :]):
            f()

    @pl.when(j == 0)
    def _():
        state_ref[...] = jnp.zeros_like(state_ref)
        kprev_ref[...] = jnp.zeros_like(kprev_ref)
        vprev_ref[...] = jnp.zeros_like(vprev_ref)

    lane = lax.broadcasted_iota(jnp.int32, (BLK, LANES), 1)
    lo_half = lane < ATT_HEAD_DIM
    hi_half = jnp.logical_not(lo_half)
    even_lane = (lane & 1) == 0
    lane2 = lax.broadcasted_iota(jnp.int32, (2 * BLK, LANES), 1)
    lo_half2 = lane2 < ATT_HEAD_DIM
    qi = lax.broadcasted_iota(jnp.int32, (BLK, 2 * BLK), 0)
    kj = lax.broadcasted_iota(jnp.int32, (BLK, 2 * BLK), 1)
    diff = qi + BLK - kj
    band = (diff >= 0) & (diff < WINDOW)
    in_cur = kj >= BLK
    ones_lo = jnp.where(lo_half2, 1.0, 0.0).astype(BF16)
    ones_hi = jnp.where(lo_half2, 0.0, 1.0).astype(BF16)
    zeros_bf = jnp.zeros((BLK, LANES), BF16)
    row_lo = lax.broadcasted_iota(jnp.int32, (LANES, LANES), 0) < RET_QK_DIM

    proj_chunks = [dict() for _ in range(N_SUB)]
    out_chunks = [dict() for _ in range(N_SUB)]
    normed = {}
    carry = {}
    att_rhs = {}

    def proj_tile(s, i, col):
        if s == 0:
            return pfirst_ref[i * BLK:(i + 1) * BLK, col:col + LANES]
        chunk, off = divmod(col, PROJ_CHUNK)
        return proj_chunks[s][chunk][i * BLK:(i + 1) * BLK, off:off + LANES]

    def in_proj_chunk(s, c):
        def run():
            if s not in normed:
                xs = x_ref[0, s * SUB_TILE:(s + 1) * SUB_TILE, :]
                ms = jnp.mean(xs * xs, axis=-1, keepdims=True)
                normed[s] = (xs * lax.rsqrt(ms + RMS_EPS) * ng_ref[...]).astype(BF16)
            proj_chunks[s][c] = jnp.dot(normed[s], win_ref[:, c * PROJ_CHUNK:min((c + 1) * PROJ_CHUNK, IN_WIDTH)],
                                        preferred_element_type=F32)
        return run

    def out_proj_chunk(s, r, c):
        def run():
            out_chunks[s][(r, c)] = jnp.dot(mix_refs[s][r * OUT_TILE:(r + 1) * OUT_TILE, :],
                                            wout_ref[:, c * PROJ_CHUNK:(c + 1) * PROJ_CHUNK],
                                            preferred_element_type=F32)
        return run

    def finish(s, r):
        def run():
            rows = slice(s * SUB_TILE + r * OUT_TILE, s * SUB_TILE + (r + 1) * OUT_TILE)
            y = x_ref[0, rows, :] + jnp.concatenate([out_chunks[s][(r, c)] for c in range(OUT_CHUNKS)], axis=1)
            ms = jnp.mean(y * y, axis=-1, keepdims=True)
            o_ref[0, rows, :] = y * lax.rsqrt(ms + RMS_EPS) * fg_ref[...]
        return run

    def out_fillers(s):
        fillers = []
        for r in range(OUT_PARTS):
            fillers += [out_proj_chunk(s, r, c) for c in range(OUT_CHUNKS)] + [finish(s, r)]
        return fillers

    def kv_prep(s, i):
        if s == 0 and i == 0:
            carry["k"] = [kprev_ref[g] for g in range(ATT_KV_HEADS)]
            carry["v"] = [vprev_ref[g] for g in range(ATT_KV_HEADS)]
        k_cur = proj_tile(s, i, OFF_AK)
        v_cur = proj_tile(s, i, OFF_AV)
        k_t = k_cur.T
        kdup_cur = [jnp.concatenate([k_t[g * ATT_HEAD_DIM:(g + 1) * ATT_HEAD_DIM]] * HEADS_PER_TILE,
                                    axis=0).astype(BF16) for g in range(ATT_KV_HEADS)]
        v_sw = pltpu.roll(v_cur, ATT_HEAD_DIM, 1)
        vdup_cur = [jnp.where(lo_half, v_cur, v_sw).astype(BF16),
                    jnp.where(lo_half, v_sw, v_cur).astype(BF16)]
        per_head = []
        for g in range(ATT_KV_HEADS):
            kdup = jnp.concatenate([carry["k"][g], kdup_cur[g]], axis=1)
            vdup = jnp.concatenate([carry["v"][g], vdup_cur[g]], axis=0)
            zero2 = jnp.zeros_like(vdup)
            rhs_v = jnp.concatenate([
                jnp.concatenate([jnp.where(lo_half2, vdup, zero2), ones_lo], axis=1),
                jnp.concatenate([jnp.where(lo_half2, zero2, vdup), ones_hi], axis=1),
            ], axis=0)
            per_head.append((kdup, rhs_v))
        att_rhs[(s, i)] = per_head
        carry["k"], carry["v"] = kdup_cur, vdup_cur

    stash = {}

    def att_scores(s, i):
        def run():
            kv_prep(s, i)
            valid = band & (in_cur | (j > 0)) if (s == 0 and i == 0) else band
            scores = []
            for g in range(ATT_KV_HEADS):
                kdup, _ = att_rhs[(s, i)][g]
                q_rows = []
                for p in range(g * GROUP // HEADS_PER_TILE, (g + 1) * GROUP // HEADS_PER_TILE):
                    q2 = (proj_tile(s, i, OFF_AQ + p * LANES) * (ATT_HEAD_DIM ** -0.5 * LOG2E)).astype(BF16)
                    q_rows += [jnp.where(lo_half, q2, zeros_bf), jnp.where(hi_half, q2, zeros_bf)]
                sc_g = jnp.dot(jnp.concatenate(q_rows, axis=0), kdup, preferred_element_type=F32)
                scores += [jnp.where(valid, sc_g[h * BLK:(h + 1) * BLK], NEG_INF) for h in range(GROUP)]
            stash[("sc", s, i)] = scores
        return run

    def att_softmax(s, i):
        def run():
            probs, sink_terms = [], []
            for head, sc in enumerate(stash.pop(("sc", s, i))):
                sink = sinks_ref[head] * LOG2E
                m = jnp.maximum(jnp.max(sc, axis=-1, keepdims=True), sink)
                probs.append(jnp.exp2(sc - m).astype(BF16))
                sink_terms.append(jnp.exp2(sink - m))
            stash[("p", s, i)] = (probs, sink_terms)
        return run

    def att_values(s, i):
        def run():
            probs, sink_terms = stash.pop(("p", s, i))
            pairs_per_group = GROUP // HEADS_PER_TILE
            for g in range(ATT_KV_HEADS):
                _, rhs_v = att_rhs[(s, i)][g]
                lhs = jnp.concatenate(
                    [jnp.concatenate(probs[(g * pairs_per_group + pp) * HEADS_PER_TILE:
                                           (g * pairs_per_group + pp + 1) * HEADS_PER_TILE], axis=1)
                     for pp in range(pairs_per_group)], axis=0)
                res_g = jnp.dot(lhs, rhs_v, preferred_element_type=F32)
                for pp in range(pairs_per_group):
                    p = g * pairs_per_group + pp
                    h0 = p * HEADS_PER_TILE
                    res = res_g[pp * BLK:(pp + 1) * BLK]
                    den = res[:, LANES:] + jnp.where(lo_half, sink_terms[h0], sink_terms[h0 + 1])
                    a2 = res[:, :LANES] / den
                    z2 = proj_tile(s, i, OFF_AZ + p * LANES)
                    mix_refs[s][i * BLK:(i + 1) * BLK, p * LANES:(p + 1) * LANES] = (a2 * _silu(z2)).astype(BF16)
        return run

    def ret_scores(s, i):
        def run():
            r0 = (s * BLKS_PER_SUB + i) * BLK
            cos = cos_ref[r0:r0 + BLK, :]
            sin = sin_ref[r0:r0 + BLK, :]

            def rotate(t):
                swapped = jnp.where(even_lane, pltpu.roll(t, LANES - 1, 1), pltpu.roll(t, 1, 1))
                return t * cos + swapped * sin

            per_pair = []
            for p in range(RET_PAIRS):
                q2 = rotate(proj_tile(s, i, OFF_RQ + p * LANES))
                k2 = rotate(proj_tile(s, i, OFF_RK + p * LANES)) * (RET_QK_DIM ** -0.5)
                q2b = q2.astype(BF16)
                qd2b = (q2 * qdec_ref[p]).astype(BF16)
                k2t = k2.T
                k2tb = k2t.astype(BF16)
                kd2tb = (k2t * kdec_ref[p]).astype(BF16)
                rhs_qk = jnp.concatenate([jnp.where(row_lo, k2tb, zeros_bf),
                                          jnp.where(row_lo, zeros_bf, k2tb)], axis=1)
                sc2 = jnp.dot(q2b, rhs_qk, preferred_element_type=F32)
                lhs = []
                for e in range(HEADS_PER_TILE):
                    sel = lo_half if e == 0 else hi_half
                    sc = sc2[:, e * BLK:(e + 1) * BLK] * din_ref[p * HEADS_PER_TILE + e]
                    lhs.append(jnp.concatenate([sc.astype(BF16), jnp.where(sel, qd2b, zeros_bf)], axis=1))
                per_pair.append((lhs, kd2tb))
            stash[("ret", s, i)] = per_pair
        return run

    def ret_outputs(s, i):
        def run():
            per_pair = stash.pop(("ret", s, i))
            for p in range(RET_PAIRS):
                lhs, kd2tb = per_pair[p]
                st = state_ref[p * LANES:(p + 1) * LANES, :]
                st_b = st.astype(BF16)
                v_heads = [proj_tile(s, i, OFF_RV + (p * HEADS_PER_TILE + e) * LANES).astype(BF16)
                           for e in range(HEADS_PER_TILE)]
                for e in range(HEADS_PER_TILE):
                    head = p * HEADS_PER_TILE + e
                    rhs = jnp.concatenate([v_heads[e], st_b], axis=0)
                    o = jnp.dot(lhs[e], rhs, preferred_element_type=F32)
                    mu = jnp.mean(o, axis=-1, keepdims=True)
                    oc = o - mu
                    var = jnp.mean(oc * oc, axis=-1, keepdims=True)
                    on = oc * lax.rsqrt(var + GN_EPS) * gng_ref[:, head * LANES:(head + 1) * LANES]
                    rz = proj_tile(s, i, OFF_RZ + head * LANES)
                    mc = ATT_WIDTH + head * LANES
                    mix_refs[s][i * BLK:(i + 1) * BLK, mc:mc + LANES] = (on * _silu(rz)).astype(BF16)
                kv = jnp.dot(kd2tb, jnp.concatenate(v_heads, axis=1),
                             preferred_element_type=F32)
                kv_sel = jnp.where(row_lo, kv[:, :LANES], kv[:, LANES:])
                state_ref[p * LANES:(p + 1) * LANES, :] = st * cdec_ref[p * LANES:(p + 1) * LANES, :] + kv_sel
        return run

    def mixer_units(s):
        units = []
        for stage, weight in ((att_scores, W_ATT_SCORES), (ret_scores, W_RET_SCORES), (att_softmax, W_ATT_SOFTMAX),
                              (ret_outputs, W_RET_OUTPUTS), (att_values, W_ATT_VALUES)):
            units += [(stage(s, i), weight) for i in range(BLKS_PER_SUB)]
        return units

    for s in range(N_SUB):
        fillers = []
        if s >= 1:
            fillers += out_fillers(s - 1)
        if s + 1 < N_SUB:
            fillers += [in_proj_chunk(s + 1, c) for c in range(IN_CHUNKS)]
        else:
            fillers += project_first_subtile(xnext_ref)
        _interleave(mixer_units(s), fillers)
    tail = out_fillers(N_SUB - 1)
    per_part = OUT_CHUNKS + 1
    order = list(range(OUT_CHUNKS))
    for r in range(1, OUT_PARTS):
        order += [r * per_part + c for c in range(OUT_CHUNKS)] + [(r - 1) * per_part + OUT_CHUNKS]
    order.append((OUT_PARTS - 1) * per_part + OUT_CHUNKS)
    for k in order:
        tail[k]()

    for g in range(ATT_KV_HEADS):
        kprev_ref[g] = carry["k"][g]
        vprev_ref[g] = carry["v"][g]


def _retention_tables(seq):
    pos = np.arange(seq, dtype=np.float64)
    theta = 1.0 / (ROT_BASE ** np.linspace(0.0, 1.0, RET_QK_DIM // 2))
    ang = pos[:, None] * theta[None, :]
    cos, sin = np.cos(ang), np.sin(ang)
    cos_l = np.tile(np.repeat(cos, 2, axis=1), (1, HEADS_PER_TILE))
    sign = np.tile(np.array([-1.0, 1.0]), RET_QK_DIM // 2)
    sin_l = np.tile(np.repeat(sin, 2, axis=1) * sign[None, :], (1, HEADS_PER_TILE))

    log_gamma = np.log(1.0 - 2.0 ** (-5.0 - np.arange(RET_HEADS, dtype=np.float64)))
    idx = np.arange(BLK, dtype=np.float64)
    rel = idx[:, None] - idx[None, :]
    decay_in = np.where(rel >= 0, np.exp(log_gamma[:, None, None] * np.maximum(rel, 0.0)), 0.0)
    k_dec = np.exp(log_gamma[:, None] * (BLK - 1 - idx)[None, :])
    q_dec = np.exp(log_gamma[:, None] * (idx + 1)[None, :])
    chunk_decay = np.exp(log_gamma * BLK)

    def per_pair(dec):
        t = np.repeat(dec.T[:, :, None], RET_QK_DIM, axis=2)
        t = t.reshape(BLK, RET_PAIRS, LANES)
        return np.transpose(t, (1, 0, 2))

    cdec = np.broadcast_to(np.repeat(chunk_decay, RET_QK_DIM)[:, None],
                           (RET_QK_WIDTH, RET_V_DIM))
    k_dec_t = np.transpose(per_pair(k_dec), (0, 2, 1))
    tables = (cos_l, sin_l, decay_in, per_pair(q_dec), k_dec_t, cdec)
    return tuple(jnp.asarray(np.ascontiguousarray(a, dtype=np.float32)) for a in tables)


def _full(shape):
    return pl.BlockSpec(shape, lambda t: (0,) * len(shape))


@jax.jit
def kernel(x, norm_g, w_in, att_sinks, ret_gn_g, w_out, final_g):
    batch, seq, d = x.shape
    depth = w_in.shape[0]
    assert depth == 1 and d == D_MODEL and seq % SEQ_TILE == 0
    cos_l, sin_l, decay_in, qdec, kdec, cdec = _retention_tables(seq)
    tile = SEQ_TILE
    n_seq = seq // tile
    n_tiles = batch * n_seq

    def this_tile(t):
        return (t // n_seq, t % n_seq, 0)

    def next_first_subtile(t):
        u = jnp.minimum(t + 1, n_tiles - 1)
        return (u // n_seq, (u % n_seq) * N_SUB, 0)

    call = pl.pallas_call(
        functools.partial(_layer_kernel, n_seq),
        grid=(n_tiles,),
        in_specs=[
            pl.BlockSpec(memory_space=pltpu.SMEM),
            pl.BlockSpec((1, tile, D_MODEL), this_tile),
            pl.BlockSpec((1, SUB_TILE, D_MODEL), next_first_subtile),
            _full((1, D_MODEL)),
            _full((D_MODEL, IN_WIDTH)),
            _full((1, RET_WIDTH)),
            _full((MIX_WIDTH, D_MODEL)),
            _full((1, D_MODEL)),
            pl.BlockSpec((tile, LANES), lambda t: (t % n_seq, 0)),
            pl.BlockSpec((tile, LANES), lambda t: (t % n_seq, 0)),
            _full((RET_HEADS, BLK, BLK)),
            _full((RET_PAIRS, BLK, LANES)),
            _full((RET_PAIRS, BLK, LANES)),
            _full((RET_QK_WIDTH, RET_V_DIM)),
        ],
        out_specs=pl.BlockSpec((1, tile, D_MODEL), this_tile),
        out_shape=jax.ShapeDtypeStruct(x.shape, x.dtype),
        scratch_shapes=[
            pltpu.VMEM((RET_QK_WIDTH, RET_V_DIM), F32),
            pltpu.VMEM((ATT_KV_HEADS, BLK, LANES), BF16),
            pltpu.VMEM((ATT_KV_HEADS, BLK, LANES), BF16),
            pltpu.VMEM((SUB_TILE, IN_WIDTH), F32),
        ] + [pltpu.VMEM((SUB_TILE, MIX_WIDTH), BF16)] * N_SUB,
        compiler_params=pltpu.CompilerParams(
            dimension_semantics=("arbitrary",),
            vmem_limit_bytes=VMEM_LIMIT_BYTES),
        name="hymba_layer",
    )
    return call(att_sinks[0], x, x, norm_g[0][None, :], w_in[0].astype(BF16),
                ret_gn_g[0][None, :], w_out[0].astype(BF16), final_g[None, :],
                cos_l, sin_l, decay_in, qdec, kdec, cdec)
```

```python
import functools

import jax
import jax.numpy as jnp
import numpy as np
from jax import lax
from jax.experimental import pallas as pl
from jax.experimental.pallas import tpu as pltpu

D_MODEL = 1024
ATT_HEADS = 8
ATT_KV_HEADS = 2
ATT_HEAD_DIM = 64
WINDOW = 128
BLK = 128
RET_HEADS = 4
RET_QK_DIM = 64
RET_V_DIM = 128
ROT_BASE = 10000.0
RMS_EPS = 1e-6
GN_EPS = 1e-6
NEG_INF = -1e30
LOG2E = 1.4426950408889634

ATT_WIDTH = ATT_HEADS * ATT_HEAD_DIM
ATT_KV_WIDTH = ATT_KV_HEADS * ATT_HEAD_DIM
RET_QK_WIDTH = RET_HEADS * RET_QK_DIM
RET_WIDTH = RET_HEADS * RET_V_DIM
MIX_WIDTH = ATT_WIDTH + RET_WIDTH
IN_WIDTH = 2 * ATT_WIDTH + 2 * ATT_KV_WIDTH + 2 * RET_QK_WIDTH + 2 * RET_WIDTH

OFF_AQ = 0
OFF_AK = OFF_AQ + ATT_WIDTH
OFF_AV = OFF_AK + ATT_KV_WIDTH
OFF_AZ = OFF_AV + ATT_KV_WIDTH
OFF_RQ = OFF_AZ + ATT_WIDTH
OFF_RK = OFF_RQ + RET_QK_WIDTH
OFF_RV = OFF_RK + RET_QK_WIDTH
OFF_RZ = OFF_RV + RET_WIDTH

LANES = 128
MXU_COLS = 256
N_MXU = 2
PROJ_CHUNK = N_MXU * MXU_COLS
HEADS_PER_TILE = LANES // ATT_HEAD_DIM
ATT_PAIRS = ATT_HEADS // HEADS_PER_TILE
RET_PAIRS = RET_HEADS // HEADS_PER_TILE
GROUP = ATT_HEADS // ATT_KV_HEADS

SEQ_TILE = 1024
SUB_TILE = 256
N_SUB = SEQ_TILE // SUB_TILE
BLKS_PER_SUB = SUB_TILE // BLK
IN_CHUNKS = -(-IN_WIDTH // PROJ_CHUNK)
OUT_CHUNKS = D_MODEL // PROJ_CHUNK
WEIGHT_SLAB = 256
CAST_STEPS = D_MODEL // WEIGHT_SLAB
VMEM_LIMIT_BYTES = 56 * 1024 * 1024

F32 = jnp.float32
BF16 = jnp.bfloat16


def _silu(z):
    hz = 0.5 * z
    return hz + hz * jnp.tanh(hz)


def _interleave(units, fillers):
    done = 0
    for k, unit in enumerate(units):
        unit()
        due = -(-(k + 1) * len(fillers) // len(units))
        while done < due:
            fillers[done]()
            done += 1
    for f in fillers[done:]:
        f()


def _layer_kernel(tiles_per_seq, sinks_ref, x_ref, xnext_ref, ng_ref, win_slab_ref, gng_ref, wout_slab_ref, fg_ref,
                  cos_ref, sin_ref, din_ref, qdec_ref, kdec_ref, cdec_ref,
                  o_ref, state_ref, kprev_ref, vprev_ref, pfirst_ref, win_ref, wout_ref, *mix_refs):
    step = pl.program_id(0)

    @pl.when(step < CAST_STEPS)
    def _():
        rows = pl.ds(pl.multiple_of(step * WEIGHT_SLAB, WEIGHT_SLAB), WEIGHT_SLAB)
        win_ref[rows, :] = win_slab_ref[...].astype(BF16)
        wout_ref[rows, :] = wout_slab_ref[...].astype(BF16)

    @pl.when(step >= CAST_STEPS)
    def _():
        _tile_step(step - CAST_STEPS, tiles_per_seq, sinks_ref, x_ref, xnext_ref, ng_ref, win_ref, gng_ref, wout_ref,
                   fg_ref, cos_ref, sin_ref, din_ref, qdec_ref, kdec_ref, cdec_ref,
                   o_ref, state_ref, kprev_ref, vprev_ref, pfirst_ref, mix_refs)


def _tile_step(t, tiles_per_seq, sinks_ref, x_ref, xnext_ref, ng_ref, win_ref, gng_ref, wout_ref, fg_ref,
               cos_ref, sin_ref, din_ref, qdec_ref, kdec_ref, cdec_ref,
               o_ref, state_ref, kprev_ref, vprev_ref, pfirst_ref, mix_refs):
    j = lax.rem(t, tiles_per_seq)

    def project_first_subtile(src_ref):
        held = {}

        def chunk(c):
            def run():
                if not held:
                    xs = src_ref[0]
                    ms = jnp.mean(xs * xs, axis=-1, keepdims=True)
                    held["h"] = (xs * lax.rsqrt(ms + RMS_EPS) * ng_ref[...]).astype(BF16)
                c0, c1 = c * PROJ_CHUNK, min((c + 1) * PROJ_CHUNK, IN_WIDTH)
                pfirst_ref[:, c0:c1] = jnp.dot(held["h"], win_ref[:, c0:c1], preferred_element_type=F32)
            return run
        return [chunk(c) for c in range(IN_CHUNKS)]

    @pl.when(t == 0)
    def _():
        for f in project_first_subtile(x_ref.at[:, 0:SUB_TILE, :]):
            f()

    @pl.when(j == 0)
    def _():
        state_ref[...] = jnp.zeros_like(state_ref)
        kprev_ref[...] = jnp.zeros_like(kprev_ref)
        vprev_ref[...] = jnp.zeros_like(vprev_ref)

    lane = lax.broadcasted_iota(jnp.int32, (BLK, LANES), 1)
    lo_half = lane < ATT_HEAD_DIM
    hi_half = jnp.logical_not(lo_half)
    even_lane = (lane & 1) == 0
    lane2 = lax.broadcasted_iota(jnp.int32, (2 * BLK, LANES), 1)
    lo_half2 = lane2 < ATT_HEAD_DIM
    qi = lax.broadcasted_iota(jnp.int32, (BLK, 2 * BLK), 0)
    kj = lax.broadcasted_iota(jnp.int32, (BLK, 2 * BLK), 1)
    diff = qi + BLK - kj
    band = (diff >= 0) & (diff < WINDOW)
    in_cur = kj >= BLK
    ones_lo = jnp.where(lo_half2, 1.0, 0.0).astype(BF16)
    ones_hi = jnp.where(lo_half2, 0.0, 1.0).astype(BF16)
    zeros_bf = jnp.zeros((BLK, LANES), BF16)
    row_lo = lax.broadcasted_iota(jnp.int32, (LANES, LANES), 0) < RET_QK_DIM

    proj_chunks = [dict() for _ in range(N_SUB)]
    out_chunks = [dict() for _ in range(N_SUB)]
    normed = {}
    carry = {}
    att_rhs = {}

    def proj_tile(s, i, col):
        if s == 0:
            return pfirst_ref[i * BLK:(i + 1) * BLK, col:col + LANES]
        chunk, off = divmod(col, PROJ_CHUNK)
        return proj_chunks[s][chunk][i * BLK:(i + 1) * BLK, off:off + LANES]

    def in_proj_chunk(s, c):
        def run():
            if s not in normed:
                xs = x_ref[0, s * SUB_TILE:(s + 1) * SUB_TILE, :]
                ms = jnp.mean(xs * xs, axis=-1, keepdims=True)
                normed[s] = (xs * lax.rsqrt(ms + RMS_EPS) * ng_ref[...]).astype(BF16)
            proj_chunks[s][c] = jnp.dot(normed[s], win_ref[:, c * PROJ_CHUNK:min((c + 1) * PROJ_CHUNK, IN_WIDTH)],
                                        preferred_element_type=F32)
        return run

    def out_proj_chunk(s, c):
        def run():
            out_chunks[s][c] = jnp.dot(mix_refs[s][...], wout_ref[:, c * PROJ_CHUNK:(c + 1) * PROJ_CHUNK],
                                       preferred_element_type=F32)
        return run

    def finish(s):
        def run():
            rows = slice(s * SUB_TILE, (s + 1) * SUB_TILE)
            y = x_ref[0, rows, :] + jnp.concatenate([out_chunks[s][c] for c in range(OUT_CHUNKS)], axis=1)
            ms = jnp.mean(y * y, axis=-1, keepdims=True)
            o_ref[0, rows, :] = y * lax.rsqrt(ms + RMS_EPS) * fg_ref[...]
        return run

    def kv_prep(s, i):
        if s == 0 and i == 0:
            carry["k"] = [kprev_ref[g] for g in range(ATT_KV_HEADS)]
            carry["v"] = [vprev_ref[g] for g in range(ATT_KV_HEADS)]
        k_cur = proj_tile(s, i, OFF_AK)
        v_cur = proj_tile(s, i, OFF_AV)
        v_sw = pltpu.roll(v_cur, ATT_HEAD_DIM, 1)
        k_t = k_cur.T
        kdup_cur = [jnp.concatenate([k_t[g * ATT_HEAD_DIM:(g + 1) * ATT_HEAD_DIM]] * HEADS_PER_TILE,
                                    axis=0).astype(BF16) for g in range(ATT_KV_HEADS)]
        vdup_cur = [jnp.where(lo_half, v_cur, v_sw).astype(BF16),
                    jnp.where(lo_half, v_sw, v_cur).astype(BF16)]
        per_head = []
        for g in range(ATT_KV_HEADS):
            kdup = jnp.concatenate([carry["k"][g], kdup_cur[g]], axis=1)
            vdup = jnp.concatenate([carry["v"][g], vdup_cur[g]], axis=0)
            zero2 = jnp.zeros_like(vdup)
            rhs_v = jnp.concatenate([
                jnp.concatenate([jnp.where(lo_half2, vdup, zero2), ones_lo], axis=1),
                jnp.concatenate([jnp.where(lo_half2, zero2, vdup), ones_hi], axis=1),
            ], axis=0)
            per_head.append((kdup, rhs_v))
        att_rhs[(s, i)] = per_head
        carry["k"], carry["v"] = kdup_cur, vdup_cur

    stash = {}

    def att_scores(s, i):
        def run():
            kv_prep(s, i)
            valid = band & (in_cur | (j > 0)) if (s == 0 and i == 0) else band
            scores = []
            for p in range(ATT_PAIRS):
                kdup, _ = att_rhs[(s, i)][p // (GROUP // HEADS_PER_TILE)]
                q2 = (proj_tile(s, i, OFF_AQ + p * LANES) * (ATT_HEAD_DIM ** -0.5 * LOG2E)).astype(BF16)
                for e in range(HEADS_PER_TILE):
                    qm = jnp.where(lo_half if e == 0 else hi_half, q2, zeros_bf)
                    sc = jnp.dot(qm, kdup, preferred_element_type=F32)
                    scores.append(jnp.where(valid, sc, NEG_INF))
            stash[("sc", s, i)] = scores
        return run

    def att_softmax(s, i):
        def run():
            probs, sink_terms = [], []
            for head, sc in enumerate(stash.pop(("sc", s, i))):
                sink = sinks_ref[head] * LOG2E
                m = jnp.maximum(jnp.max(sc, axis=-1, keepdims=True), sink)
                probs.append(jnp.exp2(sc - m).astype(BF16))
                sink_terms.append(jnp.exp2(sink - m))
            stash[("p", s, i)] = (probs, sink_terms)
        return run

    def att_values(s, i):
        def run():
            probs, sink_terms = stash.pop(("p", s, i))
            for p in range(ATT_PAIRS):
                _, rhs_v = att_rhs[(s, i)][p // (GROUP // HEADS_PER_TILE)]
                h0 = p * HEADS_PER_TILE
                lhs = jnp.concatenate(probs[h0:h0 + HEADS_PER_TILE], axis=1)
                res = jnp.dot(lhs, rhs_v, preferred_element_type=F32)
                den = res[:, LANES:] + jnp.where(lo_half, sink_terms[h0], sink_terms[h0 + 1])
                a2 = res[:, :LANES] / den
                z2 = proj_tile(s, i, OFF_AZ + p * LANES)
                mix_refs[s][i * BLK:(i + 1) * BLK, p * LANES:(p + 1) * LANES] = (a2 * _silu(z2)).astype(BF16)
        return run

    def ret_scores(s, i):
        def run():
            r0 = (s * BLKS_PER_SUB + i) * BLK
            cos = cos_ref[r0:r0 + BLK, :]
            sin = sin_ref[r0:r0 + BLK, :]

            def rotate(t):
                swapped = jnp.where(even_lane, pltpu.roll(t, LANES - 1, 1), pltpu.roll(t, 1, 1))
                return t * cos + swapped * sin

            per_pair = []
            for p in range(RET_PAIRS):
                q2 = rotate(proj_tile(s, i, OFF_RQ + p * LANES))
                k2 = rotate(proj_tile(s, i, OFF_RK + p * LANES)) * (RET_QK_DIM ** -0.5)
                q2b = q2.astype(BF16)
                qd2b = (q2 * qdec_ref[p]).astype(BF16)
                k2t = k2.T
                k2tb = k2t.astype(BF16)
                kd2tb = (k2t * kdec_ref[p]).astype(BF16)
                rhs_qk = jnp.concatenate([jnp.where(row_lo, k2tb, zeros_bf),
                                          jnp.where(row_lo, zeros_bf, k2tb)], axis=1)
                sc2 = jnp.dot(q2b, rhs_qk, preferred_element_type=F32)
                lhs = []
                for e in range(HEADS_PER_TILE):
                    sel = lo_half if e == 0 else hi_half
                    sc = sc2[:, e * BLK:(e + 1) * BLK] * din_ref[p * HEADS_PER_TILE + e]
                    lhs.append(jnp.concatenate([sc.astype(BF16), jnp.where(sel, qd2b, zeros_bf)], axis=1))
                per_pair.append((lhs, kd2tb))
            stash[("ret", s, i)] = per_pair
        return run

    def ret_outputs(s, i):
        def run():
            per_pair = stash.pop(("ret", s, i))
            for p in range(RET_PAIRS):
                lhs, kd2tb = per_pair[p]
                st = state_ref[p * LANES:(p + 1) * LANES, :]
                st_b = st.astype(BF16)
                v_heads = [proj_tile(s, i, OFF_RV + (p * HEADS_PER_TILE + e) * LANES).astype(BF16)
                           for e in range(HEADS_PER_TILE)]
                for e in range(HEADS_PER_TILE):
                    head = p * HEADS_PER_TILE + e
                    rhs = jnp.concatenate([v_heads[e], st_b], axis=0)
                    o = jnp.dot(lhs[e], rhs, preferred_element_type=F32)
                    mu = jnp.mean(o, axis=-1, keepdims=True)
                    oc = o - mu
                    var = jnp.mean(oc * oc, axis=-1, keepdims=True)
                    on = oc * lax.rsqrt(var + GN_EPS) * gng_ref[:, head * LANES:(head + 1) * LANES]
                    rz = proj_tile(s, i, OFF_RZ + head * LANES)
                    mc = ATT_WIDTH + head * LANES
                    mix_refs[s][i * BLK:(i + 1) * BLK, mc:mc + LANES] = (on * _silu(rz)).astype(BF16)
                kv = jnp.concatenate(
                    [jnp.dot(kd2tb[e * RET_QK_DIM:(e + 1) * RET_QK_DIM], v_heads[e], preferred_element_type=F32)
                     for e in range(HEADS_PER_TILE)], axis=0)
                state_ref[p * LANES:(p + 1) * LANES, :] = st * cdec_ref[p * LANES:(p + 1) * LANES, :] + kv
        return run

    def mixer_units(s):
        units = []
        for stage in (att_scores, ret_scores, att_softmax, ret_outputs, att_values):
            units += [stage(s, i) for i in range(BLKS_PER_SUB)]
        return units

    for s in range(N_SUB):
        fillers = []
        if s >= 1:
            fillers += [out_proj_chunk(s - 1, c) for c in range(OUT_CHUNKS)] + [finish(s - 1)]
        if s + 1 < N_SUB:
            fillers += [in_proj_chunk(s + 1, c) for c in range(IN_CHUNKS)]
        else:
            fillers += project_first_subtile(xnext_ref)
        _interleave(mixer_units(s), fillers)
    for c in range(OUT_CHUNKS):
        out_proj_chunk(N_SUB - 1, c)()
    finish(N_SUB - 1)()

    for g in range(ATT_KV_HEADS):
        kprev_ref[g] = carry["k"][g]
        vprev_ref[g] = carry["v"][g]


def _retention_tables(seq):
    pos = np.arange(seq, dtype=np.float64)
    theta = 1.0 / (ROT_BASE ** np.linspace(0.0, 1.0, RET_QK_DIM // 2))
    ang = pos[:, None] * theta[None, :]
    cos, sin = np.cos(ang), np.sin(ang)
    cos_l = np.tile(np.repeat(cos, 2, axis=1), (1, HEADS_PER_TILE))
    sign = np.tile(np.array([-1.0, 1.0]), RET_QK_DIM // 2)
    sin_l = np.tile(np.repeat(sin, 2, axis=1) * sign[None, :], (1, HEADS_PER_TILE))

    log_gamma = np.log(1.0 - 2.0 ** (-5.0 - np.arange(RET_HEADS, dtype=np.float64)))
    idx = np.arange(BLK, dtype=np.float64)
    rel = idx[:, None] - idx[None, :]
    decay_in = np.where(rel >= 0, np.exp(log_gamma[:, None, None] * np.maximum(rel, 0.0)), 0.0)
    k_dec = np.exp(log_gamma[:, None] * (BLK - 1 - idx)[None, :])
    q_dec = np.exp(log_gamma[:, None] * (idx + 1)[None, :])
    chunk_decay = np.exp(log_gamma * BLK)

    def per_pair(dec):
        t = np.repeat(dec.T[:, :, None], RET_QK_DIM, axis=2)
        t = t.reshape(BLK, RET_PAIRS, LANES)
        return np.transpose(t, (1, 0, 2))

    cdec = np.broadcast_to(np.repeat(chunk_decay, RET_QK_DIM)[:, None],
                           (RET_QK_WIDTH, RET_V_DIM))
    k_dec_t = np.transpose(per_pair(k_dec), (0, 2, 1))
    tables = (cos_l, sin_l, decay_in, per_pair(q_dec), k_dec_t, cdec)
    return tuple(jnp.asarray(np.ascontiguousarray(a, dtype=np.float32)) for a in tables)


def _full(shape):
    return pl.BlockSpec(shape, lambda step: (0,) * len(shape))


@jax.jit
def kernel(x, norm_g, w_in, att_sinks, ret_gn_g, w_out, final_g):
    batch, seq, d = x.shape
    depth = w_in.shape[0]
    assert depth == 1 and d == D_MODEL and seq % SEQ_TILE == 0
    cos_l, sin_l, decay_in, qdec, kdec, cdec = _retention_tables(seq)
    tile = SEQ_TILE
    n_seq = seq // tile
    n_tiles = batch * n_seq

    def tile_of(step):
        return jnp.maximum(step - CAST_STEPS, 0)

    def this_tile(step):
        t = tile_of(step)
        return (t // n_seq, t % n_seq, 0)

    def next_first_subtile(step):
        u = jnp.minimum(tile_of(step) + 1, n_tiles - 1)
        return (u // n_seq, (u % n_seq) * N_SUB, 0)

    def weight_slab(step):
        return (jnp.minimum(step, CAST_STEPS - 1), 0)

    call = pl.pallas_call(
        functools.partial(_layer_kernel, n_seq),
        grid=(CAST_STEPS + n_tiles,),
        in_specs=[
            pl.BlockSpec(memory_space=pltpu.SMEM),
            pl.BlockSpec((1, tile, D_MODEL), this_tile),
            pl.BlockSpec((1, SUB_TILE, D_MODEL), next_first_subtile),
            _full((1, D_MODEL)),
            pl.BlockSpec((WEIGHT_SLAB, IN_WIDTH), weight_slab),
            _full((1, RET_WIDTH)),
            pl.BlockSpec((WEIGHT_SLAB, D_MODEL), weight_slab),
            _full((1, D_MODEL)),
            pl.BlockSpec((tile, LANES), lambda step: (tile_of(step) % n_seq, 0)),
            pl.BlockSpec((tile, LANES), lambda step: (tile_of(step) % n_seq, 0)),
            _full((RET_HEADS, BLK, BLK)),
            _full((RET_PAIRS, BLK, LANES)),
            _full((RET_PAIRS, BLK, LANES)),
            _full((RET_QK_WIDTH, RET_V_DIM)),
        ],
        out_specs=pl.BlockSpec((1, tile, D_MODEL), this_tile),
        out_shape=jax.ShapeDtypeStruct(x.shape, x.dtype),
        scratch_shapes=[
            pltpu.VMEM((RET_QK_WIDTH, RET_V_DIM), F32),
            pltpu.VMEM((ATT_KV_HEADS, BLK, LANES), BF16),
            pltpu.VMEM((ATT_KV_HEADS, BLK, LANES), BF16),
            pltpu.VMEM((SUB_TILE, IN_WIDTH), F32),
            pltpu.VMEM((D_MODEL, IN_WIDTH), BF16),
            pltpu.VMEM((MIX_WIDTH, D_MODEL), BF16),
        ] + [pltpu.VMEM((SUB_TILE, MIX_WIDTH), BF16)] * N_SUB,
        compiler_params=pltpu.CompilerParams(
            dimension_semantics=("arbitrary",),
            vmem_limit_bytes=VMEM_LIMIT_BYTES),
        name="hymba_layer",
    )
    return call(att_sinks[0], x, x, norm_g[0][None, :], w_in[0],
                ret_gn_g[0][None, :], w_out[0], final_g[None, :],
                cos_l, sin_l, decay_in, qdec, kdec, cdec)
```

```python
import functools

import jax
import jax.numpy as jnp
import numpy as np
from jax import lax
from jax.experimental import pallas as pl
from jax.experimental.pallas import tpu as pltpu

D_MODEL = 1024
ATT_HEADS = 8
ATT_KV_HEADS = 2
ATT_HEAD_DIM = 64
WINDOW = 128
BLK = 128
RET_HEADS = 4
RET_QK_DIM = 64
RET_V_DIM = 128
ROT_BASE = 10000.0
RMS_EPS = 1e-6
GN_EPS = 1e-6
NEG_INF = -1e30
LOG2E = 1.4426950408889634

ATT_WIDTH = ATT_HEADS * ATT_HEAD_DIM
ATT_KV_WIDTH = ATT_KV_HEADS * ATT_HEAD_DIM
RET_QK_WIDTH = RET_HEADS * RET_QK_DIM
RET_WIDTH = RET_HEADS * RET_V_DIM
MIX_WIDTH = ATT_WIDTH + RET_WIDTH
IN_WIDTH = 2 * ATT_WIDTH + 2 * ATT_KV_WIDTH + 2 * RET_QK_WIDTH + 2 * RET_WIDTH

OFF_AQ = 0
OFF_AK = OFF_AQ + ATT_WIDTH
OFF_AV = OFF_AK + ATT_KV_WIDTH
OFF_AZ = OFF_AV + ATT_KV_WIDTH
OFF_RQ = OFF_AZ + ATT_WIDTH
OFF_RK = OFF_RQ + RET_QK_WIDTH
OFF_RV = OFF_RK + RET_QK_WIDTH
OFF_RZ = OFF_RV + RET_WIDTH

LANES = 128
MXU_COLS = 256
N_MXU = 2
PROJ_CHUNK = N_MXU * MXU_COLS
HEADS_PER_TILE = LANES // ATT_HEAD_DIM
ATT_PAIRS = ATT_HEADS // HEADS_PER_TILE
RET_PAIRS = RET_HEADS // HEADS_PER_TILE
GROUP = ATT_HEADS // ATT_KV_HEADS

SEQ_TILE = 1024
SUB_TILE = 256
N_SUB = SEQ_TILE // SUB_TILE
BLKS_PER_SUB = SUB_TILE // BLK
IN_CHUNKS = -(-IN_WIDTH // PROJ_CHUNK)
OUT_CHUNKS = D_MODEL // PROJ_CHUNK
WEIGHT_SLAB = 256
CAST_STEPS = D_MODEL // WEIGHT_SLAB
VMEM_LIMIT_BYTES = 56 * 1024 * 1024

F32 = jnp.float32
BF16 = jnp.bfloat16


def _silu(z):
    hz = 0.5 * z
    return hz + hz * jnp.tanh(hz)


def _interleave(units, fillers):
    done = 0
    for k, unit in enumerate(units):
        unit()
        due = -(-(k + 1) * len(fillers) // len(units))
        while done < due:
            fillers[done]()
            done += 1
    for f in fillers[done:]:
        f()


def _layer_kernel(tiles_per_seq, sinks_ref, x_ref, xnext_ref, ng_ref, win_slab_ref, gng_ref, wout_slab_ref, fg_ref,
                  cos_ref, sin_ref, din_ref, qdec_ref, kdec_ref, cdec_ref,
                  o_ref, state_ref, kprev_ref, vprev_ref, pfirst_ref, win_ref, wout_ref, *mix_refs):
    step = pl.program_id(0)

    @pl.when(step < CAST_STEPS)
    def _():
        rows = pl.ds(pl.multiple_of(step * WEIGHT_SLAB, WEIGHT_SLAB), WEIGHT_SLAB)
        win_ref[rows, :] = win_slab_ref[...].astype(BF16)
        wout_ref[rows, :] = wout_slab_ref[...].astype(BF16)

    @pl.when(step >= CAST_STEPS)
    def _():
        _tile_step(step - CAST_STEPS, tiles_per_seq, sinks_ref, x_ref, xnext_ref, ng_ref, win_ref, gng_ref, wout_ref,
                   fg_ref, cos_ref, sin_ref, din_ref, qdec_ref, kdec_ref, cdec_ref,
                   o_ref, state_ref, kprev_ref, vprev_ref, pfirst_ref, mix_refs)


def _tile_step(t, tiles_per_seq, sinks_ref, x_ref, xnext_ref, ng_ref, win_ref, gng_ref, wout_ref, fg_ref,
               cos_ref, sin_ref, din_ref, qdec_ref, kdec_ref, cdec_ref,
               o_ref, state_ref, kprev_ref, vprev_ref, pfirst_ref, mix_refs):
    j = lax.rem(t, tiles_per_seq)

    def project_first_subtile(src_ref):
        held = {}

        def chunk(c):
            def run():
                if not held:
                    xs = src_ref[0]
                    ms = jnp.mean(xs * xs, axis=-1, keepdims=True)
                    held["h"] = (xs * lax.rsqrt(ms + RMS_EPS) * ng_ref[...]).astype(BF16)
                c0, c1 = c * PROJ_CHUNK, min((c + 1) * PROJ_CHUNK, IN_WIDTH)
                pfirst_ref[:, c0:c1] = jnp.dot(held["h"], win_ref[:, c0:c1], preferred_element_type=F32)
            return run
        return [chunk(c) for c in range(IN_CHUNKS)]

    @pl.when(t == 0)
    def _():
        for f in project_first_subtile(x_ref.at[:, 0:SUB_TILE, :]):
            f()

    @pl.when(j == 0)
    def _():
        state_ref[...] = jnp.zeros_like(state_ref)
        kprev_ref[...] = jnp.zeros_like(kprev_ref)
        vprev_ref[...] = jnp.zeros_like(vprev_ref)

    lane = lax.broadcasted_iota(jnp.int32, (BLK, LANES), 1)
    lo_half = lane < ATT_HEAD_DIM
    hi_half = jnp.logical_not(lo_half)
    even_lane = (lane & 1) == 0
    lane2 = lax.broadcasted_iota(jnp.int32, (2 * BLK, LANES), 1)
    lo_half2 = lane2 < ATT_HEAD_DIM
    qi = lax.broadcasted_iota(jnp.int32, (BLK, 2 * BLK), 0)
    kj = lax.broadcasted_iota(jnp.int32, (BLK, 2 * BLK), 1)
    diff = qi + BLK - kj
    band = (diff >= 0) & (diff < WINDOW)
    in_cur = kj >= BLK
    ones_lo = jnp.where(lo_half2, 1.0, 0.0).astype(BF16)
    ones_hi = jnp.where(lo_half2, 0.0, 1.0).astype(BF16)
    zeros_bf = jnp.zeros((BLK, LANES), BF16)
    row_lo = lax.broadcasted_iota(jnp.int32, (LANES, LANES), 0) < RET_QK_DIM

    proj_chunks = [dict() for _ in range(N_SUB)]
    out_chunks = [dict() for _ in range(N_SUB)]
    normed = {}
    carry = {}
    att_rhs = {}

    def proj_tile(s, i, col):
        if s == 0:
            return pfirst_ref[i * BLK:(i + 1) * BLK, col:col + LANES]
        chunk, off = divmod(col, PROJ_CHUNK)
        return proj_chunks[s][chunk][i * BLK:(i + 1) * BLK, off:off + LANES]

    def in_proj_chunk(s, c):
        def run():
            if s not in normed:
                xs = x_ref[0, s * SUB_TILE:(s + 1) * SUB_TILE, :]
                ms = jnp.mean(xs * xs, axis=-1, keepdims=True)
                normed[s] = (xs * lax.rsqrt(ms + RMS_EPS) * ng_ref[...]).astype(BF16)
            proj_chunks[s][c] = jnp.dot(normed[s], win_ref[:, c * PROJ_CHUNK:min((c + 1) * PROJ_CHUNK, IN_WIDTH)],
                                        preferred_element_type=F32)
        return run

    def out_proj_chunk(s, c):
        def run():
            out_chunks[s][c] = jnp.dot(mix_refs[s][...], wout_ref[:, c * PROJ_CHUNK:(c + 1) * PROJ_CHUNK],
                                       preferred_element_type=F32)
        return run

    def finish(s):
        def run():
            rows = slice(s * SUB_TILE, (s + 1) * SUB_TILE)
            y = x_ref[0, rows, :] + jnp.concatenate([out_chunks[s][c] for c in range(OUT_CHUNKS)], axis=1)
            ms = jnp.mean(y * y, axis=-1, keepdims=True)
            o_ref[0, rows, :] = y * lax.rsqrt(ms + RMS_EPS) * fg_ref[...]
        return run

    def kv_prep(s, i):
        if s == 0 and i == 0:
            carry["k"] = [kprev_ref[g] for g in range(ATT_KV_HEADS)]
            carry["v"] = [vprev_ref[g] for g in range(ATT_KV_HEADS)]
        k_cur = proj_tile(s, i, OFF_AK)
        v_cur = proj_tile(s, i, OFF_AV)
        v_sw = pltpu.roll(v_cur, ATT_HEAD_DIM, 1)
        k_t = k_cur.T
        kdup_cur = [jnp.concatenate([k_t[g * ATT_HEAD_DIM:(g + 1) * ATT_HEAD_DIM]] * HEADS_PER_TILE,
                                    axis=0).astype(BF16) for g in range(ATT_KV_HEADS)]
        vdup_cur = [jnp.where(lo_half, v_cur, v_sw).astype(BF16),
                    jnp.where(lo_half, v_sw, v_cur).astype(BF16)]
        per_head = []
        for g in range(ATT_KV_HEADS):
            kdup = jnp.concatenate([carry["k"][g], kdup_cur[g]], axis=1)
            vdup = jnp.concatenate([carry["v"][g], vdup_cur[g]], axis=0)
            zero2 = jnp.zeros_like(vdup)
            rhs_v = jnp.concatenate([
                jnp.concatenate([jnp.where(lo_half2, vdup, zero2), ones_lo], axis=1),
                jnp.concatenate([jnp.where(lo_half2, zero2, vdup), ones_hi], axis=1),
            ], axis=0)
            per_head.append((kdup, rhs_v))
        att_rhs[(s, i)] = per_head
        carry["k"], carry["v"] = kdup_cur, vdup_cur

    stash = {}

    def att_scores(s, i):
        def run():
            kv_prep(s, i)
            valid = band & (in_cur | (j > 0)) if (s == 0 and i == 0) else band
            scores = []
            for p in range(ATT_PAIRS):
                kdup, _ = att_rhs[(s, i)][p // (GROUP // HEADS_PER_TILE)]
                q2 = (proj_tile(s, i, OFF_AQ + p * LANES) * (ATT_HEAD_DIM ** -0.5 * LOG2E)).astype(BF16)
                for e in range(HEADS_PER_TILE):
                    qm = jnp.where(lo_half if e == 0 else hi_half, q2, zeros_bf)
                    sc = jnp.dot(qm, kdup, preferred_element_type=F32)
                    scores.append(jnp.where(valid, sc, NEG_INF))
            stash[("sc", s, i)] = scores
        return run

    def att_softmax(s, i):
        def run():
            probs, sink_terms = [], []
            for head, sc in enumerate(stash.pop(("sc", s, i))):
                sink = sinks_ref[head] * LOG2E
                m = jnp.maximum(jnp.max(sc, axis=-1, keepdims=True), sink)
                probs.append(jnp.exp2(sc - m).astype(BF16))
                sink_terms.append(jnp.exp2(sink - m))
            stash[("p", s, i)] = (probs, sink_terms)
        return run

    def att_values(s, i):
        def run():
            probs, sink_terms = stash.pop(("p", s, i))
            for p in range(ATT_PAIRS):
                _, rhs_v = att_rhs[(s, i)][p // (GROUP // HEADS_PER_TILE)]
                h0 = p * HEADS_PER_TILE
                lhs = jnp.concatenate(probs[h0:h0 + HEADS_PER_TILE], axis=1)
                res = jnp.dot(lhs, rhs_v, preferred_element_type=F32)
                den = res[:, LANES:] + jnp.where(lo_half, sink_terms[h0], sink_terms[h0 + 1])
                a2 = res[:, :LANES] / den
                z2 = proj_tile(s, i, OFF_AZ + p * LANES)
                mix_refs[s][i * BLK:(i + 1) * BLK, p * LANES:(p + 1) * LANES] = (a2 * _silu(z2)).astype(BF16)
        return run

    def ret_scores(s, i):
        def run():
            r0 = (s * BLKS_PER_SUB + i) * BLK
            cos = cos_ref[r0:r0 + BLK, :]
            sin = sin_ref[r0:r0 + BLK, :]

            def rotate(t):
                swapped = jnp.where(even_lane, pltpu.roll(t, LANES - 1, 1), pltpu.roll(t, 1, 1))
                return t * cos + swapped * sin

            per_pair = []
            for p in range(RET_PAIRS):
                q2 = rotate(proj_tile(s, i, OFF_RQ + p * LANES))
                k2 = rotate(proj_tile(s, i, OFF_RK + p * LANES)) * (RET_QK_DIM ** -0.5)
                q2b = q2.astype(BF16)
                qd2b = (q2 * qdec_ref[p]).astype(BF16)
                k2t = k2.T
                k2tb = k2t.astype(BF16)
                kd2tb = (k2t * kdec_ref[p]).astype(BF16)
                rhs_qk = jnp.concatenate([jnp.where(row_lo, k2tb, zeros_bf),
                                          jnp.where(row_lo, zeros_bf, k2tb)], axis=1)
                sc2 = jnp.dot(q2b, rhs_qk, preferred_element_type=F32)
                lhs = []
                for e in range(HEADS_PER_TILE):
                    sel = lo_half if e == 0 else hi_half
                    sc = sc2[:, e * BLK:(e + 1) * BLK] * din_ref[p * HEADS_PER_TILE + e]
                    lhs.append(jnp.concatenate([sc.astype(BF16), jnp.where(sel, qd2b, zeros_bf)], axis=1))
                per_pair.append((lhs, kd2tb))
            stash[("ret", s, i)] = per_pair
        return run

    def ret_outputs(s, i):
        def run():
            per_pair = stash.pop(("ret", s, i))
            for p in range(RET_PAIRS):
                lhs, kd2tb = per_pair[p]
                st = state_ref[p * LANES:(p + 1) * LANES, :]
                st_b = st.astype(BF16)
                v_heads = [proj_tile(s, i, OFF_RV + (p * HEADS_PER_TILE + e) * LANES).astype(BF16)
                           for e in range(HEADS_PER_TILE)]
                for e in range(HEADS_PER_TILE):
                    head = p * HEADS_PER_TILE + e
                    rhs = jnp.concatenate([v_heads[e], st_b], axis=0)
                    o = jnp.dot(lhs[e], rhs, preferred_element_type=F32)
                    mu = jnp.mean(o, axis=-1, keepdims=True)
                    oc = o - mu
                    var = jnp.mean(oc * oc, axis=-1, keepdims=True)
                    on = oc * lax.rsqrt(var + GN_EPS) * gng_ref[:, head * LANES:(head + 1) * LANES]
                    rz = proj_tile(s, i, OFF_RZ + head * LANES)
                    mc = ATT_WIDTH + head * LANES
                    mix_refs[s][i * BLK:(i + 1) * BLK, mc:mc + LANES] = (on * _silu(rz)).astype(BF16)
                kv = jnp.dot(kd2tb, jnp.concatenate(v_heads, axis=1),
                             preferred_element_type=F32)
                kv_sel = jnp.where(row_lo, kv[:, :LANES], kv[:, LANES:])
                state_ref[p * LANES:(p + 1) * LANES, :] = st * cdec_ref[p * LANES:(p + 1) * LANES, :] + kv_sel
        return run

    def mixer_units(s):
        units = []
        for stage in (att_scores, ret_scores, att_softmax, ret_outputs, att_values):
            units += [stage(s, i) for i in range(BLKS_PER_SUB)]
        return units

    for s in range(N_SUB):
        fillers = []
        if s >= 1:
            fillers += [out_proj_chunk(s - 1, c) for c in range(OUT_CHUNKS)] + [finish(s - 1)]
        if s + 1 < N_SUB:
            fillers += [in_proj_chunk(s + 1, c) for c in range(IN_CHUNKS)]
        else:
            fillers += project_first_subtile(xnext_ref)
        _interleave(mixer_units(s), fillers)
    for c in range(OUT_CHUNKS):
        out_proj_chunk(N_SUB - 1, c)()
    finish(N_SUB - 1)()

    for g in range(ATT_KV_HEADS):
        kprev_ref[g] = carry["k"][g]
        vprev_ref[g] = carry["v"][g]


def _retention_tables(seq):
    pos = np.arange(seq, dtype=np.float64)
    theta = 1.0 / (ROT_BASE ** np.linspace(0.0, 1.0, RET_QK_DIM // 2))
    ang = pos[:, None] * theta[None, :]
    cos, sin = np.cos(ang), np.sin(ang)
    cos_l = np.tile(np.repeat(cos, 2, axis=1), (1, HEADS_PER_TILE))
    sign = np.tile(np.array([-1.0, 1.0]), RET_QK_DIM // 2)
    sin_l = np.tile(np.repeat(sin, 2, axis=1) * sign[None, :], (1, HEADS_PER_TILE))

    log_gamma = np.log(1.0 - 2.0 ** (-5.0 - np.arange(RET_HEADS, dtype=np.float64)))
    idx = np.arange(BLK, dtype=np.float64)
    rel = idx[:, None] - idx[None, :]
    decay_in = np.where(rel >= 0, np.exp(log_gamma[:, None, None] * np.maximum(rel, 0.0)), 0.0)
    k_dec = np.exp(log_gamma[:, None] * (BLK - 1 - idx)[None, :])
    q_dec = np.exp(log_gamma[:, None] * (idx + 1)[None, :])
    chunk_decay = np.exp(log_gamma * BLK)

    def per_pair(dec):
        t = np.repeat(dec.T[:, :, None], RET_QK_DIM, axis=2)
        t = t.reshape(BLK, RET_PAIRS, LANES)
        return np.transpose(t, (1, 0, 2))

    cdec = np.broadcast_to(np.repeat(chunk_decay, RET_QK_DIM)[:, None],
                           (RET_QK_WIDTH, RET_V_DIM))
    k_dec_t = np.transpose(per_pair(k_dec), (0, 2, 1))
    tables = (cos_l, sin_l, decay_in, per_pair(q_dec), k_dec_t, cdec)
    return tuple(jnp.asarray(np.ascontiguousarray(a, dtype=np.float32)) for a in tables)


def _full(shape):
    return pl.BlockSpec(shape, lambda step: (0,) * len(shape))


@jax.jit
def kernel(x, norm_g, w_in, att_sinks, ret_gn_g, w_out, final_g):
    batch, seq, d = x.shape
    depth = w_in.shape[0]
    assert depth == 1 and d == D_MODEL and seq % SEQ_TILE == 0
    cos_l, sin_l, decay_in, qdec, kdec, cdec = _retention_tables(seq)
    tile = SEQ_TILE
    n_seq = seq // tile
    n_tiles = batch * n_seq

    def tile_of(step):
        return jnp.maximum(step - CAST_STEPS, 0)

    def this_tile(step):
        t = tile_of(step)
        return (t // n_seq, t % n_seq, 0)

    def next_first_subtile(step):
        u = jnp.minimum(tile_of(step) + 1, n_tiles - 1)
        return (u // n_seq, (u % n_seq) * N_SUB, 0)

    def weight_slab(step):
        return (jnp.minimum(step, CAST_STEPS - 1), 0)

    call = pl.pallas_call(
        functools.partial(_layer_kernel, n_seq),
        grid=(CAST_STEPS + n_tiles,),
        in_specs=[
            pl.BlockSpec(memory_space=pltpu.SMEM),
            pl.BlockSpec((1, tile, D_MODEL), this_tile),
            pl.BlockSpec((1, SUB_TILE, D_MODEL), next_first_subtile),
            _full((1, D_MODEL)),
            pl.BlockSpec((WEIGHT_SLAB, IN_WIDTH), weight_slab),
            _full((1, RET_WIDTH)),
            pl.BlockSpec((WEIGHT_SLAB, D_MODEL), weight_slab),
            _full((1, D_MODEL)),
            pl.BlockSpec((tile, LANES), lambda step: (tile_of(step) % n_seq, 0)),
            pl.BlockSpec((tile, LANES), lambda step: (tile_of(step) % n_seq, 0)),
            _full((RET_HEADS, BLK, BLK)),
            _full((RET_PAIRS, BLK, LANES)),
            _full((RET_PAIRS, BLK, LANES)),
            _full((RET_QK_WIDTH, RET_V_DIM)),
        ],
        out_specs=pl.BlockSpec((1, tile, D_MODEL), this_tile),
        out_shape=jax.ShapeDtypeStruct(x.shape, x.dtype),
        scratch_shapes=[
            pltpu.VMEM((RET_QK_WIDTH, RET_V_DIM), F32),
            pltpu.VMEM((ATT_KV_HEADS, BLK, LANES), BF16),
            pltpu.VMEM((ATT_KV_HEADS, BLK, LANES), BF16),
            pltpu.VMEM((SUB_TILE, IN_WIDTH), F32),
            pltpu.VMEM((D_MODEL, IN_WIDTH), BF16),
            pltpu.VMEM((MIX_WIDTH, D_MODEL), BF16),
        ] + [pltpu.VMEM((SUB_TILE, MIX_WIDTH), BF16)] * N_SUB,
        compiler_params=pltpu.CompilerParams(
            dimension_semantics=("arbitrary",),
            vmem_limit_bytes=VMEM_LIMIT_BYTES),
        name="hymba_layer",
    )
    return call(att_sinks[0], x, x, norm_g[0][None, :], w_in[0],
                ret_gn_g[0][None, :], w_out[0], final_g[None, :],
                cos_l, sin_l, decay_in, qdec, kdec, cdec)
```

```python
import functools

import jax
import jax.numpy as jnp
import numpy as np
from jax import lax
from jax.experimental import pallas as pl
from jax.experimental.pallas import tpu as pltpu

D_MODEL = 1024
ATT_HEADS = 8
ATT_KV_HEADS = 2
ATT_HEAD_DIM = 64
WINDOW = 128
BLK = 128
RET_HEADS = 4
RET_QK_DIM = 64
RET_V_DIM = 128
ROT_BASE = 10000.0
RMS_EPS = 1e-6
GN_EPS = 1e-6
NEG_INF = -1e30
LOG2E = 1.4426950408889634

ATT_WIDTH = ATT_HEADS * ATT_HEAD_DIM
ATT_KV_WIDTH = ATT_KV_HEADS * ATT_HEAD_DIM
RET_QK_WIDTH = RET_HEADS * RET_QK_DIM
RET_WIDTH = RET_HEADS * RET_V_DIM
MIX_WIDTH = ATT_WIDTH + RET_WIDTH
IN_WIDTH = 2 * ATT_WIDTH + 2 * ATT_KV_WIDTH + 2 * RET_QK_WIDTH + 2 * RET_WIDTH

OFF_AQ = 0
OFF_AK = OFF_AQ + ATT_WIDTH
OFF_AV = OFF_AK + ATT_KV_WIDTH
OFF_AZ = OFF_AV + ATT_KV_WIDTH
OFF_RQ = OFF_AZ + ATT_WIDTH
OFF_RK = OFF_RQ + RET_QK_WIDTH
OFF_RV = OFF_RK + RET_QK_WIDTH
OFF_RZ = OFF_RV + RET_WIDTH

LANES = 128
MXU_COLS = 256
N_MXU = 2
PROJ_CHUNK = N_MXU * MXU_COLS
HEADS_PER_TILE = LANES // ATT_HEAD_DIM
ATT_PAIRS = ATT_HEADS // HEADS_PER_TILE
RET_PAIRS = RET_HEADS // HEADS_PER_TILE
GROUP = ATT_HEADS // ATT_KV_HEADS

SEQ_TILE = 512
SUB_TILE = 256
N_SUB = SEQ_TILE // SUB_TILE
BLKS_PER_SUB = SUB_TILE // BLK
IN_CHUNKS = -(-IN_WIDTH // PROJ_CHUNK)
OUT_CHUNKS = D_MODEL // PROJ_CHUNK
WEIGHT_SLAB = 256
CAST_STEPS = D_MODEL // WEIGHT_SLAB
VMEM_LIMIT_BYTES = 56 * 1024 * 1024

F32 = jnp.float32
BF16 = jnp.bfloat16


def _silu(z):
    hz = 0.5 * z
    return hz + hz * jnp.tanh(hz)


def _interleave(units, fillers):
    done = 0
    for k, unit in enumerate(units):
        unit()
        due = -(-(k + 1) * len(fillers) // len(units))
        while done < due:
            fillers[done]()
            done += 1
    for f in fillers[done:]:
        f()


def _layer_kernel(tiles_per_seq, sinks_ref, x_ref, xnext_ref, ng_ref, win_slab_ref, gng_ref, wout_slab_ref, fg_ref,
                  cos_ref, sin_ref, din_ref, qdec_ref, kdec_ref, cdec_ref,
                  o_ref, state_ref, kprev_ref, vprev_ref, pfirst_ref, win_ref, wout_ref, *mix_refs):
    step = pl.program_id(0)

    @pl.when(step < CAST_STEPS)
    def _():
        rows = pl.ds(pl.multiple_of(step * WEIGHT_SLAB, WEIGHT_SLAB), WEIGHT_SLAB)
        win_ref[rows, :] = win_slab_ref[...].astype(BF16)
        wout_ref[rows, :] = wout_slab_ref[...].astype(BF16)

    @pl.when(step >= CAST_STEPS)
    def _():
        _tile_step(step - CAST_STEPS, tiles_per_seq, sinks_ref, x_ref, xnext_ref, ng_ref, win_ref, gng_ref, wout_ref,
                   fg_ref, cos_ref, sin_ref, din_ref, qdec_ref, kdec_ref, cdec_ref,
                   o_ref, state_ref, kprev_ref, vprev_ref, pfirst_ref, mix_refs)


def _tile_step(t, tiles_per_seq, sinks_ref, x_ref, xnext_ref, ng_ref, win_ref, gng_ref, wout_ref, fg_ref,
               cos_ref, sin_ref, din_ref, qdec_ref, kdec_ref, cdec_ref,
               o_ref, state_ref, kprev_ref, vprev_ref, pfirst_ref, mix_refs):
    j = lax.rem(t, tiles_per_seq)

    def project_first_subtile(src_ref):
        held = {}

        def chunk(c):
            def run():
                if not held:
                    xs = src_ref[0]
                    ms = jnp.mean(xs * xs, axis=-1, keepdims=True)
                    held["h"] = (xs * lax.rsqrt(ms + RMS_EPS) * ng_ref[...]).astype(BF16)
                c0, c1 = c * PROJ_CHUNK, min((c + 1) * PROJ_CHUNK, IN_WIDTH)
                pfirst_ref[:, c0:c1] = jnp.dot(held["h"], win_ref[:, c0:c1], preferred_element_type=F32)
            return run
        return [chunk(c) for c in range(IN_CHUNKS)]

    @pl.when(t == 0)
    def _():
        for f in project_first_subtile(x_ref.at[:, 0:SUB_TILE, :]):
            f()

    @pl.when(j == 0)
    def _():
        state_ref[...] = jnp.zeros_like(state_ref)
        kprev_ref[...] = jnp.zeros_like(kprev_ref)
        vprev_ref[...] = jnp.zeros_like(vprev_ref)

    lane = lax.broadcasted_iota(jnp.int32, (BLK, LANES), 1)
    lo_half = lane < ATT_HEAD_DIM
    hi_half = jnp.logical_not(lo_half)
    even_lane = (lane & 1) == 0
    lane2 = lax.broadcasted_iota(jnp.int32, (2 * BLK, LANES), 1)
    lo_half2 = lane2 < ATT_HEAD_DIM
    qi = lax.broadcasted_iota(jnp.int32, (BLK, 2 * BLK), 0)
    kj = lax.broadcasted_iota(jnp.int32, (BLK, 2 * BLK), 1)
    diff = qi + BLK - kj
    band = (diff >= 0) & (diff < WINDOW)
    in_cur = kj >= BLK
    ones_lo = jnp.where(lo_half2, 1.0, 0.0).astype(BF16)
    ones_hi = jnp.where(lo_half2, 0.0, 1.0).astype(BF16)
    zeros_bf = jnp.zeros((BLK, LANES), BF16)
    row_lo = lax.broadcasted_iota(jnp.int32, (LANES, LANES), 0) < RET_QK_DIM

    proj_chunks = [dict() for _ in range(N_SUB)]
    out_chunks = [dict() for _ in range(N_SUB)]
    normed = {}
    carry = {}
    att_rhs = {}

    def proj_tile(s, i, col):
        if s == 0:
            return pfirst_ref[i * BLK:(i + 1) * BLK, col:col + LANES]
        chunk, off = divmod(col, PROJ_CHUNK)
        return proj_chunks[s][chunk][i * BLK:(i + 1) * BLK, off:off + LANES]

    def in_proj_chunk(s, c):
        def run():
            if s not in normed:
                xs = x_ref[0, s * SUB_TILE:(s + 1) * SUB_TILE, :]
                ms = jnp.mean(xs * xs, axis=-1, keepdims=True)
                normed[s] = (xs * lax.rsqrt(ms + RMS_EPS) * ng_ref[...]).astype(BF16)
            proj_chunks[s][c] = jnp.dot(normed[s], win_ref[:, c * PROJ_CHUNK:min((c + 1) * PROJ_CHUNK, IN_WIDTH)],
                                        preferred_element_type=F32)
        return run

    def out_proj_chunk(s, c):
        def run():
            out_chunks[s][c] = jnp.dot(mix_refs[s][...], wout_ref[:, c * PROJ_CHUNK:(c + 1) * PROJ_CHUNK],
                                       preferred_element_type=F32)
        return run

    def finish(s):
        def run():
            rows = slice(s * SUB_TILE, (s + 1) * SUB_TILE)
            y = x_ref[0, rows, :] + jnp.concatenate([out_chunks[s][c] for c in range(OUT_CHUNKS)], axis=1)
            ms = jnp.mean(y * y, axis=-1, keepdims=True)
            o_ref[0, rows, :] = y * lax.rsqrt(ms + RMS_EPS) * fg_ref[...]
        return run

    def kv_prep(s, i):
        if s == 0 and i == 0:
            carry["k"] = [kprev_ref[g] for g in range(ATT_KV_HEADS)]
            carry["v"] = [vprev_ref[g] for g in range(ATT_KV_HEADS)]
        k_cur = proj_tile(s, i, OFF_AK)
        v_cur = proj_tile(s, i, OFF_AV)
        v_sw = pltpu.roll(v_cur, ATT_HEAD_DIM, 1)
        k_t = k_cur.T
        kdup_cur = [jnp.concatenate([k_t[g * ATT_HEAD_DIM:(g + 1) * ATT_HEAD_DIM]] * HEADS_PER_TILE,
                                    axis=0).astype(BF16) for g in range(ATT_KV_HEADS)]
        vdup_cur = [jnp.where(lo_half, v_cur, v_sw).astype(BF16),
                    jnp.where(lo_half, v_sw, v_cur).astype(BF16)]
        per_head = []
        for g in range(ATT_KV_HEADS):
            kdup = jnp.concatenate([carry["k"][g], kdup_cur[g]], axis=1)
            vdup = jnp.concatenate([carry["v"][g], vdup_cur[g]], axis=0)
            zero2 = jnp.zeros_like(vdup)
            rhs_v = jnp.concatenate([
                jnp.concatenate([jnp.where(lo_half2, vdup, zero2), ones_lo], axis=1),
                jnp.concatenate([jnp.where(lo_half2, zero2, vdup), ones_hi], axis=1),
            ], axis=0)
            per_head.append((kdup, rhs_v))
        att_rhs[(s, i)] = per_head
        carry["k"], carry["v"] = kdup_cur, vdup_cur

    stash = {}

    def att_scores(s, i):
        def run():
            kv_prep(s, i)
            valid = band & (in_cur | (j > 0)) if (s == 0 and i == 0) else band
            scores = []
            for p in range(ATT_PAIRS):
                kdup, _ = att_rhs[(s, i)][p // (GROUP // HEADS_PER_TILE)]
                q2 = (proj_tile(s, i, OFF_AQ + p * LANES) * (ATT_HEAD_DIM ** -0.5 * LOG2E)).astype(BF16)
                for e in range(HEADS_PER_TILE):
                    qm = jnp.where(lo_half if e == 0 else hi_half, q2, zeros_bf)
                    sc = jnp.dot(qm, kdup, preferred_element_type=F32)
                    scores.append(jnp.where(valid, sc, NEG_INF))
            stash[("sc", s, i)] = scores
        return run

    def att_softmax(s, i):
        def run():
            probs, sink_terms = [], []
            for head, sc in enumerate(stash.pop(("sc", s, i))):
                sink = sinks_ref[head] * LOG2E
                m = jnp.maximum(jnp.max(sc, axis=-1, keepdims=True), sink)
                probs.append(jnp.exp2(sc - m).astype(BF16))
                sink_terms.append(jnp.exp2(sink - m))
            stash[("p", s, i)] = (probs, sink_terms)
        return run

    def att_values(s, i):
        def run():
            probs, sink_terms = stash.pop(("p", s, i))
            for p in range(ATT_PAIRS):
                _, rhs_v = att_rhs[(s, i)][p // (GROUP // HEADS_PER_TILE)]
                h0 = p * HEADS_PER_TILE
                lhs = jnp.concatenate(probs[h0:h0 + HEADS_PER_TILE], axis=1)
                res = jnp.dot(lhs, rhs_v, preferred_element_type=F32)
                den = res[:, LANES:] + jnp.where(lo_half, sink_terms[h0], sink_terms[h0 + 1])
                a2 = res[:, :LANES] / den
                z2 = proj_tile(s, i, OFF_AZ + p * LANES)
                mix_refs[s][i * BLK:(i + 1) * BLK, p * LANES:(p + 1) * LANES] = (a2 * _silu(z2)).astype(BF16)
        return run

    def ret_scores(s, i):
        def run():
            r0 = (s * BLKS_PER_SUB + i) * BLK
            cos = cos_ref[r0:r0 + BLK, :]
            sin = sin_ref[r0:r0 + BLK, :]

            def rotate(t):
                swapped = jnp.where(even_lane, pltpu.roll(t, LANES - 1, 1), pltpu.roll(t, 1, 1))
                return t * cos + swapped * sin

            per_pair = []
            for p in range(RET_PAIRS):
                q2 = rotate(proj_tile(s, i, OFF_RQ + p * LANES))
                k2 = rotate(proj_tile(s, i, OFF_RK + p * LANES)) * (RET_QK_DIM ** -0.5)
                q2b = q2.astype(BF16)
                qd2b = (q2 * qdec_ref[p]).astype(BF16)
                k2t = k2.T
                k2tb = k2t.astype(BF16)
                kd2tb = (k2t * kdec_ref[p]).astype(BF16)
                rhs_qk = jnp.concatenate([jnp.where(row_lo, k2tb, zeros_bf),
                                          jnp.where(row_lo, zeros_bf, k2tb)], axis=1)
                sc2 = jnp.dot(q2b, rhs_qk, preferred_element_type=F32)
                lhs = []
                for e in range(HEADS_PER_TILE):
                    sel = lo_half if e == 0 else hi_half
                    sc = sc2[:, e * BLK:(e + 1) * BLK] * din_ref[p * HEADS_PER_TILE + e]
                    lhs.append(jnp.concatenate([sc.astype(BF16), jnp.where(sel, qd2b, zeros_bf)], axis=1))
                per_pair.append((lhs, kd2tb))
            stash[("ret", s, i)] = per_pair
        return run

    def ret_outputs(s, i):
        def run():
            per_pair = stash.pop(("ret", s, i))
            for p in range(RET_PAIRS):
                lhs, kd2tb = per_pair[p]
                st = state_ref[p * LANES:(p + 1) * LANES, :]
                st_b = st.astype(BF16)
                v_heads = [proj_tile(s, i, OFF_RV + (p * HEADS_PER_TILE + e) * LANES).astype(BF16)
                           for e in range(HEADS_PER_TILE)]
                for e in range(HEADS_PER_TILE):
                    head = p * HEADS_PER_TILE + e
                    rhs = jnp.concatenate([v_heads[e], st_b], axis=0)
                    o = jnp.dot(lhs[e], rhs, preferred_element_type=F32)
                    mu = jnp.mean(o, axis=-1, keepdims=True)
                    oc = o - mu
                    var = jnp.mean(oc * oc, axis=-1, keepdims=True)
                    on = oc * lax.rsqrt(var + GN_EPS) * gng_ref[:, head * LANES:(head + 1) * LANES]
                    rz = proj_tile(s, i, OFF_RZ + head * LANES)
                    mc = ATT_WIDTH + head * LANES
                    mix_refs[s][i * BLK:(i + 1) * BLK, mc:mc + LANES] = (on * _silu(rz)).astype(BF16)
                kv = jnp.dot(kd2tb, jnp.concatenate(v_heads, axis=1),
                             preferred_element_type=F32)
                kv_sel = jnp.where(row_lo, kv[:, :LANES], kv[:, LANES:])
                state_ref[p * LANES:(p + 1) * LANES, :] = st * cdec_ref[p * LANES:(p + 1) * LANES, :] + kv_sel
        return run

    def mixer_units(s):
        units = []
        for stage in (att_scores, ret_scores, att_softmax, ret_outputs, att_values):
            units += [stage(s, i) for i in range(BLKS_PER_SUB)]
        return units

    for s in range(N_SUB):
        fillers = []
        if s >= 1:
            fillers += [out_proj_chunk(s - 1, c) for c in range(OUT_CHUNKS)] + [finish(s - 1)]
        if s + 1 < N_SUB:
            fillers += [in_proj_chunk(s + 1, c) for c in range(IN_CHUNKS)]
        else:
            fillers += project_first_subtile(xnext_ref)
        _interleave(mixer_units(s), fillers)
    for c in range(OUT_CHUNKS):
        out_proj_chunk(N_SUB - 1, c)()
    finish(N_SUB - 1)()

    for g in range(ATT_KV_HEADS):
        kprev_ref[g] = carry["k"][g]
        vprev_ref[g] = carry["v"][g]


def _retention_tables(seq):
    pos = np.arange(seq, dtype=np.float64)
    theta = 1.0 / (ROT_BASE ** np.linspace(0.0, 1.0, RET_QK_DIM // 2))
    ang = pos[:, None] * theta[None, :]
    cos, sin = np.cos(ang), np.sin(ang)
    cos_l = np.tile(np.repeat(cos, 2, axis=1), (1, HEADS_PER_TILE))
    sign = np.tile(np.array([-1.0, 1.0]), RET_QK_DIM // 2)
    sin_l = np.tile(np.repeat(sin, 2, axis=1) * sign[None, :], (1, HEADS_PER_TILE))

    log_gamma = np.log(1.0 - 2.0 ** (-5.0 - np.arange(RET_HEADS, dtype=np.float64)))
    idx = np.arange(BLK, dtype=np.float64)
    rel = idx[:, None] - idx[None, :]
    decay_in = np.where(rel >= 0, np.exp(log_gamma[:, None, None] * np.maximum(rel, 0.0)), 0.0)
    k_dec = np.exp(log_gamma[:, None] * (BLK - 1 - idx)[None, :])
    q_dec = np.exp(log_gamma[:, None] * (idx + 1)[None, :])
    chunk_decay = np.exp(log_gamma * BLK)

    def per_pair(dec):
        t = np.repeat(dec.T[:, :, None], RET_QK_DIM, axis=2)
        t = t.reshape(BLK, RET_PAIRS, LANES)
        return np.transpose(t, (1, 0, 2))

    cdec = np.broadcast_to(np.repeat(chunk_decay, RET_QK_DIM)[:, None],
                           (RET_QK_WIDTH, RET_V_DIM))
    k_dec_t = np.transpose(per_pair(k_dec), (0, 2, 1))
    tables = (cos_l, sin_l, decay_in, per_pair(q_dec), k_dec_t, cdec)
    return tuple(jnp.asarray(np.ascontiguousarray(a, dtype=np.float32)) for a in tables)


def _full(shape):
    return pl.BlockSpec(shape, lambda step: (0,) * len(shape))


@jax.jit
def kernel(x, norm_g, w_in, att_sinks, ret_gn_g, w_out, final_g):
    batch, seq, d = x.shape
    depth = w_in.shape[0]
    assert depth == 1 and d == D_MODEL and seq % SEQ_TILE == 0
    cos_l, sin_l, decay_in, qdec, kdec, cdec = _retention_tables(seq)
    tile = SEQ_TILE
    n_seq = seq // tile
    n_tiles = batch * n_seq

    def tile_of(step):
        return jnp.maximum(step - CAST_STEPS, 0)

    def this_tile(step):
        t = tile_of(step)
        return (t // n_seq, t % n_seq, 0)

    def next_first_subtile(step):
        u = jnp.minimum(tile_of(step) + 1, n_tiles - 1)
        return (u // n_seq, (u % n_seq) * N_SUB, 0)

    def weight_slab(step):
        return (jnp.minimum(step, CAST_STEPS - 1), 0)

    call = pl.pallas_call(
        functools.partial(_layer_kernel, n_seq),
        grid=(CAST_STEPS + n_tiles,),
        in_specs=[
            pl.BlockSpec(memory_space=pltpu.SMEM),
            pl.BlockSpec((1, tile, D_MODEL), this_tile),
            pl.BlockSpec((1, SUB_TILE, D_MODEL), next_first_subtile),
            _full((1, D_MODEL)),
            pl.BlockSpec((WEIGHT_SLAB, IN_WIDTH), weight_slab),
            _full((1, RET_WIDTH)),
            pl.BlockSpec((WEIGHT_SLAB, D_MODEL), weight_slab),
            _full((1, D_MODEL)),
            pl.BlockSpec((tile, LANES), lambda step: (tile_of(step) % n_seq, 0)),
            pl.BlockSpec((tile, LANES), lambda step: (tile_of(step) % n_seq, 0)),
            _full((RET_HEADS, BLK, BLK)),
            _full((RET_PAIRS, BLK, LANES)),
            _full((RET_PAIRS, BLK, LANES)),
            _full((RET_QK_WIDTH, RET_V_DIM)),
        ],
        out_specs=pl.BlockSpec((1, tile, D_MODEL), this_tile),
        out_shape=jax.ShapeDtypeStruct(x.shape, x.dtype),
        scratch_shapes=[
            pltpu.VMEM((RET_QK_WIDTH, RET_V_DIM), F32),
            pltpu.VMEM((ATT_KV_HEADS, BLK, LANES), BF16),
            pltpu.VMEM((ATT_KV_HEADS, BLK, LANES), BF16),
            pltpu.VMEM((SUB_TILE, IN_WIDTH), F32),
            pltpu.VMEM((D_MODEL, IN_WIDTH), BF16),
            pltpu.VMEM((MIX_WIDTH, D_MODEL), BF16),
        ] + [pltpu.VMEM((SUB_TILE, MIX_WIDTH), BF16)] * N_SUB,
        compiler_params=pltpu.CompilerParams(
            dimension_semantics=("arbitrary",),
            vmem_limit_bytes=VMEM_LIMIT_BYTES),
        name="hymba_layer",
    )
    return call(att_sinks[0], x, x, norm_g[0][None, :], w_in[0],
                ret_gn_g[0][None, :], w_out[0], final_g[None, :],
                cos_l, sin_l, decay_in, qdec, kdec, cdec)
```

```python
import functools

import jax
import jax.numpy as jnp
import numpy as np
from jax import lax
from jax.experimental import pallas as pl
from jax.experimental.pallas import tpu as pltpu

D_MODEL = 1024
ATT_HEADS = 8
ATT_KV_HEADS = 2
ATT_HEAD_DIM = 64
WINDOW = 128
BLK = 128
RET_HEADS = 4
RET_QK_DIM = 64
RET_V_DIM = 128
ROT_BASE = 10000.0
RMS_EPS = 1e-6
GN_EPS = 1e-6
NEG_INF = -1e30
LOG2E = 1.4426950408889634

ATT_WIDTH = ATT_HEADS * ATT_HEAD_DIM
ATT_KV_WIDTH = ATT_KV_HEADS * ATT_HEAD_DIM
RET_QK_WIDTH = RET_HEADS * RET_QK_DIM
RET_WIDTH = RET_HEADS * RET_V_DIM
MIX_WIDTH = ATT_WIDTH + RET_WIDTH
IN_WIDTH = 2 * ATT_WIDTH + 2 * ATT_KV_WIDTH + 2 * RET_QK_WIDTH + 2 * RET_WIDTH

OFF_AQ = 0
OFF_AK = OFF_AQ + ATT_WIDTH
OFF_AV = OFF_AK + ATT_KV_WIDTH
OFF_AZ = OFF_AV + ATT_KV_WIDTH
OFF_RQ = OFF_AZ + ATT_WIDTH
OFF_RK = OFF_RQ + RET_QK_WIDTH
OFF_RV = OFF_RK + RET_QK_WIDTH
OFF_RZ = OFF_RV + RET_WIDTH

LANES = 128
MXU_COLS = 256
N_MXU = 2
PROJ_CHUNK = N_MXU * MXU_COLS
HEADS_PER_TILE = LANES // ATT_HEAD_DIM
ATT_PAIRS = ATT_HEADS // HEADS_PER_TILE
RET_PAIRS = RET_HEADS // HEADS_PER_TILE
GROUP = ATT_HEADS // ATT_KV_HEADS

SEQ_TILE = 1024
SUB_TILE = 256
N_SUB = SEQ_TILE // SUB_TILE
BLKS_PER_SUB = SUB_TILE // BLK
IN_CHUNKS = -(-IN_WIDTH // PROJ_CHUNK)
OUT_CHUNKS = D_MODEL // PROJ_CHUNK
WEIGHT_SLAB = 256
CAST_STEPS = D_MODEL // WEIGHT_SLAB
VMEM_LIMIT_BYTES = 56 * 1024 * 1024

F32 = jnp.float32
BF16 = jnp.bfloat16


def _silu(z):
    hz = 0.5 * z
    return hz + hz * jnp.tanh(hz)


def _interleave(units, fillers):
    done = 0
    for k, unit in enumerate(units):
        unit()
        due = -(-(k + 1) * len(fillers) // len(units))
        while done < due:
            fillers[done]()
            done += 1
    for f in fillers[done:]:
        f()


def _layer_kernel(tiles_per_seq, sinks_ref, x_ref, xnext_ref, ng_ref, win_slab_ref, gng_ref, wout_slab_ref, fg_ref,
                  cos_ref, sin_ref, din_ref, qdec_ref, kdec_ref, cdec_ref,
                  o_ref, state_ref, kprev_ref, vprev_ref, pfirst_ref, win_ref, wout_ref, *mix_refs):
    step = pl.program_id(0)

    @pl.when(step < CAST_STEPS)
    def _():
        rows = pl.ds(pl.multiple_of(step * WEIGHT_SLAB, WEIGHT_SLAB), WEIGHT_SLAB)
        win_ref[rows, :] = win_slab_ref[...].astype(BF16)
        wout_ref[rows, :] = wout_slab_ref[...].astype(BF16)

    @pl.when(step >= CAST_STEPS)
    def _():
        _tile_step(step - CAST_STEPS, tiles_per_seq, sinks_ref, x_ref, xnext_ref, ng_ref, win_ref, gng_ref, wout_ref,
                   fg_ref, cos_ref, sin_ref, din_ref, qdec_ref, kdec_ref, cdec_ref,
                   o_ref, state_ref, kprev_ref, vprev_ref, pfirst_ref, mix_refs)


def _tile_step(t, tiles_per_seq, sinks_ref, x_ref, xnext_ref, ng_ref, win_ref, gng_ref, wout_ref, fg_ref,
               cos_ref, sin_ref, din_ref, qdec_ref, kdec_ref, cdec_ref,
               o_ref, state_ref, kprev_ref, vprev_ref, pfirst_ref, mix_refs):
    j = lax.rem(t, tiles_per_seq)

    def project_first_subtile(src_ref):
        held = {}

        def chunk(c):
            def run():
                if not held:
                    xs = src_ref[0]
                    ms = jnp.mean(xs * xs, axis=-1, keepdims=True)
                    held["h"] = (xs * lax.rsqrt(ms + RMS_EPS) * ng_ref[...]).astype(BF16)
                c0, c1 = c * PROJ_CHUNK, min((c + 1) * PROJ_CHUNK, IN_WIDTH)
                pfirst_ref[:, c0:c1] = jnp.dot(held["h"], win_ref[:, c0:c1], preferred_element_type=F32)
            return run
        return [chunk(c) for c in range(IN_CHUNKS)]

    @pl.when(t == 0)
    def _():
        for f in project_first_subtile(x_ref.at[:, 0:SUB_TILE, :]):
            f()

    @pl.when(j == 0)
    def _():
        state_ref[...] = jnp.zeros_like(state_ref)
        kprev_ref[...] = jnp.zeros_like(kprev_ref)
        vprev_ref[...] = jnp.zeros_like(vprev_ref)

    lane = lax.broadcasted_iota(jnp.int32, (BLK, LANES), 1)
    lo_half = lane < ATT_HEAD_DIM
    hi_half = jnp.logical_not(lo_half)
    even_lane = (lane & 1) == 0
    lane2 = lax.broadcasted_iota(jnp.int32, (2 * BLK, LANES), 1)
    lo_half2 = lane2 < ATT_HEAD_DIM
    qi = lax.broadcasted_iota(jnp.int32, (BLK, 2 * BLK), 0)
    kj = lax.broadcasted_iota(jnp.int32, (BLK, 2 * BLK), 1)
    diff = qi + BLK - kj
    band = (diff >= 0) & (diff < WINDOW)
    in_cur = kj >= BLK
    ones_lo = jnp.where(lo_half2, 1.0, 0.0).astype(BF16)
    ones_hi = jnp.where(lo_half2, 0.0, 1.0).astype(BF16)
    zeros_bf = jnp.zeros((BLK, LANES), BF16)
    row_lo = lax.broadcasted_iota(jnp.int32, (LANES, LANES), 0) < RET_QK_DIM

    proj_chunks = [dict() for _ in range(N_SUB)]
    out_chunks = [dict() for _ in range(N_SUB)]
    normed = {}
    carry = {}
    att_rhs = {}

    def proj_tile(s, i, col):
        if s == 0:
            return pfirst_ref[i * BLK:(i + 1) * BLK, col:col + LANES]
        chunk, off = divmod(col, PROJ_CHUNK)
        return proj_chunks[s][chunk][i * BLK:(i + 1) * BLK, off:off + LANES]

    def in_proj_chunk(s, c):
        def run():
            if s not in normed:
                xs = x_ref[0, s * SUB_TILE:(s + 1) * SUB_TILE, :]
                ms = jnp.mean(xs * xs, axis=-1, keepdims=True)
                normed[s] = (xs * lax.rsqrt(ms + RMS_EPS) * ng_ref[...]).astype(BF16)
            proj_chunks[s][c] = jnp.dot(normed[s], win_ref[:, c * PROJ_CHUNK:min((c + 1) * PROJ_CHUNK, IN_WIDTH)],
                                        preferred_element_type=F32)
        return run

    def out_proj_chunk(s, c):
        def run():
            out_chunks[s][c] = jnp.dot(mix_refs[s][...], wout_ref[:, c * PROJ_CHUNK:(c + 1) * PROJ_CHUNK],
                                       preferred_element_type=F32)
        return run

    def finish(s):
        def run():
            rows = slice(s * SUB_TILE, (s + 1) * SUB_TILE)
            y = x_ref[0, rows, :] + jnp.concatenate([out_chunks[s][c] for c in range(OUT_CHUNKS)], axis=1)
            ms = jnp.mean(y * y, axis=-1, keepdims=True)
            o_ref[0, rows, :] = y * lax.rsqrt(ms + RMS_EPS) * fg_ref[...]
        return run

    def kv_prep(s, i):
        if s == 0 and i == 0:
            carry["k"] = [kprev_ref[g] for g in range(ATT_KV_HEADS)]
            carry["v"] = [vprev_ref[g] for g in range(ATT_KV_HEADS)]
        k_cur = proj_tile(s, i, OFF_AK)
        v_cur = proj_tile(s, i, OFF_AV)
        v_sw = pltpu.roll(v_cur, ATT_HEAD_DIM, 1)
        k_t = k_cur.T
        kdup_cur = [jnp.concatenate([k_t[g * ATT_HEAD_DIM:(g + 1) * ATT_HEAD_DIM]] * HEADS_PER_TILE,
                                    axis=0).astype(BF16) for g in range(ATT_KV_HEADS)]
        vdup_cur = [jnp.where(lo_half, v_cur, v_sw).astype(BF16),
                    jnp.where(lo_half, v_sw, v_cur).astype(BF16)]
        per_head = []
        for g in range(ATT_KV_HEADS):
            kdup = jnp.concatenate([carry["k"][g], kdup_cur[g]], axis=1)
            vdup = jnp.concatenate([carry["v"][g], vdup_cur[g]], axis=0)
            zero2 = jnp.zeros_like(vdup)
            rhs_v = jnp.concatenate([
                jnp.concatenate([jnp.where(lo_half2, vdup, zero2), ones_lo], axis=1),
                jnp.concatenate([jnp.where(lo_half2, zero2, vdup), ones_hi], axis=1),
            ], axis=0)
            per_head.append((kdup, rhs_v))
        att_rhs[(s, i)] = per_head
        carry["k"], carry["v"] = kdup_cur, vdup_cur

    stash = {}

    def att_scores(s, i):
        def run():
            kv_prep(s, i)
            valid = band & (in_cur | (j > 0)) if (s == 0 and i == 0) else band
            scores = []
            for p in range(ATT_PAIRS):
                kdup, _ = att_rhs[(s, i)][p // (GROUP // HEADS_PER_TILE)]
                q2 = (proj_tile(s, i, OFF_AQ + p * LANES) * (ATT_HEAD_DIM ** -0.5 * LOG2E)).astype(BF16)
                for e in range(HEADS_PER_TILE):
                    qm = jnp.where(lo_half if e == 0 else hi_half, q2, zeros_bf)
                    sc = jnp.dot(qm, kdup, preferred_element_type=F32)
                    scores.append(jnp.where(valid, sc, NEG_INF))
            stash[("sc", s, i)] = scores
        return run

    def att_softmax(s, i):
        def run():
            probs, sink_terms = [], []
            for head, sc in enumerate(stash.pop(("sc", s, i))):
                sink = sinks_ref[head] * LOG2E
                m = jnp.maximum(jnp.max(sc, axis=-1, keepdims=True), sink)
                probs.append(jnp.exp2(sc - m).astype(BF16))
                sink_terms.append(jnp.exp2(sink - m))
            stash[("p", s, i)] = (probs, sink_terms)
        return run

    def att_values(s, i):
        def run():
            probs, sink_terms = stash.pop(("p", s, i))
            for p in range(ATT_PAIRS):
                _, rhs_v = att_rhs[(s, i)][p // (GROUP // HEADS_PER_TILE)]
                h0 = p * HEADS_PER_TILE
                lhs = jnp.concatenate(probs[h0:h0 + HEADS_PER_TILE], axis=1)
                res = jnp.dot(lhs, rhs_v, preferred_element_type=F32)
                den = res[:, LANES:] + jnp.where(lo_half, sink_terms[h0], sink_terms[h0 + 1])
                a2 = res[:, :LANES] / den
                z2 = proj_tile(s, i, OFF_AZ + p * LANES)
                mix_refs[s][i * BLK:(i + 1) * BLK, p * LANES:(p + 1) * LANES] = (a2 * _silu(z2)).astype(BF16)
        return run

    def ret_scores(s, i):
        def run():
            r0 = (s * BLKS_PER_SUB + i) * BLK
            cos = cos_ref[r0:r0 + BLK, :]
            sin = sin_ref[r0:r0 + BLK, :]

            def rotate(t):
                swapped = jnp.where(even_lane, pltpu.roll(t, LANES - 1, 1), pltpu.roll(t, 1, 1))
                return t * cos + swapped * sin

            per_pair = []
            for p in range(RET_PAIRS):
                q2 = rotate(proj_tile(s, i, OFF_RQ + p * LANES))
                k2 = rotate(proj_tile(s, i, OFF_RK + p * LANES)) * (RET_QK_DIM ** -0.5)
                q2b = q2.astype(BF16)
                qd2b = (q2 * qdec_ref[p]).astype(BF16)
                k2t = k2.T
                k2tb = k2t.astype(BF16)
                kd2tb = (k2t * kdec_ref[p]).astype(BF16)
                rhs_qk = jnp.concatenate([jnp.where(row_lo, k2tb, zeros_bf),
                                          jnp.where(row_lo, zeros_bf, k2tb)], axis=1)
                sc2 = jnp.dot(q2b, rhs_qk, preferred_element_type=F32)
                decayed = [(sc2[:, e * BLK:(e + 1) * BLK] * din_ref[p * HEADS_PER_TILE + e]).astype(BF16)
                           for e in range(HEADS_PER_TILE)]
                lhs = jnp.concatenate(decayed + [qd2b], axis=1)
                per_pair.append((lhs, kd2tb))
            stash[("ret", s, i)] = per_pair
        return run

    def ret_outputs(s, i):
        def run():
            per_pair = stash.pop(("ret", s, i))
            for p in range(RET_PAIRS):
                lhs, kd2tb = per_pair[p]
                st = state_ref[p * LANES:(p + 1) * LANES, :]
                st_b = st.astype(BF16)
                v_heads = [proj_tile(s, i, OFF_RV + (p * HEADS_PER_TILE + e) * LANES).astype(BF16)
                           for e in range(HEADS_PER_TILE)]
                rhs = jnp.concatenate([
                    jnp.concatenate([v_heads[0], zeros_bf], axis=1),
                    jnp.concatenate([zeros_bf, v_heads[1]], axis=1),
                    jnp.concatenate([jnp.where(row_lo, st_b, zeros_bf), jnp.where(row_lo, zeros_bf, st_b)], axis=1),
                ], axis=0)
                o2 = jnp.dot(lhs, rhs, preferred_element_type=F32)
                for e in range(HEADS_PER_TILE):
                    head = p * HEADS_PER_TILE + e
                    o = o2[:, e * LANES:(e + 1) * LANES]
                    mu = jnp.mean(o, axis=-1, keepdims=True)
                    oc = o - mu
                    var = jnp.mean(oc * oc, axis=-1, keepdims=True)
                    on = oc * lax.rsqrt(var + GN_EPS) * gng_ref[:, head * LANES:(head + 1) * LANES]
                    rz = proj_tile(s, i, OFF_RZ + head * LANES)
                    mc = ATT_WIDTH + head * LANES
                    mix_refs[s][i * BLK:(i + 1) * BLK, mc:mc + LANES] = (on * _silu(rz)).astype(BF16)
                kv = jnp.dot(kd2tb, jnp.concatenate(v_heads, axis=1),
                             preferred_element_type=F32)
                kv_sel = jnp.where(row_lo, kv[:, :LANES], kv[:, LANES:])
                state_ref[p * LANES:(p + 1) * LANES, :] = st * cdec_ref[p * LANES:(p + 1) * LANES, :] + kv_sel
        return run

    def mixer_units(s):
        units = []
        for stage in (att_scores, ret_scores, att_softmax, ret_outputs, att_values):
            units += [stage(s, i) for i in range(BLKS_PER_SUB)]
        return units

    for s in range(N_SUB):
        fillers = []
        if s >= 1:
            fillers += [out_proj_chunk(s - 1, c) for c in range(OUT_CHUNKS)] + [finish(s - 1)]
        if s + 1 < N_SUB:
            fillers += [in_proj_chunk(s + 1, c) for c in range(IN_CHUNKS)]
        else:
            fillers += project_first_subtile(xnext_ref)
        _interleave(mixer_units(s), fillers)
    for c in range(OUT_CHUNKS):
        out_proj_chunk(N_SUB - 1, c)()
    finish(N_SUB - 1)()

    for g in range(ATT_KV_HEADS):
        kprev_ref[g] = carry["k"][g]
        vprev_ref[g] = carry["v"][g]


def _retention_tables(seq):
    pos = np.arange(seq, dtype=np.float64)
    theta = 1.0 / (ROT_BASE ** np.linspace(0.0, 1.0, RET_QK_DIM // 2))
    ang = pos[:, None] * theta[None, :]
    cos, sin = np.cos(ang), np.sin(ang)
    cos_l = np.tile(np.repeat(cos, 2, axis=1), (1, HEADS_PER_TILE))
    sign = np.tile(np.array([-1.0, 1.0]), RET_QK_DIM // 2)
    sin_l = np.tile(np.repeat(sin, 2, axis=1) * sign[None, :], (1, HEADS_PER_TILE))

    log_gamma = np.log(1.0 - 2.0 ** (-5.0 - np.arange(RET_HEADS, dtype=np.float64)))
    idx = np.arange(BLK, dtype=np.float64)
    rel = idx[:, None] - idx[None, :]
    decay_in = np.where(rel >= 0, np.exp(log_gamma[:, None, None] * np.maximum(rel, 0.0)), 0.0)
    k_dec = np.exp(log_gamma[:, None] * (BLK - 1 - idx)[None, :])
    q_dec = np.exp(log_gamma[:, None] * (idx + 1)[None, :])
    chunk_decay = np.exp(log_gamma * BLK)

    def per_pair(dec):
        t = np.repeat(dec.T[:, :, None], RET_QK_DIM, axis=2)
        t = t.reshape(BLK, RET_PAIRS, LANES)
        return np.transpose(t, (1, 0, 2))

    cdec = np.broadcast_to(np.repeat(chunk_decay, RET_QK_DIM)[:, None],
                           (RET_QK_WIDTH, RET_V_DIM))
    k_dec_t = np.transpose(per_pair(k_dec), (0, 2, 1))
    tables = (cos_l, sin_l, decay_in, per_pair(q_dec), k_dec_t, cdec)
    return tuple(jnp.asarray(np.ascontiguousarray(a, dtype=np.float32)) for a in tables)


def _full(shape):
    return pl.BlockSpec(shape, lambda step: (0,) * len(shape))


@jax.jit
def kernel(x, norm_g, w_in, att_sinks, ret_gn_g, w_out, final_g):
    batch, seq, d = x.shape
    depth = w_in.shape[0]
    assert depth == 1 and d == D_MODEL and seq % SEQ_TILE == 0
    cos_l, sin_l, decay_in, qdec, kdec, cdec = _retention_tables(seq)
    tile = SEQ_TILE
    n_seq = seq // tile
    n_tiles = batch * n_seq

    def tile_of(step):
        return jnp.maximum(step - CAST_STEPS, 0)

    def this_tile(step):
        t = tile_of(step)
        return (t // n_seq, t % n_seq, 0)

    def next_first_subtile(step):
        u = jnp.minimum(tile_of(step) + 1, n_tiles - 1)
        return (u // n_seq, (u % n_seq) * N_SUB, 0)

    def weight_slab(step):
        return (jnp.minimum(step, CAST_STEPS - 1), 0)

    call = pl.pallas_call(
        functools.partial(_layer_kernel, n_seq),
        grid=(CAST_STEPS + n_tiles,),
        in_specs=[
            pl.BlockSpec(memory_space=pltpu.SMEM),
            pl.BlockSpec((1, tile, D_MODEL), this_tile),
            pl.BlockSpec((1, SUB_TILE, D_MODEL), next_first_subtile),
            _full((1, D_MODEL)),
            pl.BlockSpec((WEIGHT_SLAB, IN_WIDTH), weight_slab),
            _full((1, RET_WIDTH)),
            pl.BlockSpec((WEIGHT_SLAB, D_MODEL), weight_slab),
            _full((1, D_MODEL)),
            pl.BlockSpec((tile, LANES), lambda step: (tile_of(step) % n_seq, 0)),
            pl.BlockSpec((tile, LANES), lambda step: (tile_of(step) % n_seq, 0)),
            _full((RET_HEADS, BLK, BLK)),
            _full((RET_PAIRS, BLK, LANES)),
            _full((RET_PAIRS, BLK, LANES)),
            _full((RET_QK_WIDTH, RET_V_DIM)),
        ],
        out_specs=pl.BlockSpec((1, tile, D_MODEL), this_tile),
        out_shape=jax.ShapeDtypeStruct(x.shape, x.dtype),
        scratch_shapes=[
            pltpu.VMEM((RET_QK_WIDTH, RET_V_DIM), F32),
            pltpu.VMEM((ATT_KV_HEADS, BLK, LANES), BF16),
            pltpu.VMEM((ATT_KV_HEADS, BLK, LANES), BF16),
            pltpu.VMEM((SUB_TILE, IN_WIDTH), F32),
            pltpu.VMEM((D_MODEL, IN_WIDTH), BF16),
            pltpu.VMEM((MIX_WIDTH, D_MODEL), BF16),
        ] + [pltpu.VMEM((SUB_TILE, MIX_WIDTH), BF16)] * N_SUB,
        compiler_params=pltpu.CompilerParams(
            dimension_semantics=("arbitrary",),
            vmem_limit_bytes=VMEM_LIMIT_BYTES),
        name="hymba_layer",
    )
    return call(att_sinks[0], x, x, norm_g[0][None, :], w_in[0],
                ret_gn_g[0][None, :], w_out[0], final_g[None, :],
                cos_l, sin_l, decay_in, qdec, kdec, cdec)
```

```python
import functools

import jax
import jax.numpy as jnp
import numpy as np
from jax import lax
from jax.experimental import pallas as pl
from jax.experimental.pallas import tpu as pltpu

D_MODEL = 1024
ATT_HEADS = 8
ATT_KV_HEADS = 2
ATT_HEAD_DIM = 64
WINDOW = 128
BLK = 128
RET_HEADS = 4
RET_QK_DIM = 64
RET_V_DIM = 128
ROT_BASE = 10000.0
RMS_EPS = 1e-6
GN_EPS = 1e-6
NEG_INF = -1e30
LOG2E = 1.4426950408889634

ATT_WIDTH = ATT_HEADS * ATT_HEAD_DIM
ATT_KV_WIDTH = ATT_KV_HEADS * ATT_HEAD_DIM
RET_QK_WIDTH = RET_HEADS * RET_QK_DIM
RET_WIDTH = RET_HEADS * RET_V_DIM
MIX_WIDTH = ATT_WIDTH + RET_WIDTH
IN_WIDTH = 2 * ATT_WIDTH + 2 * ATT_KV_WIDTH + 2 * RET_QK_WIDTH + 2 * RET_WIDTH

OFF_AQ = 0
OFF_AK = OFF_AQ + ATT_WIDTH
OFF_AV = OFF_AK + ATT_KV_WIDTH
OFF_AZ = OFF_AV + ATT_KV_WIDTH
OFF_RQ = OFF_AZ + ATT_WIDTH
OFF_RK = OFF_RQ + RET_QK_WIDTH
OFF_RV = OFF_RK + RET_QK_WIDTH
OFF_RZ = OFF_RV + RET_WIDTH

LANES = 128
MXU_COLS = 256
N_MXU = 2
PROJ_CHUNK = N_MXU * MXU_COLS
HEADS_PER_TILE = LANES // ATT_HEAD_DIM
ATT_PAIRS = ATT_HEADS // HEADS_PER_TILE
RET_PAIRS = RET_HEADS // HEADS_PER_TILE
GROUP = ATT_HEADS // ATT_KV_HEADS

SEQ_TILE = 1024
SUB_TILE = 256
N_SUB = SEQ_TILE // SUB_TILE
BLKS_PER_SUB = SUB_TILE // BLK
IN_CHUNKS = -(-IN_WIDTH // PROJ_CHUNK)
OUT_CHUNKS = D_MODEL // PROJ_CHUNK
WEIGHT_SLAB = 256
CAST_STEPS = D_MODEL // WEIGHT_SLAB
VMEM_LIMIT_BYTES = 56 * 1024 * 1024

F32 = jnp.float32
BF16 = jnp.bfloat16


def _silu(z):
    hz = 0.5 * z
    return hz + hz * jnp.tanh(hz)


def _interleave(units, fillers):
    done = 0
    for k, unit in enumerate(units):
        unit()
        due = -(-(k + 1) * len(fillers) // len(units))
        while done < due:
            fillers[done]()
            done += 1
    for f in fillers[done:]:
        f()


def _layer_kernel(tiles_per_seq, sinks_ref, x_ref, xnext_ref, ng_ref, win_slab_ref, gng_ref, wout_slab_ref, fg_ref,
                  cos_ref, sin_ref, din_ref, qdec_ref, kdec_ref, cdec_ref,
                  o_ref, state_ref, kprev_ref, vprev_ref, pfirst_ref, win_ref, wout_ref, *mix_refs):
    step = pl.program_id(0)

    @pl.when(step < CAST_STEPS)
    def _():
        rows = pl.ds(pl.multiple_of(step * WEIGHT_SLAB, WEIGHT_SLAB), WEIGHT_SLAB)
        win_ref[rows, :] = win_slab_ref[...].astype(BF16)
        wout_ref[rows, :] = wout_slab_ref[...].astype(BF16)

    @pl.when(step >= CAST_STEPS)
    def _():
        _tile_step(step - CAST_STEPS, tiles_per_seq, sinks_ref, x_ref, xnext_ref, ng_ref, win_ref, gng_ref, wout_ref,
                   fg_ref, cos_ref, sin_ref, din_ref, qdec_ref, kdec_ref, cdec_ref,
                   o_ref, state_ref, kprev_ref, vprev_ref, pfirst_ref, mix_refs)


def _tile_step(t, tiles_per_seq, sinks_ref, x_ref, xnext_ref, ng_ref, win_ref, gng_ref, wout_ref, fg_ref,
               cos_ref, sin_ref, din_ref, qdec_ref, kdec_ref, cdec_ref,
               o_ref, state_ref, kprev_ref, vprev_ref, pfirst_ref, mix_refs):
    j = lax.rem(t, tiles_per_seq)

    def project_first_subtile(src_ref):
        held = {}

        def chunk(c):
            def run():
                if not held:
                    xs = src_ref[0]
                    ms = jnp.mean(xs * xs, axis=-1, keepdims=True)
                    held["h"] = (xs * lax.rsqrt(ms + RMS_EPS) * ng_ref[...]).astype(BF16)
                c0, c1 = c * PROJ_CHUNK, min((c + 1) * PROJ_CHUNK, IN_WIDTH)
                pfirst_ref[:, c0:c1] = jnp.dot(held["h"], win_ref[:, c0:c1], preferred_element_type=F32)
            return run
        return [chunk(c) for c in range(IN_CHUNKS)]

    @pl.when(t == 0)
    def _():
        for f in project_first_subtile(x_ref.at[:, 0:SUB_TILE, :]):
            f()

    @pl.when(j == 0)
    def _():
        state_ref[...] = jnp.zeros_like(state_ref)
        kprev_ref[...] = jnp.zeros_like(kprev_ref)
        vprev_ref[...] = jnp.zeros_like(vprev_ref)

    lane = lax.broadcasted_iota(jnp.int32, (BLK, LANES), 1)
    lo_half = lane < ATT_HEAD_DIM
    hi_half = jnp.logical_not(lo_half)
    even_lane = (lane & 1) == 0
    lane2 = lax.broadcasted_iota(jnp.int32, (2 * BLK, LANES), 1)
    lo_half2 = lane2 < ATT_HEAD_DIM
    qi = lax.broadcasted_iota(jnp.int32, (BLK, 2 * BLK), 0)
    kj = lax.broadcasted_iota(jnp.int32, (BLK, 2 * BLK), 1)
    diff = qi + BLK - kj
    band = (diff >= 0) & (diff < WINDOW)
    in_cur = kj >= BLK
    ones_lo = jnp.where(lo_half2, 1.0, 0.0).astype(BF16)
    ones_hi = jnp.where(lo_half2, 0.0, 1.0).astype(BF16)
    zeros_bf = jnp.zeros((BLK, LANES), BF16)
    row_lo = lax.broadcasted_iota(jnp.int32, (LANES, LANES), 0) < RET_QK_DIM

    proj_chunks = [dict() for _ in range(N_SUB)]
    out_chunks = [dict() for _ in range(N_SUB)]
    normed = {}
    carry = {}
    att_rhs = {}

    def proj_tile(s, i, col):
        if s == 0:
            return pfirst_ref[i * BLK:(i + 1) * BLK, col:col + LANES]
        chunk, off = divmod(col, PROJ_CHUNK)
        return proj_chunks[s][chunk][i * BLK:(i + 1) * BLK, off:off + LANES]

    def in_proj_chunk(s, c):
        def run():
            if s not in normed:
                xs = x_ref[0, s * SUB_TILE:(s + 1) * SUB_TILE, :]
                ms = jnp.mean(xs * xs, axis=-1, keepdims=True)
                normed[s] = (xs * lax.rsqrt(ms + RMS_EPS) * ng_ref[...]).astype(BF16)
            proj_chunks[s][c] = jnp.dot(normed[s], win_ref[:, c * PROJ_CHUNK:min((c + 1) * PROJ_CHUNK, IN_WIDTH)],
                                        preferred_element_type=F32)
        return run

    def out_proj_chunk(s, c):
        def run():
            out_chunks[s][c] = jnp.dot(mix_refs[s][...], wout_ref[:, c * PROJ_CHUNK:(c + 1) * PROJ_CHUNK],
                                       preferred_element_type=F32)
        return run

    def finish(s):
        def run():
            rows = slice(s * SUB_TILE, (s + 1) * SUB_TILE)
            y = x_ref[0, rows, :] + jnp.concatenate([out_chunks[s][c] for c in range(OUT_CHUNKS)], axis=1)
            ms = jnp.mean(y * y, axis=-1, keepdims=True)
            o_ref[0, rows, :] = y * lax.rsqrt(ms + RMS_EPS) * fg_ref[...]
        return run

    def kv_prep(s, i):
        if s == 0 and i == 0:
            carry["k"] = [kprev_ref[g] for g in range(ATT_KV_HEADS)]
            carry["v"] = [vprev_ref[g] for g in range(ATT_KV_HEADS)]
        k_cur = proj_tile(s, i, OFF_AK)
        v_cur = proj_tile(s, i, OFF_AV)
        v_sw = pltpu.roll(v_cur, ATT_HEAD_DIM, 1)
        k_t = k_cur.T
        kdup_cur = [jnp.concatenate([k_t[g * ATT_HEAD_DIM:(g + 1) * ATT_HEAD_DIM]] * HEADS_PER_TILE,
                                    axis=0).astype(BF16) for g in range(ATT_KV_HEADS)]
        vdup_cur = [jnp.where(lo_half, v_cur, v_sw).astype(BF16),
                    jnp.where(lo_half, v_sw, v_cur).astype(BF16)]
        per_head = []
        for g in range(ATT_KV_HEADS):
            kdup = jnp.concatenate([carry["k"][g], kdup_cur[g]], axis=1)
            vdup = jnp.concatenate([carry["v"][g], vdup_cur[g]], axis=0)
            zero2 = jnp.zeros_like(vdup)
            rhs_v = jnp.concatenate([
                jnp.concatenate([jnp.where(lo_half2, vdup, zero2), ones_lo], axis=1),
                jnp.concatenate([jnp.where(lo_half2, zero2, vdup), ones_hi], axis=1),
            ], axis=0)
            per_head.append((kdup, rhs_v))
        att_rhs[(s, i)] = per_head
        carry["k"], carry["v"] = kdup_cur, vdup_cur

    stash = {}

    def att_scores(s, i):
        def run():
            kv_prep(s, i)
            valid = band & (in_cur | (j > 0)) if (s == 0 and i == 0) else band
            scores = []
            for p in range(ATT_PAIRS):
                kdup, _ = att_rhs[(s, i)][p // (GROUP // HEADS_PER_TILE)]
                q2 = (proj_tile(s, i, OFF_AQ + p * LANES) * (ATT_HEAD_DIM ** -0.5 * LOG2E)).astype(BF16)
                for e in range(HEADS_PER_TILE):
                    qm = jnp.where(lo_half if e == 0 else hi_half, q2, zeros_bf)
                    sc = jnp.dot(qm, kdup, preferred_element_type=F32)
                    scores.append(jnp.where(valid, sc, NEG_INF))
            stash[("sc", s, i)] = scores
        return run

    def att_softmax(s, i):
        def run():
            probs, sink_terms = [], []
            for head, sc in enumerate(stash.pop(("sc", s, i))):
                sink = sinks_ref[head] * LOG2E
                m = jnp.maximum(jnp.max(sc, axis=-1, keepdims=True), sink)
                probs.append(jnp.exp2(sc - m).astype(BF16))
                sink_terms.append(jnp.exp2(sink - m))
            stash[("p", s, i)] = (probs, sink_terms)
        return run

    def att_values(s, i):
        def run():
            probs, sink_terms = stash.pop(("p", s, i))
            for p in range(ATT_PAIRS):
                _, rhs_v = att_rhs[(s, i)][p // (GROUP // HEADS_PER_TILE)]
                h0 = p * HEADS_PER_TILE
                lhs = jnp.concatenate(probs[h0:h0 + HEADS_PER_TILE], axis=1)
                res = jnp.dot(lhs, rhs_v, preferred_element_type=F32)
                den = res[:, LANES:] + jnp.where(lo_half, sink_terms[h0], sink_terms[h0 + 1])
                a2 = res[:, :LANES] / den
                z2 = proj_tile(s, i, OFF_AZ + p * LANES)
                mix_refs[s][i * BLK:(i + 1) * BLK, p * LANES:(p + 1) * LANES] = (a2 * _silu(z2)).astype(BF16)
        return run

    def ret_scores(s, i):
        def run():
            r0 = (s * BLKS_PER_SUB + i) * BLK
            cos = cos_ref[r0:r0 + BLK, :]
            sin = sin_ref[r0:r0 + BLK, :]

            def rotate(t):
                swapped = jnp.where(even_lane, pltpu.roll(t, LANES - 1, 1), pltpu.roll(t, 1, 1))
                return t * cos + swapped * sin

            per_pair = []
            for p in range(RET_PAIRS):
                q2 = rotate(proj_tile(s, i, OFF_RQ + p * LANES))
                k2 = rotate(proj_tile(s, i, OFF_RK + p * LANES)) * (RET_QK_DIM ** -0.5)
                q2b = q2.astype(BF16)
                qd2b = (q2 * qdec_ref[p]).astype(BF16)
                k2t = k2.T
                k2tb = k2t.astype(BF16)
                kd2tb = (k2t * kdec_ref[p]).astype(BF16)
                rhs_qk = jnp.concatenate([jnp.where(row_lo, k2tb, zeros_bf),
                                          jnp.where(row_lo, zeros_bf, k2tb)], axis=1)
                sc2 = jnp.dot(q2b, rhs_qk, preferred_element_type=F32)
                lhs = []
                for e in range(HEADS_PER_TILE):
                    sel = lo_half if e == 0 else hi_half
                    sc = sc2[:, e * BLK:(e + 1) * BLK] * din_ref[p * HEADS_PER_TILE + e]
                    lhs.append(jnp.concatenate([sc.astype(BF16), jnp.where(sel, qd2b, zeros_bf)], axis=1))
                per_pair.append((lhs, kd2tb))
            stash[("ret", s, i)] = per_pair
        return run

    def ret_outputs(s, i):
        def run():
            per_pair = stash.pop(("ret", s, i))
            for p in range(RET_PAIRS):
                lhs, kd2tb = per_pair[p]
                st = state_ref[p * LANES:(p + 1) * LANES, :]
                st_b = st.astype(BF16)
                v_heads = [proj_tile(s, i, OFF_RV + (p * HEADS_PER_TILE + e) * LANES).astype(BF16)
                           for e in range(HEADS_PER_TILE)]
                for e in range(HEADS_PER_TILE):
                    head = p * HEADS_PER_TILE + e
                    rhs = jnp.concatenate([v_heads[e], st_b], axis=0)
                    o = jnp.dot(lhs[e], rhs, preferred_element_type=F32)
                    mu = jnp.mean(o, axis=-1, keepdims=True)
                    oc = o - mu
                    var = jnp.mean(oc * oc, axis=-1, keepdims=True)
                    on = oc * lax.rsqrt(var + GN_EPS) * gng_ref[:, head * LANES:(head + 1) * LANES]
                    rz = proj_tile(s, i, OFF_RZ + head * LANES)
                    mc = ATT_WIDTH + head * LANES
                    mix_refs[s][i * BLK:(i + 1) * BLK, mc:mc + LANES] = (on * _silu(rz)).astype(BF16)
                kv = jnp.dot(kd2tb, jnp.concatenate(v_heads, axis=1),
                             preferred_element_type=F32)
                kv_sel = jnp.where(row_lo, kv[:, :LANES], kv[:, LANES:])
                state_ref[p * LANES:(p + 1) * LANES, :] = st * cdec_ref[p * LANES:(p + 1) * LANES, :] + kv_sel
        return run

    def mixer_units(s):
        units = []
        for stage in (att_scores, ret_scores, att_softmax, ret_outputs, att_values):
            units += [stage(s, i) for i in range(BLKS_PER_SUB)]
        return units

    for s in range(N_SUB):
        fillers = []
        if s + 1 < N_SUB:
            fillers += [in_proj_chunk(s + 1, c) for c in range(IN_CHUNKS)]
        else:
            fillers += project_first_subtile(xnext_ref)
        if s >= 1:
            fillers[2:2] = [out_proj_chunk(s - 1, c) for c in range(OUT_CHUNKS)] + [finish(s - 1)]
        _interleave(mixer_units(s), fillers)
    for c in range(OUT_CHUNKS):
        out_proj_chunk(N_SUB - 1, c)()
    finish(N_SUB - 1)()

    for g in range(ATT_KV_HEADS):
        kprev_ref[g] = carry["k"][g]
        vprev_ref[g] = carry["v"][g]


def _retention_tables(seq):
    pos = np.arange(seq, dtype=np.float64)
    theta = 1.0 / (ROT_BASE ** np.linspace(0.0, 1.0, RET_QK_DIM // 2))
    ang = pos[:, None] * theta[None, :]
    cos, sin = np.cos(ang), np.sin(ang)
    cos_l = np.tile(np.repeat(cos, 2, axis=1), (1, HEADS_PER_TILE))
    sign = np.tile(np.array([-1.0, 1.0]), RET_QK_DIM // 2)
    sin_l = np.tile(np.repeat(sin, 2, axis=1) * sign[None, :], (1, HEADS_PER_TILE))

    log_gamma = np.log(1.0 - 2.0 ** (-5.0 - np.arange(RET_HEADS, dtype=np.float64)))
    idx = np.arange(BLK, dtype=np.float64)
    rel = idx[:, None] - idx[None, :]
    decay_in = np.where(rel >= 0, np.exp(log_gamma[:, None, None] * np.maximum(rel, 0.0)), 0.0)
    k_dec = np.exp(log_gamma[:, None] * (BLK - 1 - idx)[None, :])
    q_dec = np.exp(log_gamma[:, None] * (idx + 1)[None, :])
    chunk_decay = np.exp(log_gamma * BLK)

    def per_pair(dec):
        t = np.repeat(dec.T[:, :, None], RET_QK_DIM, axis=2)
        t = t.reshape(BLK, RET_PAIRS, LANES)
        return np.transpose(t, (1, 0, 2))

    cdec = np.broadcast_to(np.repeat(chunk_decay, RET_QK_DIM)[:, None],
                           (RET_QK_WIDTH, RET_V_DIM))
    k_dec_t = np.transpose(per_pair(k_dec), (0, 2, 1))
    tables = (cos_l, sin_l, decay_in, per_pair(q_dec), k_dec_t, cdec)
    return tuple(jnp.asarray(np.ascontiguousarray(a, dtype=np.float32)) for a in tables)


def _full(shape):
    return pl.BlockSpec(shape, lambda step: (0,) * len(shape))


@jax.jit
def kernel(x, norm_g, w_in, att_sinks, ret_gn_g, w_out, final_g):
    batch, seq, d = x.shape
    depth = w_in.shape[0]
    assert depth == 1 and d == D_MODEL and seq % SEQ_TILE == 0
    cos_l, sin_l, decay_in, qdec, kdec, cdec = _retention_tables(seq)
    tile = SEQ_TILE
    n_seq = seq // tile
    n_tiles = batch * n_seq

    def tile_of(step):
        return jnp.maximum(step - CAST_STEPS, 0)

    def this_tile(step):
        t = tile_of(step)
        return (t // n_seq, t % n_seq, 0)

    def next_first_subtile(step):
        u = jnp.minimum(tile_of(step) + 1, n_tiles - 1)
        return (u // n_seq, (u % n_seq) * N_SUB, 0)

    def weight_slab(step):
        return (jnp.minimum(step, CAST_STEPS - 1), 0)

    call = pl.pallas_call(
        functools.partial(_layer_kernel, n_seq),
        grid=(CAST_STEPS + n_tiles,),
        in_specs=[
            pl.BlockSpec(memory_space=pltpu.SMEM),
            pl.BlockSpec((1, tile, D_MODEL), this_tile),
            pl.BlockSpec((1, SUB_TILE, D_MODEL), next_first_subtile),
            _full((1, D_MODEL)),
            pl.BlockSpec((WEIGHT_SLAB, IN_WIDTH), weight_slab),
            _full((1, RET_WIDTH)),
            pl.BlockSpec((WEIGHT_SLAB, D_MODEL), weight_slab),
            _full((1, D_MODEL)),
            pl.BlockSpec((tile, LANES), lambda step: (tile_of(step) % n_seq, 0)),
            pl.BlockSpec((tile, LANES), lambda step: (tile_of(step) % n_seq, 0)),
            _full((RET_HEADS, BLK, BLK)),
            _full((RET_PAIRS, BLK, LANES)),
            _full((RET_PAIRS, BLK, LANES)),
            _full((RET_QK_WIDTH, RET_V_DIM)),
        ],
        out_specs=pl.BlockSpec((1, tile, D_MODEL), this_tile),
        out_shape=jax.ShapeDtypeStruct(x.shape, x.dtype),
        scratch_shapes=[
            pltpu.VMEM((RET_QK_WIDTH, RET_V_DIM), F32),
            pltpu.VMEM((ATT_KV_HEADS, BLK, LANES), BF16),
            pltpu.VMEM((ATT_KV_HEADS, BLK, LANES), BF16),
            pltpu.VMEM((SUB_TILE, IN_WIDTH), F32),
            pltpu.VMEM((D_MODEL, IN_WIDTH), BF16),
            pltpu.VMEM((MIX_WIDTH, D_MODEL), BF16),
        ] + [pltpu.VMEM((SUB_TILE, MIX_WIDTH), BF16)] * N_SUB,
        compiler_params=pltpu.CompilerParams(
            dimension_semantics=("arbitrary",),
            vmem_limit_bytes=VMEM_LIMIT_BYTES),
        name="hymba_layer",
    )
    return call(att_sinks[0], x, x, norm_g[0][None, :], w_in[0],
                ret_gn_g[0][None, :], w_out[0], final_g[None, :],
                cos_l, sin_l, decay_in, qdec, kdec, cdec)
```

```python
import functools

import jax
import jax.numpy as jnp
import numpy as np
from jax import lax
from jax.experimental import pallas as pl
from jax.experimental.pallas import tpu as pltpu

D_MODEL = 1024
ATT_HEADS = 8
ATT_KV_HEADS = 2
ATT_HEAD_DIM = 64
WINDOW = 128
BLK = 128
RET_HEADS = 4
RET_QK_DIM = 64
RET_V_DIM = 128
ROT_BASE = 10000.0
RMS_EPS = 1e-6
GN_EPS = 1e-6
NEG_INF = -1e30
LOG2E = 1.4426950408889634

ATT_WIDTH = ATT_HEADS * ATT_HEAD_DIM
ATT_KV_WIDTH = ATT_KV_HEADS * ATT_HEAD_DIM
RET_QK_WIDTH = RET_HEADS * RET_QK_DIM
RET_WIDTH = RET_HEADS * RET_V_DIM
MIX_WIDTH = ATT_WIDTH + RET_WIDTH
IN_WIDTH = 2 * ATT_WIDTH + 2 * ATT_KV_WIDTH + 2 * RET_QK_WIDTH + 2 * RET_WIDTH

OFF_AQ = 0
OFF_AK = OFF_AQ + ATT_WIDTH
OFF_AV = OFF_AK + ATT_KV_WIDTH
OFF_AZ = OFF_AV + ATT_KV_WIDTH
OFF_RQ = OFF_AZ + ATT_WIDTH
OFF_RK = OFF_RQ + RET_QK_WIDTH
OFF_RV = OFF_RK + RET_QK_WIDTH
OFF_RZ = OFF_RV + RET_WIDTH

LANES = 128
MXU_COLS = 256
N_MXU = 2
PROJ_CHUNK = N_MXU * MXU_COLS
HEADS_PER_TILE = LANES // ATT_HEAD_DIM
ATT_PAIRS = ATT_HEADS // HEADS_PER_TILE
RET_PAIRS = RET_HEADS // HEADS_PER_TILE
GROUP = ATT_HEADS // ATT_KV_HEADS

SEQ_TILE = 1024
SUB_TILE = 256
N_SUB = SEQ_TILE // SUB_TILE
BLKS_PER_SUB = SUB_TILE // BLK
IN_CHUNKS = -(-IN_WIDTH // PROJ_CHUNK)
OUT_CHUNKS = D_MODEL // PROJ_CHUNK
WEIGHT_SLAB = 256
CAST_STEPS = D_MODEL // WEIGHT_SLAB
VMEM_LIMIT_BYTES = 56 * 1024 * 1024

F32 = jnp.float32
BF16 = jnp.bfloat16


def _silu(z):
    hz = 0.5 * z
    return hz + hz * jnp.tanh(hz)


def _interleave(units, fillers):
    done = 0
    for k, unit in enumerate(units):
        unit()
        due = -(-(k + 1) * len(fillers) // len(units))
        while done < due:
            fillers[done]()
            done += 1
    for f in fillers[done:]:
        f()


def _layer_kernel(tiles_per_seq, sinks_ref, x_ref, xnext_ref, ng_ref, win_slab_ref, gng_ref, wout_slab_ref, fg_ref,
                  cos_ref, sin_ref, din_ref, qdec_ref, kdec_ref, cdec_ref,
                  o_ref, state_ref, kprev_ref, vprev_ref, pfirst_ref, win_ref, wout_ref, *mix_refs):
    step = pl.program_id(0)

    @pl.when(step < CAST_STEPS)
    def _():
        rows = pl.ds(pl.multiple_of(step * WEIGHT_SLAB, WEIGHT_SLAB), WEIGHT_SLAB)
        win_ref[rows, :] = win_slab_ref[...].astype(BF16)
        wout_ref[rows, :] = wout_slab_ref[...].astype(BF16)

    @pl.when(step >= CAST_STEPS)
    def _():
        _tile_step(step - CAST_STEPS, tiles_per_seq, sinks_ref, x_ref, xnext_ref, ng_ref, win_ref, gng_ref, wout_ref,
                   fg_ref, cos_ref, sin_ref, din_ref, qdec_ref, kdec_ref, cdec_ref,
                   o_ref, state_ref, kprev_ref, vprev_ref, pfirst_ref, mix_refs)


def _tile_step(t, tiles_per_seq, sinks_ref, x_ref, xnext_ref, ng_ref, win_ref, gng_ref, wout_ref, fg_ref,
               cos_ref, sin_ref, din_ref, qdec_ref, kdec_ref, cdec_ref,
               o_ref, state_ref, kprev_ref, vprev_ref, pfirst_ref, mix_refs):
    j = lax.rem(t, tiles_per_seq)

    def project_first_subtile(src_ref):
        held = {}

        def chunk(c):
            def run():
                if not held:
                    xs = src_ref[0]
                    ms = jnp.mean(xs * xs, axis=-1, keepdims=True)
                    held["h"] = (xs * lax.rsqrt(ms + RMS_EPS) * ng_ref[...]).astype(BF16)
                c0, c1 = c * PROJ_CHUNK, min((c + 1) * PROJ_CHUNK, IN_WIDTH)
                pfirst_ref[:, c0:c1] = jnp.dot(held["h"], win_ref[:, c0:c1], preferred_element_type=F32)
            return run
        return [chunk(c) for c in range(IN_CHUNKS)]

    @pl.when(t == 0)
    def _():
        for f in project_first_subtile(x_ref.at[:, 0:SUB_TILE, :]):
            f()

    @pl.when(j == 0)
    def _():
        state_ref[...] = jnp.zeros_like(state_ref)
        kprev_ref[...] = jnp.zeros_like(kprev_ref)
        vprev_ref[...] = jnp.zeros_like(vprev_ref)

    lane = lax.broadcasted_iota(jnp.int32, (BLK, LANES), 1)
    lo_half = lane < ATT_HEAD_DIM
    hi_half = jnp.logical_not(lo_half)
    even_lane = (lane & 1) == 0
    lane2 = lax.broadcasted_iota(jnp.int32, (2 * BLK, LANES), 1)
    lo_half2 = lane2 < ATT_HEAD_DIM
    qi = lax.broadcasted_iota(jnp.int32, (BLK, 2 * BLK), 0)
    kj = lax.broadcasted_iota(jnp.int32, (BLK, 2 * BLK), 1)
    diff = qi + BLK - kj
    band = (diff >= 0) & (diff < WINDOW)
    in_cur = kj >= BLK
    ones_lo = jnp.where(lo_half2, 1.0, 0.0).astype(BF16)
    ones_hi = jnp.where(lo_half2, 0.0, 1.0).astype(BF16)
    zeros_bf = jnp.zeros((BLK, LANES), BF16)
    row_lo = lax.broadcasted_iota(jnp.int32, (LANES, LANES), 0) < RET_QK_DIM

    proj_chunks = [dict() for _ in range(N_SUB)]
    out_chunks = [dict() for _ in range(N_SUB)]
    normed = {}
    carry = {}
    att_rhs = {}

    def proj_tile(s, i, col):
        if s == 0:
            return pfirst_ref[i * BLK:(i + 1) * BLK, col:col + LANES]
        chunk, off = divmod(col, PROJ_CHUNK)
        return proj_chunks[s][chunk][i * BLK:(i + 1) * BLK, off:off + LANES]

    def in_proj_chunk(s, c):
        def run():
            if s not in normed:
                xs = x_ref[0, s * SUB_TILE:(s + 1) * SUB_TILE, :]
                ms = jnp.mean(xs * xs, axis=-1, keepdims=True)
                normed[s] = (xs * lax.rsqrt(ms + RMS_EPS) * ng_ref[...]).astype(BF16)
            proj_chunks[s][c] = jnp.dot(normed[s], win_ref[:, c * PROJ_CHUNK:min((c + 1) * PROJ_CHUNK, IN_WIDTH)],
                                        preferred_element_type=F32)
        return run

    def out_proj_chunk(s, c, r0=0, r1=SUB_TILE):
        def run():
            out_chunks[s][(c, r0)] = jnp.dot(mix_refs[s][r0:r1, :], wout_ref[:, c * PROJ_CHUNK:(c + 1) * PROJ_CHUNK],
                                             preferred_element_type=F32)
        return run

    def finish(s, r0=0, r1=SUB_TILE):
        def run():
            rows = slice(s * SUB_TILE + r0, s * SUB_TILE + r1)
            y = x_ref[0, rows, :] + jnp.concatenate([out_chunks[s][(c, r0)] for c in range(OUT_CHUNKS)], axis=1)
            ms = jnp.mean(y * y, axis=-1, keepdims=True)
            o_ref[0, rows, :] = y * lax.rsqrt(ms + RMS_EPS) * fg_ref[...]
        return run

    def out_fillers(s, r0=0, r1=SUB_TILE):
        return [out_proj_chunk(s, c, r0, r1) for c in range(OUT_CHUNKS)] + [finish(s, r0, r1)]

    def kv_prep(s, i):
        if s == 0 and i == 0:
            carry["k"] = [kprev_ref[g] for g in range(ATT_KV_HEADS)]
            carry["v"] = [vprev_ref[g] for g in range(ATT_KV_HEADS)]
        k_cur = proj_tile(s, i, OFF_AK)
        v_cur = proj_tile(s, i, OFF_AV)
        v_sw = pltpu.roll(v_cur, ATT_HEAD_DIM, 1)
        k_t = k_cur.T
        kdup_cur = [jnp.concatenate([k_t[g * ATT_HEAD_DIM:(g + 1) * ATT_HEAD_DIM]] * HEADS_PER_TILE,
                                    axis=0).astype(BF16) for g in range(ATT_KV_HEADS)]
        vdup_cur = [jnp.where(lo_half, v_cur, v_sw).astype(BF16),
                    jnp.where(lo_half, v_sw, v_cur).astype(BF16)]
        per_head = []
        for g in range(ATT_KV_HEADS):
            kdup = jnp.concatenate([carry["k"][g], kdup_cur[g]], axis=1)
            vdup = jnp.concatenate([carry["v"][g], vdup_cur[g]], axis=0)
            zero2 = jnp.zeros_like(vdup)
            rhs_v = jnp.concatenate([
                jnp.concatenate([jnp.where(lo_half2, vdup, zero2), ones_lo], axis=1),
                jnp.concatenate([jnp.where(lo_half2, zero2, vdup), ones_hi], axis=1),
            ], axis=0)
            per_head.append((kdup, rhs_v))
        att_rhs[(s, i)] = per_head
        carry["k"], carry["v"] = kdup_cur, vdup_cur

    stash = {}

    def att_scores(s, i):
        def run():
            kv_prep(s, i)
            valid = band & (in_cur | (j > 0)) if (s == 0 and i == 0) else band
            scores = []
            for p in range(ATT_PAIRS):
                kdup, _ = att_rhs[(s, i)][p // (GROUP // HEADS_PER_TILE)]
                q2 = (proj_tile(s, i, OFF_AQ + p * LANES) * (ATT_HEAD_DIM ** -0.5 * LOG2E)).astype(BF16)
                for e in range(HEADS_PER_TILE):
                    qm = jnp.where(lo_half if e == 0 else hi_half, q2, zeros_bf)
                    sc = jnp.dot(qm, kdup, preferred_element_type=F32)
                    scores.append(jnp.where(valid, sc, NEG_INF))
            stash[("sc", s, i)] = scores
        return run

    def att_softmax(s, i):
        def run():
            probs, sink_terms = [], []
            for head, sc in enumerate(stash.pop(("sc", s, i))):
                sink = sinks_ref[head] * LOG2E
                m = jnp.maximum(jnp.max(sc, axis=-1, keepdims=True), sink)
                probs.append(jnp.exp2(sc - m).astype(BF16))
                sink_terms.append(jnp.exp2(sink - m))
            stash[("p", s, i)] = (probs, sink_terms)
        return run

    def att_values(s, i):
        def run():
            probs, sink_terms = stash.pop(("p", s, i))
            for p in range(ATT_PAIRS):
                _, rhs_v = att_rhs[(s, i)][p // (GROUP // HEADS_PER_TILE)]
                h0 = p * HEADS_PER_TILE
                lhs = jnp.concatenate(probs[h0:h0 + HEADS_PER_TILE], axis=1)
                res = jnp.dot(lhs, rhs_v, preferred_element_type=F32)
                den = res[:, LANES:] + jnp.where(lo_half, sink_terms[h0], sink_terms[h0 + 1])
                a2 = res[:, :LANES] / den
                z2 = proj_tile(s, i, OFF_AZ + p * LANES)
                mix_refs[s][i * BLK:(i + 1) * BLK, p * LANES:(p + 1) * LANES] = (a2 * _silu(z2)).astype(BF16)
        return run

    def ret_scores(s, i):
        def run():
            r0 = (s * BLKS_PER_SUB + i) * BLK
            cos = cos_ref[r0:r0 + BLK, :]
            sin = sin_ref[r0:r0 + BLK, :]

            def rotate(t):
                swapped = jnp.where(even_lane, pltpu.roll(t, LANES - 1, 1), pltpu.roll(t, 1, 1))
                return t * cos + swapped * sin

            per_pair = []
            for p in range(RET_PAIRS):
                q2 = rotate(proj_tile(s, i, OFF_RQ + p * LANES))
                k2 = rotate(proj_tile(s, i, OFF_RK + p * LANES)) * (RET_QK_DIM ** -0.5)
                q2b = q2.astype(BF16)
                qd2b = (q2 * qdec_ref[p]).astype(BF16)
                k2t = k2.T
                k2tb = k2t.astype(BF16)
                kd2tb = (k2t * kdec_ref[p]).astype(BF16)
                rhs_qk = jnp.concatenate([jnp.where(row_lo, k2tb, zeros_bf),
                                          jnp.where(row_lo, zeros_bf, k2tb)], axis=1)
                sc2 = jnp.dot(q2b, rhs_qk, preferred_element_type=F32)
                lhs = []
                for e in range(HEADS_PER_TILE):
                    sel = lo_half if e == 0 else hi_half
                    sc = sc2[:, e * BLK:(e + 1) * BLK] * din_ref[p * HEADS_PER_TILE + e]
                    lhs.append(jnp.concatenate([sc.astype(BF16), jnp.where(sel, qd2b, zeros_bf)], axis=1))
                per_pair.append((lhs, kd2tb))
            stash[("ret", s, i)] = per_pair
        return run

    def ret_outputs(s, i):
        def run():
            per_pair = stash.pop(("ret", s, i))
            for p in range(RET_PAIRS):
                lhs, kd2tb = per_pair[p]
                st = state_ref[p * LANES:(p + 1) * LANES, :]
                st_b = st.astype(BF16)
                v_heads = [proj_tile(s, i, OFF_RV + (p * HEADS_PER_TILE + e) * LANES).astype(BF16)
                           for e in range(HEADS_PER_TILE)]
                for e in range(HEADS_PER_TILE):
                    head = p * HEADS_PER_TILE + e
                    rhs = jnp.concatenate([v_heads[e], st_b], axis=0)
                    o = jnp.dot(lhs[e], rhs, preferred_element_type=F32)
                    mu = jnp.mean(o, axis=-1, keepdims=True)
                    oc = o - mu
                    var = jnp.mean(oc * oc, axis=-1, keepdims=True)
                    on = oc * lax.rsqrt(var + GN_EPS) * gng_ref[:, head * LANES:(head + 1) * LANES]
                    rz = proj_tile(s, i, OFF_RZ + head * LANES)
                    mc = ATT_WIDTH + head * LANES
                    mix_refs[s][i * BLK:(i + 1) * BLK, mc:mc + LANES] = (on * _silu(rz)).astype(BF16)
                kv = jnp.dot(kd2tb, jnp.concatenate(v_heads, axis=1),
                             preferred_element_type=F32)
                kv_sel = jnp.where(row_lo, kv[:, :LANES], kv[:, LANES:])
                state_ref[p * LANES:(p + 1) * LANES, :] = st * cdec_ref[p * LANES:(p + 1) * LANES, :] + kv_sel
        return run

    stages = (att_scores, ret_scores, att_softmax, ret_outputs, att_values)

    def mixer_units(s, blocks=range(BLKS_PER_SUB)):
        units = []
        for stage in stages:
            units += [stage(s, i) for i in blocks]
        return units

    last = N_SUB - 1
    for s in range(last):
        fillers = out_fillers(s - 1) if s >= 1 else []
        fillers += [in_proj_chunk(s + 1, c) for c in range(IN_CHUNKS)]
        _interleave(mixer_units(s), fillers)
    fillers = (out_fillers(last - 1) if last >= 1 else []) + project_first_subtile(xnext_ref)
    per_block = -(-len(fillers) // BLKS_PER_SUB)
    for i in range(BLKS_PER_SUB):
        part = fillers[i * per_block:(i + 1) * per_block]
        if i >= 1:
            part = part + out_fillers(last, (i - 1) * BLK, i * BLK)
        _interleave(mixer_units(last, blocks=[i]), part)
    for f in out_fillers(last, (BLKS_PER_SUB - 1) * BLK, BLKS_PER_SUB * BLK):
        f()

    for g in range(ATT_KV_HEADS):
        kprev_ref[g] = carry["k"][g]
        vprev_ref[g] = carry["v"][g]


def _retention_tables(seq):
    pos = np.arange(seq, dtype=np.float64)
    theta = 1.0 / (ROT_BASE ** np.linspace(0.0, 1.0, RET_QK_DIM // 2))
    ang = pos[:, None] * theta[None, :]
    cos, sin = np.cos(ang), np.sin(ang)
    cos_l = np.tile(np.repeat(cos, 2, axis=1), (1, HEADS_PER_TILE))
    sign = np.tile(np.array([-1.0, 1.0]), RET_QK_DIM // 2)
    sin_l = np.tile(np.repeat(sin, 2, axis=1) * sign[None, :], (1, HEADS_PER_TILE))

    log_gamma = np.log(1.0 - 2.0 ** (-5.0 - np.arange(RET_HEADS, dtype=np.float64)))
    idx = np.arange(BLK, dtype=np.float64)
    rel = idx[:, None] - idx[None, :]
    decay_in = np.where(rel >= 0, np.exp(log_gamma[:, None, None] * np.maximum(rel, 0.0)), 0.0)
    k_dec = np.exp(log_gamma[:, None] * (BLK - 1 - idx)[None, :])
    q_dec = np.exp(log_gamma[:, None] * (idx + 1)[None, :])
    chunk_decay = np.exp(log_gamma * BLK)

    def per_pair(dec):
        t = np.repeat(dec.T[:, :, None], RET_QK_DIM, axis=2)
        t = t.reshape(BLK, RET_PAIRS, LANES)
        return np.transpose(t, (1, 0, 2))

    cdec = np.broadcast_to(np.repeat(chunk_decay, RET_QK_DIM)[:, None],
                           (RET_QK_WIDTH, RET_V_DIM))
    k_dec_t = np.transpose(per_pair(k_dec), (0, 2, 1))
    tables = (cos_l, sin_l, decay_in, per_pair(q_dec), k_dec_t, cdec)
    return tuple(jnp.asarray(np.ascontiguousarray(a, dtype=np.float32)) for a in tables)


def _full(shape):
    return pl.BlockSpec(shape, lambda step: (0,) * len(shape))


@jax.jit
def kernel(x, norm_g, w_in, att_sinks, ret_gn_g, w_out, final_g):
    batch, seq, d = x.shape
    depth = w_in.shape[0]
    assert depth == 1 and d == D_MODEL and seq % SEQ_TILE == 0
    cos_l, sin_l, decay_in, qdec, kdec, cdec = _retention_tables(seq)
    tile = SEQ_TILE
    n_seq = seq // tile
    n_tiles = batch * n_seq

    def tile_of(step):
        return jnp.maximum(step - CAST_STEPS, 0)

    def this_tile(step):
        t = tile_of(step)
        return (t // n_seq, t % n_seq, 0)

    def next_first_subtile(step):
        u = jnp.minimum(tile_of(step) + 1, n_tiles - 1)
        return (u // n_seq, (u % n_seq) * N_SUB, 0)

    def weight_slab(step):
        return (jnp.minimum(step, CAST_STEPS - 1), 0)

    call = pl.pallas_call(
        functools.partial(_layer_kernel, n_seq),
        grid=(CAST_STEPS + n_tiles,),
        in_specs=[
            pl.BlockSpec(memory_space=pltpu.SMEM),
            pl.BlockSpec((1, tile, D_MODEL), this_tile),
            pl.BlockSpec((1, SUB_TILE, D_MODEL), next_first_subtile),
            _full((1, D_MODEL)),
            pl.BlockSpec((WEIGHT_SLAB, IN_WIDTH), weight_slab),
            _full((1, RET_WIDTH)),
            pl.BlockSpec((WEIGHT_SLAB, D_MODEL), weight_slab),
            _full((1, D_MODEL)),
            pl.BlockSpec((tile, LANES), lambda step: (tile_of(step) % n_seq, 0)),
            pl.BlockSpec((tile, LANES), lambda step: (tile_of(step) % n_seq, 0)),
            _full((RET_HEADS, BLK, BLK)),
            _full((RET_PAIRS, BLK, LANES)),
            _full((RET_PAIRS, BLK, LANES)),
            _full((RET_QK_WIDTH, RET_V_DIM)),
        ],
        out_specs=pl.BlockSpec((1, tile, D_MODEL), this_tile),
        out_shape=jax.ShapeDtypeStruct(x.shape, x.dtype),
        scratch_shapes=[
            pltpu.VMEM((RET_QK_WIDTH, RET_V_DIM), F32),
            pltpu.VMEM((ATT_KV_HEADS, BLK, LANES), BF16),
            pltpu.VMEM((ATT_KV_HEADS, BLK, LANES), BF16),
            pltpu.VMEM((SUB_TILE, IN_WIDTH), F32),
            pltpu.VMEM((D_MODEL, IN_WIDTH), BF16),
            pltpu.VMEM((MIX_WIDTH, D_MODEL), BF16),
        ] + [pltpu.VMEM((SUB_TILE, MIX_WIDTH), BF16)] * N_SUB,
        compiler_params=pltpu.CompilerParams(
            dimension_semantics=("arbitrary",),
            vmem_limit_bytes=VMEM_LIMIT_BYTES),
        name="hymba_layer",
    )
    return call(att_sinks[0], x, x, norm_g[0][None, :], w_in[0],
                ret_gn_g[0][None, :], w_out[0], final_g[None, :],
                cos_l, sin_l, decay_in, qdec, kdec, cdec)
```

```python
import functools

import jax
import jax.numpy as jnp
import numpy as np
from jax import lax
from jax.experimental import pallas as pl
from jax.experimental.pallas import tpu as pltpu

D_MODEL = 1024
ATT_HEADS = 8
ATT_KV_HEADS = 2
ATT_HEAD_DIM = 64
WINDOW = 128
BLK = 128
RET_HEADS = 4
RET_QK_DIM = 64
RET_V_DIM = 128
ROT_BASE = 10000.0
RMS_EPS = 1e-6
GN_EPS = 1e-6
NEG_INF = -1e30
LOG2E = 1.4426950408889634

ATT_WIDTH = ATT_HEADS * ATT_HEAD_DIM
ATT_KV_WIDTH = ATT_KV_HEADS * ATT_HEAD_DIM
RET_QK_WIDTH = RET_HEADS * RET_QK_DIM
RET_WIDTH = RET_HEADS * RET_V_DIM
MIX_WIDTH = ATT_WIDTH + RET_WIDTH
IN_WIDTH = 2 * ATT_WIDTH + 2 * ATT_KV_WIDTH + 2 * RET_QK_WIDTH + 2 * RET_WIDTH

OFF_AQ = 0
OFF_AK = OFF_AQ + ATT_WIDTH
OFF_AV = OFF_AK + ATT_KV_WIDTH
OFF_AZ = OFF_AV + ATT_KV_WIDTH
OFF_RQ = OFF_AZ + ATT_WIDTH
OFF_RK = OFF_RQ + RET_QK_WIDTH
OFF_RV = OFF_RK + RET_QK_WIDTH
OFF_RZ = OFF_RV + RET_WIDTH

LANES = 128
MXU_COLS = 256
N_MXU = 2
PROJ_CHUNK = N_MXU * MXU_COLS
HEADS_PER_TILE = LANES // ATT_HEAD_DIM
ATT_PAIRS = ATT_HEADS // HEADS_PER_TILE
RET_PAIRS = RET_HEADS // HEADS_PER_TILE
GROUP = ATT_HEADS // ATT_KV_HEADS

SEQ_TILE = 1024
SUB_TILE = 256
N_SUB = SEQ_TILE // SUB_TILE
BLKS_PER_SUB = SUB_TILE // BLK
IN_CHUNKS = -(-IN_WIDTH // PROJ_CHUNK)
OUT_CHUNKS = D_MODEL // PROJ_CHUNK
WEIGHT_SLAB = 256
CAST_STEPS = D_MODEL // WEIGHT_SLAB
VMEM_LIMIT_BYTES = 56 * 1024 * 1024

F32 = jnp.float32
BF16 = jnp.bfloat16


def _silu(z):
    hz = 0.5 * z
    return hz + hz * jnp.tanh(hz)


def _interleave(units, fillers):
    done = 0
    for k, unit in enumerate(units):
        unit()
        due = -(-(k + 1) * len(fillers) // len(units))
        while done < due:
            fillers[done]()
            done += 1
    for f in fillers[done:]:
        f()


def _layer_kernel(tiles_per_seq, sinks_ref, x_ref, xnext_ref, ng_ref, win_slab_ref, gng_ref, wout_slab_ref, fg_ref,
                  cos_ref, sin_ref, din_ref, qdec_ref, kdec_ref, cdec_ref,
                  o_ref, state_ref, kprev_ref, vprev_ref, pfirst_ref, win_ref, wout_ref, *mix_refs):
    step = pl.program_id(0)

    @pl.when(step < CAST_STEPS)
    def _():
        rows = pl.ds(pl.multiple_of(step * WEIGHT_SLAB, WEIGHT_SLAB), WEIGHT_SLAB)
        win_ref[rows, :] = win_slab_ref[...].astype(BF16)
        wout_ref[rows, :] = wout_slab_ref[...].astype(BF16)

    @pl.when(step >= CAST_STEPS)
    def _():
        _tile_step(step - CAST_STEPS, tiles_per_seq, sinks_ref, x_ref, xnext_ref, ng_ref, win_ref, gng_ref, wout_ref,
                   fg_ref, cos_ref, sin_ref, din_ref, qdec_ref, kdec_ref, cdec_ref,
                   o_ref, state_ref, kprev_ref, vprev_ref, pfirst_ref, mix_refs)


def _tile_step(t, tiles_per_seq, sinks_ref, x_ref, xnext_ref, ng_ref, win_ref, gng_ref, wout_ref, fg_ref,
               cos_ref, sin_ref, din_ref, qdec_ref, kdec_ref, cdec_ref,
               o_ref, state_ref, kprev_ref, vprev_ref, pfirst_ref, mix_refs):
    j = lax.rem(t, tiles_per_seq)

    def project_first_subtile(src_ref):
        held = {}

        def chunk(c):
            def run():
                if not held:
                    xs = src_ref[0]
                    ms = jnp.mean(xs * xs, axis=-1, keepdims=True)
                    held["h"] = (xs * lax.rsqrt(ms + RMS_EPS) * ng_ref[...]).astype(BF16)
                c0, c1 = c * PROJ_CHUNK, min((c + 1) * PROJ_CHUNK, IN_WIDTH)
                pfirst_ref[:, c0:c1] = jnp.dot(held["h"], win_ref[:, c0:c1], preferred_element_type=F32)
            return run
        return [chunk(c) for c in range(IN_CHUNKS)]

    @pl.when(t == 0)
    def _():
        for f in project_first_subtile(x_ref.at[:, 0:SUB_TILE, :]):
            f()

    @pl.when(j == 0)
    def _():
        state_ref[...] = jnp.zeros_like(state_ref)
        kprev_ref[...] = jnp.zeros_like(kprev_ref)
        vprev_ref[...] = jnp.zeros_like(vprev_ref)

    lane = lax.broadcasted_iota(jnp.int32, (BLK, LANES), 1)
    lo_half = lane < ATT_HEAD_DIM
    hi_half = jnp.logical_not(lo_half)
    even_lane = (lane & 1) == 0
    lane2 = lax.broadcasted_iota(jnp.int32, (2 * BLK, LANES), 1)
    lo_half2 = lane2 < ATT_HEAD_DIM
    qi = lax.broadcasted_iota(jnp.int32, (BLK, 2 * BLK), 0)
    kj = lax.broadcasted_iota(jnp.int32, (BLK, 2 * BLK), 1)
    diff = qi + BLK - kj
    band = (diff >= 0) & (diff < WINDOW)
    in_cur = kj >= BLK
    ones_lo = jnp.where(lo_half2, 1.0, 0.0).astype(BF16)
    ones_hi = jnp.where(lo_half2, 0.0, 1.0).astype(BF16)
    zeros_bf = jnp.zeros((BLK, LANES), BF16)
    row_lo = lax.broadcasted_iota(jnp.int32, (LANES, LANES), 0) < RET_QK_DIM

    proj_chunks = [dict() for _ in range(N_SUB)]
    out_chunks = [dict() for _ in range(N_SUB)]
    normed = {}
    carry = {}
    att_rhs = {}

    def proj_tile(s, i, col):
        if s == 0:
            return pfirst_ref[i * BLK:(i + 1) * BLK, col:col + LANES]
        chunk, off = divmod(col, PROJ_CHUNK)
        return proj_chunks[s][chunk][i * BLK:(i + 1) * BLK, off:off + LANES]

    def in_proj_chunk(s, c):
        def run():
            if s not in normed:
                xs = x_ref[0, s * SUB_TILE:(s + 1) * SUB_TILE, :]
                ms = jnp.mean(xs * xs, axis=-1, keepdims=True)
                normed[s] = (xs * lax.rsqrt(ms + RMS_EPS) * ng_ref[...]).astype(BF16)
            proj_chunks[s][c] = jnp.dot(normed[s], win_ref[:, c * PROJ_CHUNK:min((c + 1) * PROJ_CHUNK, IN_WIDTH)],
                                        preferred_element_type=F32)
        return run

    def out_proj_chunk(s, c):
        def run():
            out_chunks[s][c] = jnp.dot(mix_refs[s][...], wout_ref[:, c * PROJ_CHUNK:(c + 1) * PROJ_CHUNK],
                                       preferred_element_type=F32)
        return run

    def finish(s):
        def run():
            rows = slice(s * SUB_TILE, (s + 1) * SUB_TILE)
            y = x_ref[0, rows, :] + jnp.concatenate([out_chunks[s][c] for c in range(OUT_CHUNKS)], axis=1)
            ms = jnp.mean(y * y, axis=-1, keepdims=True)
            o_ref[0, rows, :] = y * lax.rsqrt(ms + RMS_EPS) * fg_ref[...]
        return run

    def kv_prep(s, i):
        if s == 0 and i == 0:
            carry["k"] = [kprev_ref[g] for g in range(ATT_KV_HEADS)]
            carry["v"] = [vprev_ref[g] for g in range(ATT_KV_HEADS)]
        k_cur = proj_tile(s, i, OFF_AK)
        v_cur = proj_tile(s, i, OFF_AV)
        v_sw = pltpu.roll(v_cur, ATT_HEAD_DIM, 1)
        k_t = k_cur.T
        kdup_cur = [jnp.concatenate([k_t[g * ATT_HEAD_DIM:(g + 1) * ATT_HEAD_DIM]] * HEADS_PER_TILE,
                                    axis=0).astype(BF16) for g in range(ATT_KV_HEADS)]
        vdup_cur = [jnp.where(lo_half, v_cur, v_sw).astype(BF16),
                    jnp.where(lo_half, v_sw, v_cur).astype(BF16)]
        per_head = []
        for g in range(ATT_KV_HEADS):
            kdup = jnp.concatenate([carry["k"][g], kdup_cur[g]], axis=1)
            vdup = jnp.concatenate([carry["v"][g], vdup_cur[g]], axis=0)
            zero2 = jnp.zeros_like(vdup)
            rhs_v = jnp.concatenate([
                jnp.concatenate([jnp.where(lo_half2, vdup, zero2), ones_lo], axis=1),
                jnp.concatenate([jnp.where(lo_half2, zero2, vdup), ones_hi], axis=1),
            ], axis=0)
            per_head.append((kdup, rhs_v))
        att_rhs[(s, i)] = per_head
        carry["k"], carry["v"] = kdup_cur, vdup_cur

    stash = {}

    def att_scores(s, i):
        def run():
            kv_prep(s, i)
            valid = band & (in_cur | (j > 0)) if (s == 0 and i == 0) else band
            scores = []
            for p in range(ATT_PAIRS):
                kdup, _ = att_rhs[(s, i)][p // (GROUP // HEADS_PER_TILE)]
                q2 = (proj_tile(s, i, OFF_AQ + p * LANES) * (ATT_HEAD_DIM ** -0.5 * LOG2E)).astype(BF16)
                for e in range(HEADS_PER_TILE):
                    qm = jnp.where(lo_half if e == 0 else hi_half, q2, zeros_bf)
                    sc = jnp.dot(qm, kdup, preferred_element_type=F32)
                    scores.append(jnp.where(valid, sc, NEG_INF))
            stash[("sc", s, i)] = scores
        return run

    def att_softmax(s, i):
        def run():
            probs, sink_terms = [], []
            for head, sc in enumerate(stash.pop(("sc", s, i))):
                sink = sinks_ref[head] * LOG2E
                m = jnp.maximum(jnp.max(sc, axis=-1, keepdims=True), sink)
                probs.append(jnp.exp2(sc - m).astype(BF16))
                sink_terms.append(jnp.exp2(sink - m))
            stash[("p", s, i)] = (probs, sink_terms)
        return run

    def att_values(s, i):
        def run():
            probs, sink_terms = stash.pop(("p", s, i))
            for p in range(ATT_PAIRS):
                _, rhs_v = att_rhs[(s, i)][p // (GROUP // HEADS_PER_TILE)]
                h0 = p * HEADS_PER_TILE
                lhs = jnp.concatenate(probs[h0:h0 + HEADS_PER_TILE], axis=1)
                res = jnp.dot(lhs, rhs_v, preferred_element_type=F32)
                den = res[:, LANES:] + jnp.where(lo_half, sink_terms[h0], sink_terms[h0 + 1])
                a2 = res[:, :LANES] / den
                z2 = proj_tile(s, i, OFF_AZ + p * LANES)
                mix_refs[s][i * BLK:(i + 1) * BLK, p * LANES:(p + 1) * LANES] = (a2 * _silu(z2)).astype(BF16)
        return run

    def ret_scores(s, i):
        def run():
            r0 = (s * BLKS_PER_SUB + i) * BLK
            cos = cos_ref[r0:r0 + BLK, :]
            sin = sin_ref[r0:r0 + BLK, :]

            def rotate(t):
                swapped = jnp.where(even_lane, pltpu.roll(t, LANES - 1, 1), pltpu.roll(t, 1, 1))
                return t * cos + swapped * sin

            per_pair = []
            for p in range(RET_PAIRS):
                q2 = rotate(proj_tile(s, i, OFF_RQ + p * LANES))
                k2 = rotate(proj_tile(s, i, OFF_RK + p * LANES)) * (RET_QK_DIM ** -0.5)
                q2b = q2.astype(BF16)
                qd2b = (q2 * qdec_ref[p]).astype(BF16)
                k2t = k2.T
                k2tb = k2t.astype(BF16)
                kd2tb = (k2t * kdec_ref[p]).astype(BF16)
                rhs_qk = jnp.concatenate([jnp.where(row_lo, k2tb, zeros_bf),
                                          jnp.where(row_lo, zeros_bf, k2tb)], axis=1)
                sc2 = jnp.dot(q2b, rhs_qk, preferred_element_type=F32)
                lhs = []
                for e in range(HEADS_PER_TILE):
                    sel = lo_half if e == 0 else hi_half
                    sc = sc2[:, e * BLK:(e + 1) * BLK] * din_ref[p * HEADS_PER_TILE + e]
                    lhs.append(jnp.concatenate([sc.astype(BF16), jnp.where(sel, qd2b, zeros_bf)], axis=1))
                per_pair.append((lhs, kd2tb))
            stash[("ret", s, i)] = per_pair
        return run

    def ret_outputs(s, i):
        def run():
            per_pair = stash.pop(("ret", s, i))
            for p in range(RET_PAIRS):
                lhs, kd2tb = per_pair[p]
                st = state_ref[p * LANES:(p + 1) * LANES, :]
                st_b = st.astype(BF16)
                v_heads = [proj_tile(s, i, OFF_RV + (p * HEADS_PER_TILE + e) * LANES).astype(BF16)
                           for e in range(HEADS_PER_TILE)]
                for e in range(HEADS_PER_TILE):
                    head = p * HEADS_PER_TILE + e
                    rhs = jnp.concatenate([v_heads[e], st_b], axis=0)
                    o = jnp.dot(lhs[e], rhs, preferred_element_type=F32)
                    mu = jnp.mean(o, axis=-1, keepdims=True)
                    oc = o - mu
                    var = jnp.mean(oc * oc, axis=-1, keepdims=True)
                    on = oc * lax.rsqrt(var + GN_EPS) * gng_ref[:, head * LANES:(head + 1) * LANES]
                    rz = proj_tile(s, i, OFF_RZ + head * LANES)
                    mc = ATT_WIDTH + head * LANES
                    mix_refs[s][i * BLK:(i + 1) * BLK, mc:mc + LANES] = (on * _silu(rz)).astype(BF16)
                kv = jnp.dot(kd2tb, jnp.concatenate(v_heads, axis=1),
                             preferred_element_type=F32)
                kv_sel = jnp.where(row_lo, kv[:, :LANES], kv[:, LANES:])
                state_ref[p * LANES:(p + 1) * LANES, :] = st * cdec_ref[p * LANES:(p + 1) * LANES, :] + kv_sel
        return run

    def mixer_units(s):
        units = []
        for stage in (att_scores, ret_scores, att_softmax, ret_outputs, att_values):
            units += [stage(s, i) for i in range(BLKS_PER_SUB)]
        return units

    for s in range(N_SUB):
        fillers = []
        if s >= 1:
            fillers += [out_proj_chunk(s - 1, c) for c in range(OUT_CHUNKS)] + [finish(s - 1)]
        if s + 1 < N_SUB:
            fillers += [in_proj_chunk(s + 1, c) for c in range(IN_CHUNKS)]
        else:
            fillers += project_first_subtile(xnext_ref)
        if s == 0:
            fillers.pop(0)()
        _interleave(mixer_units(s), fillers)
    for c in range(OUT_CHUNKS):
        out_proj_chunk(N_SUB - 1, c)()
    finish(N_SUB - 1)()

    for g in range(ATT_KV_HEADS):
        kprev_ref[g] = carry["k"][g]
        vprev_ref[g] = carry["v"][g]


def _retention_tables(seq):
    pos = np.arange(seq, dtype=np.float64)
    theta = 1.0 / (ROT_BASE ** np.linspace(0.0, 1.0, RET_QK_DIM // 2))
    ang = pos[:, None] * theta[None, :]
    cos, sin = np.cos(ang), np.sin(ang)
    cos_l = np.tile(np.repeat(cos, 2, axis=1), (1, HEADS_PER_TILE))
    sign = np.tile(np.array([-1.0, 1.0]), RET_QK_DIM // 2)
    sin_l = np.tile(np.repeat(sin, 2, axis=1) * sign[None, :], (1, HEADS_PER_TILE))

    log_gamma = np.log(1.0 - 2.0 ** (-5.0 - np.arange(RET_HEADS, dtype=np.float64)))
    idx = np.arange(BLK, dtype=np.float64)
    rel = idx[:, None] - idx[None, :]
    decay_in = np.where(rel >= 0, np.exp(log_gamma[:, None, None] * np.maximum(rel, 0.0)), 0.0)
    k_dec = np.exp(log_gamma[:, None] * (BLK - 1 - idx)[None, :])
    q_dec = np.exp(log_gamma[:, None] * (idx + 1)[None, :])
    chunk_decay = np.exp(log_gamma * BLK)

    def per_pair(dec):
        t = np.repeat(dec.T[:, :, None], RET_QK_DIM, axis=2)
        t = t.reshape(BLK, RET_PAIRS, LANES)
        return np.transpose(t, (1, 0, 2))

    cdec = np.broadcast_to(np.repeat(chunk_decay, RET_QK_DIM)[:, None],
                           (RET_QK_WIDTH, RET_V_DIM))
    k_dec_t = np.transpose(per_pair(k_dec), (0, 2, 1))
    tables = (cos_l, sin_l, decay_in, per_pair(q_dec), k_dec_t, cdec)
    return tuple(jnp.asarray(np.ascontiguousarray(a, dtype=np.float32)) for a in tables)


def _full(shape):
    return pl.BlockSpec(shape, lambda step: (0,) * len(shape))


@jax.jit
def kernel(x, norm_g, w_in, att_sinks, ret_gn_g, w_out, final_g):
    batch, seq, d = x.shape
    depth = w_in.shape[0]
    assert depth == 1 and d == D_MODEL and seq % SEQ_TILE == 0
    cos_l, sin_l, decay_in, qdec, kdec, cdec = _retention_tables(seq)
    tile = SEQ_TILE
    n_seq = seq // tile
    n_tiles = batch * n_seq

    def tile_of(step):
        return jnp.maximum(step - CAST_STEPS, 0)

    def this_tile(step):
        t = tile_of(step)
        return (t // n_seq, t % n_seq, 0)

    def next_first_subtile(step):
        u = jnp.minimum(tile_of(step) + 1, n_tiles - 1)
        return (u // n_seq, (u % n_seq) * N_SUB, 0)

    def weight_slab(step):
        return (jnp.minimum(step, CAST_STEPS - 1), 0)

    call = pl.pallas_call(
        functools.partial(_layer_kernel, n_seq),
        grid=(CAST_STEPS + n_tiles,),
        in_specs=[
            pl.BlockSpec(memory_space=pltpu.SMEM),
            pl.BlockSpec((1, tile, D_MODEL), this_tile),
            pl.BlockSpec((1, SUB_TILE, D_MODEL), next_first_subtile),
            _full((1, D_MODEL)),
            pl.BlockSpec((WEIGHT_SLAB, IN_WIDTH), weight_slab),
            _full((1, RET_WIDTH)),
            pl.BlockSpec((WEIGHT_SLAB, D_MODEL), weight_slab),
            _full((1, D_MODEL)),
            pl.BlockSpec((tile, LANES), lambda step: (tile_of(step) % n_seq, 0)),
            pl.BlockSpec((tile, LANES), lambda step: (tile_of(step) % n_seq, 0)),
            _full((RET_HEADS, BLK, BLK)),
            _full((RET_PAIRS, BLK, LANES)),
            _full((RET_PAIRS, BLK, LANES)),
            _full((RET_QK_WIDTH, RET_V_DIM)),
        ],
        out_specs=pl.BlockSpec((1, tile, D_MODEL), this_tile),
        out_shape=jax.ShapeDtypeStruct(x.shape, x.dtype),
        scratch_shapes=[
            pltpu.VMEM((RET_QK_WIDTH, RET_V_DIM), F32),
            pltpu.VMEM((ATT_KV_HEADS, BLK, LANES), BF16),
            pltpu.VMEM((ATT_KV_HEADS, BLK, LANES), BF16),
            pltpu.VMEM((SUB_TILE, IN_WIDTH), F32),
            pltpu.VMEM((D_MODEL, IN_WIDTH), BF16),
            pltpu.VMEM((MIX_WIDTH, D_MODEL), BF16),
        ] + [pltpu.VMEM((SUB_TILE, MIX_WIDTH), BF16)] * N_SUB,
        compiler_params=pltpu.CompilerParams(
            dimension_semantics=("arbitrary",),
            vmem_limit_bytes=VMEM_LIMIT_BYTES),
        name="hymba_layer",
    )
    return call(att_sinks[0], x, x, norm_g[0][None, :], w_in[0],
                ret_gn_g[0][None, :], w_out[0], final_g[None, :],
                cos_l, sin_l, decay_in, qdec, kdec, cdec)
```

```python
import functools

import jax
import jax.numpy as jnp
import numpy as np
from jax import lax
from jax.experimental import pallas as pl
from jax.experimental.pallas import tpu as pltpu

D_MODEL = 1024
ATT_HEADS = 8
ATT_KV_HEADS = 2
ATT_HEAD_DIM = 64
WINDOW = 128
BLK = 128
RET_HEADS = 4
RET_QK_DIM = 64
RET_V_DIM = 128
ROT_BASE = 10000.0
RMS_EPS = 1e-6
GN_EPS = 1e-6
NEG_INF = -1e30
LOG2E = 1.4426950408889634

ATT_WIDTH = ATT_HEADS * ATT_HEAD_DIM
ATT_KV_WIDTH = ATT_KV_HEADS * ATT_HEAD_DIM
RET_QK_WIDTH = RET_HEADS * RET_QK_DIM
RET_WIDTH = RET_HEADS * RET_V_DIM
MIX_WIDTH = ATT_WIDTH + RET_WIDTH
IN_WIDTH = 2 * ATT_WIDTH + 2 * ATT_KV_WIDTH + 2 * RET_QK_WIDTH + 2 * RET_WIDTH

OFF_AQ = 0
OFF_AK = OFF_AQ + ATT_WIDTH
OFF_AV = OFF_AK + ATT_KV_WIDTH
OFF_AZ = OFF_AV + ATT_KV_WIDTH
OFF_RQ = OFF_AZ + ATT_WIDTH
OFF_RK = OFF_RQ + RET_QK_WIDTH
OFF_RV = OFF_RK + RET_QK_WIDTH
OFF_RZ = OFF_RV + RET_WIDTH

LANES = 128
MXU_COLS = 256
N_MXU = 2
PROJ_CHUNK = N_MXU * MXU_COLS
HEADS_PER_TILE = LANES // ATT_HEAD_DIM
ATT_PAIRS = ATT_HEADS // HEADS_PER_TILE
RET_PAIRS = RET_HEADS // HEADS_PER_TILE
GROUP = ATT_HEADS // ATT_KV_HEADS

SEQ_TILE = 1024
SUB_TILE = 256
N_SUB = SEQ_TILE // SUB_TILE
BLKS_PER_SUB = SUB_TILE // BLK
IN_CHUNKS = -(-IN_WIDTH // PROJ_CHUNK)
OUT_CHUNKS = D_MODEL // PROJ_CHUNK
WEIGHT_SLAB = 256
CAST_STEPS = D_MODEL // WEIGHT_SLAB
VMEM_LIMIT_BYTES = 56 * 1024 * 1024

F32 = jnp.float32
BF16 = jnp.bfloat16


def _silu(z):
    hz = 0.5 * z
    return hz + hz * jnp.tanh(hz)


def _interleave(units, fillers):
    done = 0
    for k, unit in enumerate(units):
        unit()
        due = -(-(k + 1) * len(fillers) // len(units))
        while done < due:
            fillers[done]()
            done += 1
    for f in fillers[done:]:
        f()


def _layer_kernel(tiles_per_seq, sinks_ref, x_ref, xnext_ref, ng_ref, win_slab_ref, gng_ref, wout_slab_ref, fg_ref,
                  cos_ref, sin_ref, din_ref, qdec_ref, kdec_ref, cdec_ref,
                  o_ref, state_ref, kprev_ref, vprev_ref, pfirst_ref, win_ref, wout_ref, *mix_refs):
    step = pl.program_id(0)

    @pl.when(step < CAST_STEPS)
    def _():
        rows = pl.ds(pl.multiple_of(step * WEIGHT_SLAB, WEIGHT_SLAB), WEIGHT_SLAB)
        win_ref[rows, :] = win_slab_ref[...].astype(BF16)
        wout_ref[rows, :] = wout_slab_ref[...].astype(BF16)

    @pl.when(step >= CAST_STEPS)
    def _():
        _tile_step(step - CAST_STEPS, tiles_per_seq, sinks_ref, x_ref, xnext_ref, ng_ref, win_ref, gng_ref, wout_ref,
                   fg_ref, cos_ref, sin_ref, din_ref, qdec_ref, kdec_ref, cdec_ref,
                   o_ref, state_ref, kprev_ref, vprev_ref, pfirst_ref, mix_refs)


def _tile_step(t, tiles_per_seq, sinks_ref, x_ref, xnext_ref, ng_ref, win_ref, gng_ref, wout_ref, fg_ref,
               cos_ref, sin_ref, din_ref, qdec_ref, kdec_ref, cdec_ref,
               o_ref, state_ref, kprev_ref, vprev_ref, pfirst_ref, mix_refs):
    j = lax.rem(t, tiles_per_seq)

    def project_first_subtile(src_ref):
        held = {}

        def chunk(c):
            def run():
                if not held:
                    xs = src_ref[0]
                    ms = jnp.mean(xs * xs, axis=-1, keepdims=True)
                    held["h"] = (xs * lax.rsqrt(ms + RMS_EPS) * ng_ref[...]).astype(BF16)
                c0, c1 = c * PROJ_CHUNK, min((c + 1) * PROJ_CHUNK, IN_WIDTH)
                pfirst_ref[:, c0:c1] = jnp.dot(held["h"], win_ref[:, c0:c1], preferred_element_type=F32)
            return run
        return [chunk(c) for c in range(IN_CHUNKS)]

    @pl.when(t == 0)
    def _():
        for f in project_first_subtile(x_ref.at[:, 0:SUB_TILE, :]):
            f()

    @pl.when(j == 0)
    def _():
        state_ref[...] = jnp.zeros_like(state_ref)
        kprev_ref[...] = jnp.zeros_like(kprev_ref)
        vprev_ref[...] = jnp.zeros_like(vprev_ref)

    lane = lax.broadcasted_iota(jnp.int32, (BLK, LANES), 1)
    lo_half = lane < ATT_HEAD_DIM
    hi_half = jnp.logical_not(lo_half)
    even_lane = (lane & 1) == 0
    lane2 = lax.broadcasted_iota(jnp.int32, (2 * BLK, LANES), 1)
    lo_half2 = lane2 < ATT_HEAD_DIM
    qi = lax.broadcasted_iota(jnp.int32, (BLK, 2 * BLK), 0)
    kj = lax.broadcasted_iota(jnp.int32, (BLK, 2 * BLK), 1)
    diff = qi + BLK - kj
    band = (diff >= 0) & (diff < WINDOW)
    in_cur = kj >= BLK
    ones_lo = jnp.where(lo_half2, 1.0, 0.0).astype(BF16)
    ones_hi = jnp.where(lo_half2, 0.0, 1.0).astype(BF16)
    zeros_bf = jnp.zeros((BLK, LANES), BF16)
    row_lo = lax.broadcasted_iota(jnp.int32, (LANES, LANES), 0) < RET_QK_DIM

    proj_chunks = [dict() for _ in range(N_SUB)]
    out_chunks = [dict() for _ in range(N_SUB)]
    normed = {}
    carry = {}
    att_rhs = {}

    def proj_tile(s, i, col):
        if s == 0:
            return pfirst_ref[i * BLK:(i + 1) * BLK, col:col + LANES]
        chunk, off = divmod(col, PROJ_CHUNK)
        return proj_chunks[s][chunk][i * BLK:(i + 1) * BLK, off:off + LANES]

    def in_proj_chunk(s, c):
        def run():
            if s not in normed:
                xs = x_ref[0, s * SUB_TILE:(s + 1) * SUB_TILE, :]
                ms = jnp.mean(xs * xs, axis=-1, keepdims=True)
                normed[s] = (xs * lax.rsqrt(ms + RMS_EPS) * ng_ref[...]).astype(BF16)
            proj_chunks[s][c] = jnp.dot(normed[s], win_ref[:, c * PROJ_CHUNK:min((c + 1) * PROJ_CHUNK, IN_WIDTH)],
                                        preferred_element_type=F32)
        return run

    def out_proj_chunk(s, c):
        def run():
            out_chunks[s][c] = jnp.dot(mix_refs[s][...], wout_ref[:, c * PROJ_CHUNK:(c + 1) * PROJ_CHUNK],
                                       preferred_element_type=F32)
        return run

    def finish(s):
        def run():
            rows = slice(s * SUB_TILE, (s + 1) * SUB_TILE)
            y = x_ref[0, rows, :] + jnp.concatenate([out_chunks[s][c] for c in range(OUT_CHUNKS)], axis=1)
            ms = jnp.mean(y * y, axis=-1, keepdims=True)
            o_ref[0, rows, :] = y * lax.rsqrt(ms + RMS_EPS) * fg_ref[...]
        return run

    def kv_prep(s, i):
        if s == 0 and i == 0:
            carry["k"] = [kprev_ref[g] for g in range(ATT_KV_HEADS)]
            carry["v"] = [vprev_ref[g] for g in range(ATT_KV_HEADS)]
        k_cur = proj_tile(s, i, OFF_AK)
        v_cur = proj_tile(s, i, OFF_AV)
        v_sw = pltpu.roll(v_cur, ATT_HEAD_DIM, 1)
        k_t = k_cur.T
        kdup_cur = [jnp.concatenate([k_t[g * ATT_HEAD_DIM:(g + 1) * ATT_HEAD_DIM]] * HEADS_PER_TILE,
                                    axis=0).astype(BF16) for g in range(ATT_KV_HEADS)]
        vdup_cur = [jnp.where(lo_half, v_cur, v_sw).astype(BF16),
                    jnp.where(lo_half, v_sw, v_cur).astype(BF16)]
        per_head = []
        for g in range(ATT_KV_HEADS):
            kdup = jnp.concatenate([carry["k"][g], kdup_cur[g]], axis=1)
            vdup = jnp.concatenate([carry["v"][g], vdup_cur[g]], axis=0)
            zero2 = jnp.zeros_like(vdup)
            rhs_v = jnp.concatenate([
                jnp.concatenate([jnp.where(lo_half2, vdup, zero2), ones_lo], axis=1),
                jnp.concatenate([jnp.where(lo_half2, zero2, vdup), ones_hi], axis=1),
            ], axis=0)
            per_head.append((kdup, rhs_v))
        att_rhs[(s, i)] = per_head
        carry["k"], carry["v"] = kdup_cur, vdup_cur

    stash = {}

    def att_scores(s, i):
        def run():
            kv_prep(s, i)
            valid = band & (in_cur | (j > 0)) if (s == 0 and i == 0) else band
            scores = []
            for p in range(ATT_PAIRS):
                kdup, _ = att_rhs[(s, i)][p // (GROUP // HEADS_PER_TILE)]
                q2 = (proj_tile(s, i, OFF_AQ + p * LANES) * (ATT_HEAD_DIM ** -0.5 * LOG2E)).astype(BF16)
                for e in range(HEADS_PER_TILE):
                    qm = jnp.where(lo_half if e == 0 else hi_half, q2, zeros_bf)
                    sc = jnp.dot(qm, kdup, preferred_element_type=F32)
                    scores.append(jnp.where(valid, sc, NEG_INF))
            stash[("sc", s, i)] = scores
        return run

    def att_softmax(s, i):
        def run():
            probs, sink_terms = [], []
            for head, sc in enumerate(stash.pop(("sc", s, i))):
                sink = sinks_ref[head] * LOG2E
                m = jnp.maximum(jnp.max(sc, axis=-1, keepdims=True), sink)
                probs.append(jnp.exp2(sc - m).astype(BF16))
                sink_terms.append(jnp.exp2(sink - m))
            stash[("p", s, i)] = (probs, sink_terms)
        return run

    def att_values(s, i):
        def run():
            probs, sink_terms = stash.pop(("p", s, i))
            for p in range(ATT_PAIRS):
                _, rhs_v = att_rhs[(s, i)][p // (GROUP // HEADS_PER_TILE)]
                h0 = p * HEADS_PER_TILE
                lhs = jnp.concatenate(probs[h0:h0 + HEADS_PER_TILE], axis=1)
                res = jnp.dot(lhs, rhs_v, preferred_element_type=F32)
                den = res[:, LANES:] + jnp.where(lo_half, sink_terms[h0], sink_terms[h0 + 1])
                a2 = res[:, :LANES] / den
                z2 = proj_tile(s, i, OFF_AZ + p * LANES)
                mix_refs[s][i * BLK:(i + 1) * BLK, p * LANES:(p + 1) * LANES] = (a2 * _silu(z2)).astype(BF16)
        return run

    def ret_scores(s, i):
        def run():
            r0 = (s * BLKS_PER_SUB + i) * BLK
            cos = cos_ref[r0:r0 + BLK, :]
            sin = sin_ref[r0:r0 + BLK, :]

            def rotate(t):
                swapped = jnp.where(even_lane, pltpu.roll(t, LANES - 1, 1), pltpu.roll(t, 1, 1))
                return t * cos + swapped * sin

            per_pair = []
            for p in range(RET_PAIRS):
                q2 = rotate(proj_tile(s, i, OFF_RQ + p * LANES))
                k2 = rotate(proj_tile(s, i, OFF_RK + p * LANES)) * (RET_QK_DIM ** -0.5)
                q2b = q2.astype(BF16)
                qd2b = (q2 * qdec_ref[p]).astype(BF16)
                k2t = k2.T
                k2tb = k2t.astype(BF16)
                kd2tb = (k2t * kdec_ref[p]).astype(BF16)
                rhs_qk = jnp.concatenate([jnp.where(row_lo, k2tb, zeros_bf),
                                          jnp.where(row_lo, zeros_bf, k2tb)], axis=1)
                sc2 = jnp.dot(q2b, rhs_qk, preferred_element_type=F32)
                lhs = []
                for e in range(HEADS_PER_TILE):
                    sel = lo_half if e == 0 else hi_half
                    sc = sc2[:, e * BLK:(e + 1) * BLK] * din_ref[p * HEADS_PER_TILE + e]
                    lhs.append(jnp.concatenate([sc.astype(BF16), jnp.where(sel, qd2b, zeros_bf)], axis=1))
                per_pair.append((lhs, kd2tb))
            stash[("ret", s, i)] = per_pair
        return run

    def ret_outputs(s, i):
        def run():
            per_pair = stash.pop(("ret", s, i))
            for p in range(RET_PAIRS):
                lhs, kd2tb = per_pair[p]
                st = state_ref[p * LANES:(p + 1) * LANES, :]
                st_b = st.astype(BF16)
                v_heads = [proj_tile(s, i, OFF_RV + (p * HEADS_PER_TILE + e) * LANES).astype(BF16)
                           for e in range(HEADS_PER_TILE)]
                for e in range(HEADS_PER_TILE):
                    head = p * HEADS_PER_TILE + e
                    rhs = jnp.concatenate([v_heads[e], st_b], axis=0)
                    o = jnp.dot(lhs[e], rhs, preferred_element_type=F32)
                    mu = jnp.mean(o, axis=-1, keepdims=True)
                    oc = o - mu
                    var = jnp.mean(oc * oc, axis=-1, keepdims=True)
                    on = oc * lax.rsqrt(var + GN_EPS) * gng_ref[:, head * LANES:(head + 1) * LANES]
                    rz = proj_tile(s, i, OFF_RZ + head * LANES)
                    mc = ATT_WIDTH + head * LANES
                    mix_refs[s][i * BLK:(i + 1) * BLK, mc:mc + LANES] = (on * _silu(rz)).astype(BF16)
                kv = jnp.dot(kd2tb, jnp.concatenate(v_heads, axis=1),
                             preferred_element_type=F32)
                kv_sel = jnp.where(row_lo, kv[:, :LANES], kv[:, LANES:])
                state_ref[p * LANES:(p + 1) * LANES, :] = st * cdec_ref[p * LANES:(p + 1) * LANES, :] + kv_sel
        return run

    def mixer_units(s):
        units = []
        for stage in (att_scores, ret_scores, att_softmax, ret_outputs, att_values):
            units += [stage(s, i) for i in range(BLKS_PER_SUB)]
        return units

    for s in range(N_SUB):
        fillers = []
        if s >= 1:
            fillers += [out_proj_chunk(s - 1, c) for c in range(OUT_CHUNKS)] + [finish(s - 1)]
        if s + 1 < N_SUB:
            fillers += [in_proj_chunk(s + 1, c) for c in range(IN_CHUNKS)]
        else:
            fillers += project_first_subtile(xnext_ref)
        if s == 0:
            fillers.pop(0)()
            fillers.pop(0)()
        _interleave(mixer_units(s), fillers)
    for c in range(OUT_CHUNKS):
        out_proj_chunk(N_SUB - 1, c)()
    finish(N_SUB - 1)()

    for g in range(ATT_KV_HEADS):
        kprev_ref[g] = carry["k"][g]
        vprev_ref[g] = carry["v"][g]


def _retention_tables(seq):
    pos = np.arange(seq, dtype=np.float64)
    theta = 1.0 / (ROT_BASE ** np.linspace(0.0, 1.0, RET_QK_DIM // 2))
    ang = pos[:, None] * theta[None, :]
    cos, sin = np.cos(ang), np.sin(ang)
    cos_l = np.tile(np.repeat(cos, 2, axis=1), (1, HEADS_PER_TILE))
    sign = np.tile(np.array([-1.0, 1.0]), RET_QK_DIM // 2)
    sin_l = np.tile(np.repeat(sin, 2, axis=1) * sign[None, :], (1, HEADS_PER_TILE))

    log_gamma = np.log(1.0 - 2.0 ** (-5.0 - np.arange(RET_HEADS, dtype=np.float64)))
    idx = np.arange(BLK, dtype=np.float64)
    rel = idx[:, None] - idx[None, :]
    decay_in = np.where(rel >= 0, np.exp(log_gamma[:, None, None] * np.maximum(rel, 0.0)), 0.0)
    k_dec = np.exp(log_gamma[:, None] * (BLK - 1 - idx)[None, :])
    q_dec = np.exp(log_gamma[:, None] * (idx + 1)[None, :])
    chunk_decay = np.exp(log_gamma * BLK)

    def per_pair(dec):
        t = np.repeat(dec.T[:, :, None], RET_QK_DIM, axis=2)
        t = t.reshape(BLK, RET_PAIRS, LANES)
        return np.transpose(t, (1, 0, 2))

    cdec = np.broadcast_to(np.repeat(chunk_decay, RET_QK_DIM)[:, None],
                           (RET_QK_WIDTH, RET_V_DIM))
    k_dec_t = np.transpose(per_pair(k_dec), (0, 2, 1))
    tables = (cos_l, sin_l, decay_in, per_pair(q_dec), k_dec_t, cdec)
    return tuple(jnp.asarray(np.ascontiguousarray(a, dtype=np.float32)) for a in tables)


def _full(shape):
    return pl.BlockSpec(shape, lambda step: (0,) * len(shape))


@jax.jit
def kernel(x, norm_g, w_in, att_sinks, ret_gn_g, w_out, final_g):
    batch, seq, d = x.shape
    depth = w_in.shape[0]
    assert depth == 1 and d == D_MODEL and seq % SEQ_TILE == 0
    cos_l, sin_l, decay_in, qdec, kdec, cdec = _retention_tables(seq)
    tile = SEQ_TILE
    n_seq = seq // tile
    n_tiles = batch * n_seq

    def tile_of(step):
        return jnp.maximum(step - CAST_STEPS, 0)

    def this_tile(step):
        t = tile_of(step)
        return (t // n_seq, t % n_seq, 0)

    def next_first_subtile(step):
        u = jnp.minimum(tile_of(step) + 1, n_tiles - 1)
        return (u // n_seq, (u % n_seq) * N_SUB, 0)

    def weight_slab(step):
        return (jnp.minimum(step, CAST_STEPS - 1), 0)

    call = pl.pallas_call(
        functools.partial(_layer_kernel, n_seq),
        grid=(CAST_STEPS + n_tiles,),
        in_specs=[
            pl.BlockSpec(memory_space=pltpu.SMEM),
            pl.BlockSpec((1, tile, D_MODEL), this_tile),
            pl.BlockSpec((1, SUB_TILE, D_MODEL), next_first_subtile),
            _full((1, D_MODEL)),
            pl.BlockSpec((WEIGHT_SLAB, IN_WIDTH), weight_slab),
            _full((1, RET_WIDTH)),
            pl.BlockSpec((WEIGHT_SLAB, D_MODEL), weight_slab),
            _full((1, D_MODEL)),
            pl.BlockSpec((tile, LANES), lambda step: (tile_of(step) % n_seq, 0)),
            pl.BlockSpec((tile, LANES), lambda step: (tile_of(step) % n_seq, 0)),
            _full((RET_HEADS, BLK, BLK)),
            _full((RET_PAIRS, BLK, LANES)),
            _full((RET_PAIRS, BLK, LANES)),
            _full((RET_QK_WIDTH, RET_V_DIM)),
        ],
        out_specs=pl.BlockSpec((1, tile, D_MODEL), this_tile),
        out_shape=jax.ShapeDtypeStruct(x.shape, x.dtype),
        scratch_shapes=[
            pltpu.VMEM((RET_QK_WIDTH, RET_V_DIM), F32),
            pltpu.VMEM((ATT_KV_HEADS, BLK, LANES), BF16),
            pltpu.VMEM((ATT_KV_HEADS, BLK, LANES), BF16),
            pltpu.VMEM((SUB_TILE, IN_WIDTH), F32),
            pltpu.VMEM((D_MODEL, IN_WIDTH), BF16),
            pltpu.VMEM((MIX_WIDTH, D_MODEL), BF16),
        ] + [pltpu.VMEM((SUB_TILE, MIX_WIDTH), BF16)] * N_SUB,
        compiler_params=pltpu.CompilerParams(
            dimension_semantics=("arbitrary",),
            vmem_limit_bytes=VMEM_LIMIT_BYTES),
        name="hymba_layer",
    )
    return call(att_sinks[0], x, x, norm_g[0][None, :], w_in[0],
                ret_gn_g[0][None, :], w_out[0], final_g[None, :],
                cos_l, sin_l, decay_in, qdec, kdec, cdec)
```

```python
import functools

import jax
import jax.numpy as jnp
import numpy as np
from jax import lax
from jax.experimental import pallas as pl
from jax.experimental.pallas import tpu as pltpu

D_MODEL = 1024
ATT_HEADS = 8
ATT_KV_HEADS = 2
ATT_HEAD_DIM = 64
WINDOW = 128
BLK = 128
RET_HEADS = 4
RET_QK_DIM = 64
RET_V_DIM = 128
ROT_BASE = 10000.0
RMS_EPS = 1e-6
GN_EPS = 1e-6
NEG_INF = -1e30
LOG2E = 1.4426950408889634

ATT_WIDTH = ATT_HEADS * ATT_HEAD_DIM
ATT_KV_WIDTH = ATT_KV_HEADS * ATT_HEAD_DIM
RET_QK_WIDTH = RET_HEADS * RET_QK_DIM
RET_WIDTH = RET_HEADS * RET_V_DIM
MIX_WIDTH = ATT_WIDTH + RET_WIDTH
IN_WIDTH = 2 * ATT_WIDTH + 2 * ATT_KV_WIDTH + 2 * RET_QK_WIDTH + 2 * RET_WIDTH

OFF_AQ = 0
OFF_AK = OFF_AQ + ATT_WIDTH
OFF_AV = OFF_AK + ATT_KV_WIDTH
OFF_AZ = OFF_AV + ATT_KV_WIDTH
OFF_RQ = OFF_AZ + ATT_WIDTH
OFF_RK = OFF_RQ + RET_QK_WIDTH
OFF_RV = OFF_RK + RET_QK_WIDTH
OFF_RZ = OFF_RV + RET_WIDTH

LANES = 128
MXU_COLS = 256
N_MXU = 2
PROJ_CHUNK = N_MXU * MXU_COLS
HEADS_PER_TILE = LANES // ATT_HEAD_DIM
ATT_PAIRS = ATT_HEADS // HEADS_PER_TILE
RET_PAIRS = RET_HEADS // HEADS_PER_TILE
GROUP = ATT_HEADS // ATT_KV_HEADS

SEQ_TILE = 1024
SUB_TILE = 256
N_SUB = SEQ_TILE // SUB_TILE
BLKS_PER_SUB = SUB_TILE // BLK
IN_CHUNKS = -(-IN_WIDTH // PROJ_CHUNK)
OUT_CHUNKS = D_MODEL // PROJ_CHUNK
WEIGHT_SLAB = 256
CAST_STEPS = D_MODEL // WEIGHT_SLAB
VMEM_LIMIT_BYTES = 56 * 1024 * 1024

F32 = jnp.float32
BF16 = jnp.bfloat16


def _silu(z):
    hz = 0.5 * z
    return hz + hz * jnp.tanh(hz)


def _interleave(units, fillers):
    done = 0
    for k, unit in enumerate(units):
        unit()
        due = -(-(k + 1) * len(fillers) // len(units))
        while done < due:
            fillers[done]()
            done += 1
    for f in fillers[done:]:
        f()


def _layer_kernel(tiles_per_seq, sinks_ref, x_ref, xnext_ref, ng_ref, win_slab_ref, gng_ref, wout_slab_ref, fg_ref,
                  cos_ref, sin_ref, din_ref, qdec_ref, kdec_ref, cdec_ref,
                  o_ref, state_ref, kprev_ref, vprev_ref, pfirst_ref, win_ref, wout_ref, *mix_refs):
    step = pl.program_id(0)

    @pl.when(step < CAST_STEPS)
    def _():
        rows = pl.ds(pl.multiple_of(step * WEIGHT_SLAB, WEIGHT_SLAB), WEIGHT_SLAB)
        win_ref[rows, :] = win_slab_ref[...].astype(BF16)
        wout_ref[rows, :] = wout_slab_ref[...].astype(BF16)

    @pl.when(step >= CAST_STEPS)
    def _():
        _tile_step(step - CAST_STEPS, tiles_per_seq, sinks_ref, x_ref, xnext_ref, ng_ref, win_ref, gng_ref, wout_ref,
                   fg_ref, cos_ref, sin_ref, din_ref, qdec_ref, kdec_ref, cdec_ref,
                   o_ref, state_ref, kprev_ref, vprev_ref, pfirst_ref, mix_refs)


def _tile_step(t, tiles_per_seq, sinks_ref, x_ref, xnext_ref, ng_ref, win_ref, gng_ref, wout_ref, fg_ref,
               cos_ref, sin_ref, din_ref, qdec_ref, kdec_ref, cdec_ref,
               o_ref, state_ref, kprev_ref, vprev_ref, pfirst_ref, mix_refs):
    j = lax.rem(t, tiles_per_seq)

    def project_first_subtile(src_ref):
        held = {}

        def chunk(c):
            def run():
                if not held:
                    xs = src_ref[0]
                    ms = jnp.mean(xs * xs, axis=-1, keepdims=True)
                    held["h"] = (xs * lax.rsqrt(ms + RMS_EPS) * ng_ref[...]).astype(BF16)
                c0, c1 = c * PROJ_CHUNK, min((c + 1) * PROJ_CHUNK, IN_WIDTH)
                pfirst_ref[:, c0:c1] = jnp.dot(held["h"], win_ref[:, c0:c1], preferred_element_type=F32)
            return run
        return [chunk(c) for c in range(IN_CHUNKS)]

    @pl.when(t == 0)
    def _():
        for f in project_first_subtile(x_ref.at[:, 0:SUB_TILE, :]):
            f()

    @pl.when(j == 0)
    def _():
        state_ref[...] = jnp.zeros_like(state_ref)
        kprev_ref[...] = jnp.zeros_like(kprev_ref)
        vprev_ref[...] = jnp.zeros_like(vprev_ref)

    lane = lax.broadcasted_iota(jnp.int32, (BLK, LANES), 1)
    lo_half = lane < ATT_HEAD_DIM
    hi_half = jnp.logical_not(lo_half)
    even_lane = (lane & 1) == 0
    lane2 = lax.broadcasted_iota(jnp.int32, (2 * BLK, LANES), 1)
    lo_half2 = lane2 < ATT_HEAD_DIM
    qi = lax.broadcasted_iota(jnp.int32, (BLK, 2 * BLK), 0)
    kj = lax.broadcasted_iota(jnp.int32, (BLK, 2 * BLK), 1)
    diff = qi + BLK - kj
    band = (diff >= 0) & (diff < WINDOW)
    in_cur = kj >= BLK
    ones_lo = jnp.where(lo_half2, 1.0, 0.0).astype(BF16)
    ones_hi = jnp.where(lo_half2, 0.0, 1.0).astype(BF16)
    zeros_bf = jnp.zeros((BLK, LANES), BF16)
    row_lo = lax.broadcasted_iota(jnp.int32, (LANES, LANES), 0) < RET_QK_DIM

    proj_chunks = [dict() for _ in range(N_SUB)]
    out_chunks = [dict() for _ in range(N_SUB)]
    normed = {}
    carry = {}
    att_rhs = {}

    def proj_tile(s, i, col):
        if s == 0:
            return pfirst_ref[i * BLK:(i + 1) * BLK, col:col + LANES]
        chunk, off = divmod(col, PROJ_CHUNK)
        return proj_chunks[s][chunk][i * BLK:(i + 1) * BLK, off:off + LANES]

    def in_proj_chunk(s, c):
        def run():
            if s not in normed:
                xs = x_ref[0, s * SUB_TILE:(s + 1) * SUB_TILE, :]
                ms = jnp.mean(xs * xs, axis=-1, keepdims=True)
                normed[s] = (xs * lax.rsqrt(ms + RMS_EPS) * ng_ref[...]).astype(BF16)
            proj_chunks[s][c] = jnp.dot(normed[s], win_ref[:, c * PROJ_CHUNK:min((c + 1) * PROJ_CHUNK, IN_WIDTH)],
                                        preferred_element_type=F32)
        return run

    def out_proj_chunk(s, c):
        def run():
            out_chunks[s][c] = jnp.dot(mix_refs[s][...], wout_ref[:, c * PROJ_CHUNK:(c + 1) * PROJ_CHUNK],
                                       preferred_element_type=F32)
        return run

    def finish(s):
        def run():
            rows = slice(s * SUB_TILE, (s + 1) * SUB_TILE)
            y = x_ref[0, rows, :] + jnp.concatenate([out_chunks[s][c] for c in range(OUT_CHUNKS)], axis=1)
            ms = jnp.mean(y * y, axis=-1, keepdims=True)
            o_ref[0, rows, :] = y * lax.rsqrt(ms + RMS_EPS) * fg_ref[...]
        return run

    def kv_prep(s, i):
        if s == 0 and i == 0:
            carry["k"] = [kprev_ref[g] for g in range(ATT_KV_HEADS)]
            carry["v"] = [vprev_ref[g] for g in range(ATT_KV_HEADS)]
        k_cur = proj_tile(s, i, OFF_AK)
        v_cur = proj_tile(s, i, OFF_AV)
        v_sw = pltpu.roll(v_cur, ATT_HEAD_DIM, 1)
        k_t = k_cur.T
        kdup_cur = [jnp.concatenate([k_t[g * ATT_HEAD_DIM:(g + 1) * ATT_HEAD_DIM]] * HEADS_PER_TILE,
                                    axis=0).astype(BF16) for g in range(ATT_KV_HEADS)]
        vdup_cur = [jnp.where(lo_half, v_cur, v_sw).astype(BF16),
                    jnp.where(lo_half, v_sw, v_cur).astype(BF16)]
        per_head = []
        for g in range(ATT_KV_HEADS):
            kdup = jnp.concatenate([carry["k"][g], kdup_cur[g]], axis=1)
            vdup = jnp.concatenate([carry["v"][g], vdup_cur[g]], axis=0)
            zero2 = jnp.zeros_like(vdup)
            rhs_v = jnp.concatenate([
                jnp.concatenate([jnp.where(lo_half2, vdup, zero2), ones_lo], axis=1),
                jnp.concatenate([jnp.where(lo_half2, zero2, vdup), ones_hi], axis=1),
            ], axis=0)
            per_head.append((kdup, rhs_v))
        att_rhs[(s, i)] = per_head
        carry["k"], carry["v"] = kdup_cur, vdup_cur

    stash = {}

    def att_scores(s, i):
        def run():
            kv_prep(s, i)
            valid = band & (in_cur | (j > 0)) if (s == 0 and i == 0) else band
            scores = []
            for p in range(ATT_PAIRS):
                kdup, _ = att_rhs[(s, i)][p // (GROUP // HEADS_PER_TILE)]
                q2 = (proj_tile(s, i, OFF_AQ + p * LANES) * (ATT_HEAD_DIM ** -0.5 * LOG2E)).astype(BF16)
                for e in range(HEADS_PER_TILE):
                    qm = jnp.where(lo_half if e == 0 else hi_half, q2, zeros_bf)
                    sc = jnp.dot(qm, kdup, preferred_element_type=F32)
                    scores.append(jnp.where(valid, sc, NEG_INF))
            stash[("sc", s, i)] = scores
        return run

    def att_softmax(s, i):
        def run():
            probs, sink_terms = [], []
            for head, sc in enumerate(stash.pop(("sc", s, i))):
                sink = sinks_ref[head] * LOG2E
                m = jnp.maximum(jnp.max(sc, axis=-1, keepdims=True), sink)
                probs.append(jnp.exp2(sc - m).astype(BF16))
                sink_terms.append(jnp.exp2(sink - m))
            stash[("p", s, i)] = (probs, sink_terms)
        return run

    def att_values(s, i):
        def run():
            probs, sink_terms = stash.pop(("p", s, i))
            for p in range(ATT_PAIRS):
                _, rhs_v = att_rhs[(s, i)][p // (GROUP // HEADS_PER_TILE)]
                h0 = p * HEADS_PER_TILE
                lhs = jnp.concatenate(probs[h0:h0 + HEADS_PER_TILE], axis=1)
                res = jnp.dot(lhs, rhs_v, preferred_element_type=F32)
                den = res[:, LANES:] + jnp.where(lo_half, sink_terms[h0], sink_terms[h0 + 1])
                a2 = res[:, :LANES] / den
                z2 = proj_tile(s, i, OFF_AZ + p * LANES)
                mix_refs[s][i * BLK:(i + 1) * BLK, p * LANES:(p + 1) * LANES] = (a2 * _silu(z2)).astype(BF16)
        return run

    def ret_scores(s, i):
        def run():
            r0 = (s * BLKS_PER_SUB + i) * BLK
            cos = cos_ref[r0:r0 + BLK, :]
            sin = sin_ref[r0:r0 + BLK, :]

            def rotate(t):
                swapped = jnp.where(even_lane, pltpu.roll(t, LANES - 1, 1), pltpu.roll(t, 1, 1))
                return t * cos + swapped * sin

            per_pair = []
            for p in range(RET_PAIRS):
                q2 = rotate(proj_tile(s, i, OFF_RQ + p * LANES))
                k2 = rotate(proj_tile(s, i, OFF_RK + p * LANES)) * (RET_QK_DIM ** -0.5)
                q2b = q2.astype(BF16)
                qd2b = (q2 * qdec_ref[p]).astype(BF16)
                k2t = k2.T
                k2tb = k2t.astype(BF16)
                kd2tb = (k2t * kdec_ref[p]).astype(BF16)
                rhs_qk = jnp.concatenate([jnp.where(row_lo, k2tb, zeros_bf),
                                          jnp.where(row_lo, zeros_bf, k2tb)], axis=1)
                sc2 = jnp.dot(q2b, rhs_qk, preferred_element_type=F32)
                lhs = []
                for e in range(HEADS_PER_TILE):
                    sel = lo_half if e == 0 else hi_half
                    sc = sc2[:, e * BLK:(e + 1) * BLK] * din_ref[p * HEADS_PER_TILE + e]
                    lhs.append(jnp.concatenate([sc.astype(BF16), jnp.where(sel, qd2b, zeros_bf)], axis=1))
                per_pair.append((lhs, kd2tb))
            stash[("ret", s, i)] = per_pair
        return run

    def ret_outputs(s, i):
        def run():
            per_pair = stash.pop(("ret", s, i))
            for p in range(RET_PAIRS):
                lhs, kd2tb = per_pair[p]
                st = state_ref[p * LANES:(p + 1) * LANES, :]
                st_b = st.astype(BF16)
                v_heads = [proj_tile(s, i, OFF_RV + (p * HEADS_PER_TILE + e) * LANES).astype(BF16)
                           for e in range(HEADS_PER_TILE)]
                for e in range(HEADS_PER_TILE):
                    head = p * HEADS_PER_TILE + e
                    rhs = jnp.concatenate([v_heads[e], st_b], axis=0)
                    o = jnp.dot(lhs[e], rhs, preferred_element_type=F32)
                    mu = jnp.mean(o, axis=-1, keepdims=True)
                    oc = o - mu
                    var = jnp.mean(oc * oc, axis=-1, keepdims=True)
                    on = oc * lax.rsqrt(var + GN_EPS) * gng_ref[:, head * LANES:(head + 1) * LANES]
                    rz = proj_tile(s, i, OFF_RZ + head * LANES)
                    mc = ATT_WIDTH + head * LANES
                    mix_refs[s][i * BLK:(i + 1) * BLK, mc:mc + LANES] = (on * _silu(rz)).astype(BF16)
                kv = jnp.dot(kd2tb, jnp.concatenate(v_heads, axis=1),
                             preferred_element_type=F32)
                kv_sel = jnp.where(row_lo, kv[:, :LANES], kv[:, LANES:])
                state_ref[p * LANES:(p + 1) * LANES, :] = st * cdec_ref[p * LANES:(p + 1) * LANES, :] + kv_sel
        return run

    def mixer_units(s):
        units = []
        for stage in (ret_scores, att_scores, ret_outputs, att_softmax, att_values):
            units += [stage(s, i) for i in range(BLKS_PER_SUB)]
        return units

    for s in range(N_SUB):
        fillers = []
        if s >= 1:
            fillers += [out_proj_chunk(s - 1, c) for c in range(OUT_CHUNKS)] + [finish(s - 1)]
        if s + 1 < N_SUB:
            fillers += [in_proj_chunk(s + 1, c) for c in range(IN_CHUNKS)]
        else:
            fillers += project_first_subtile(xnext_ref)
        if s == 0:
            fillers.pop(0)()
        _interleave(mixer_units(s), fillers)
    for c in range(OUT_CHUNKS):
        out_proj_chunk(N_SUB - 1, c)()
    finish(N_SUB - 1)()

    for g in range(ATT_KV_HEADS):
        kprev_ref[g] = carry["k"][g]
        vprev_ref[g] = carry["v"][g]


def _retention_tables(seq):
    pos = np.arange(seq, dtype=np.float64)
    theta = 1.0 / (ROT_BASE ** np.linspace(0.0, 1.0, RET_QK_DIM // 2))
    ang = pos[:, None] * theta[None, :]
    cos, sin = np.cos(ang), np.sin(ang)
    cos_l = np.tile(np.repeat(cos, 2, axis=1), (1, HEADS_PER_TILE))
    sign = np.tile(np.array([-1.0, 1.0]), RET_QK_DIM // 2)
    sin_l = np.tile(np.repeat(sin, 2, axis=1) * sign[None, :], (1, HEADS_PER_TILE))

    log_gamma = np.log(1.0 - 2.0 ** (-5.0 - np.arange(RET_HEADS, dtype=np.float64)))
    idx = np.arange(BLK, dtype=np.float64)
    rel = idx[:, None] - idx[None, :]
    decay_in = np.where(rel >= 0, np.exp(log_gamma[:, None, None] * np.maximum(rel, 0.0)), 0.0)
    k_dec = np.exp(log_gamma[:, None] * (BLK - 1 - idx)[None, :])
    q_dec = np.exp(log_gamma[:, None] * (idx + 1)[None, :])
    chunk_decay = np.exp(log_gamma * BLK)

    def per_pair(dec):
        t = np.repeat(dec.T[:, :, None], RET_QK_DIM, axis=2)
        t = t.reshape(BLK, RET_PAIRS, LANES)
        return np.transpose(t, (1, 0, 2))

    cdec = np.broadcast_to(np.repeat(chunk_decay, RET_QK_DIM)[:, None],
                           (RET_QK_WIDTH, RET_V_DIM))
    k_dec_t = np.transpose(per_pair(k_dec), (0, 2, 1))
    tables = (cos_l, sin_l, decay_in, per_pair(q_dec), k_dec_t, cdec)
    return tuple(jnp.asarray(np.ascontiguousarray(a, dtype=np.float32)) for a in tables)


def _full(shape):
    return pl.BlockSpec(shape, lambda step: (0,) * len(shape))


@jax.jit
def kernel(x, norm_g, w_in, att_sinks, ret_gn_g, w_out, final_g):
    batch, seq, d = x.shape
    depth = w_in.shape[0]
    assert depth == 1 and d == D_MODEL and seq % SEQ_TILE == 0
    cos_l, sin_l, decay_in, qdec, kdec, cdec = _retention_tables(seq)
    tile = SEQ_TILE
    n_seq = seq // tile
    n_tiles = batch * n_seq

    def tile_of(step):
        return jnp.maximum(step - CAST_STEPS, 0)

    def this_tile(step):
        t = tile_of(step)
        return (t // n_seq, t % n_seq, 0)

    def next_first_subtile(step):
        u = jnp.minimum(tile_of(step) + 1, n_tiles - 1)
        return (u // n_seq, (u % n_seq) * N_SUB, 0)

    def weight_slab(step):
        return (jnp.minimum(step, CAST_STEPS - 1), 0)

    call = pl.pallas_call(
        functools.partial(_layer_kernel, n_seq),
        grid=(CAST_STEPS + n_tiles,),
        in_specs=[
            pl.BlockSpec(memory_space=pltpu.SMEM),
            pl.BlockSpec((1, tile, D_MODEL), this_tile),
            pl.BlockSpec((1, SUB_TILE, D_MODEL), next_first_subtile),
            _full((1, D_MODEL)),
            pl.BlockSpec((WEIGHT_SLAB, IN_WIDTH), weight_slab),
            _full((1, RET_WIDTH)),
            pl.BlockSpec((WEIGHT_SLAB, D_MODEL), weight_slab),
            _full((1, D_MODEL)),
            pl.BlockSpec((tile, LANES), lambda step: (tile_of(step) % n_seq, 0)),
            pl.BlockSpec((tile, LANES), lambda step: (tile_of(step) % n_seq, 0)),
            _full((RET_HEADS, BLK, BLK)),
            _full((RET_PAIRS, BLK, LANES)),
            _full((RET_PAIRS, BLK, LANES)),
            _full((RET_QK_WIDTH, RET_V_DIM)),
        ],
        out_specs=pl.BlockSpec((1, tile, D_MODEL), this_tile),
        out_shape=jax.ShapeDtypeStruct(x.shape, x.dtype),
        scratch_shapes=[
            pltpu.VMEM((RET_QK_WIDTH, RET_V_DIM), F32),
            pltpu.VMEM((ATT_KV_HEADS, BLK, LANES), BF16),
            pltpu.VMEM((ATT_KV_HEADS, BLK, LANES), BF16),
            pltpu.VMEM((SUB_TILE, IN_WIDTH), F32),
            pltpu.VMEM((D_MODEL, IN_WIDTH), BF16),
            pltpu.VMEM((MIX_WIDTH, D_MODEL), BF16),
        ] + [pltpu.VMEM((SUB_TILE, MIX_WIDTH), BF16)] * N_SUB,
        compiler_params=pltpu.CompilerParams(
            dimension_semantics=("arbitrary",),
            vmem_limit_bytes=VMEM_LIMIT_BYTES),
        name="hymba_layer",
    )
    return call(att_sinks[0], x, x, norm_g[0][None, :], w_in[0],
                ret_gn_g[0][None, :], w_out[0], final_g[None, :],
                cos_l, sin_l, decay_in, qdec, kdec, cdec)
```

```python
import functools

import jax
import jax.numpy as jnp
import numpy as np
from jax import lax
from jax.experimental import pallas as pl
from jax.experimental.pallas import tpu as pltpu

D_MODEL = 1024
ATT_HEADS = 8
ATT_KV_HEADS = 2
ATT_HEAD_DIM = 64
WINDOW = 128
BLK = 128
RET_HEADS = 4
RET_QK_DIM = 64
RET_V_DIM = 128
ROT_BASE = 10000.0
RMS_EPS = 1e-6
GN_EPS = 1e-6
NEG_INF = -1e30
LOG2E = 1.4426950408889634

ATT_WIDTH = ATT_HEADS * ATT_HEAD_DIM
ATT_KV_WIDTH = ATT_KV_HEADS * ATT_HEAD_DIM
RET_QK_WIDTH = RET_HEADS * RET_QK_DIM
RET_WIDTH = RET_HEADS * RET_V_DIM
MIX_WIDTH = ATT_WIDTH + RET_WIDTH
IN_WIDTH = 2 * ATT_WIDTH + 2 * ATT_KV_WIDTH + 2 * RET_QK_WIDTH + 2 * RET_WIDTH

OFF_AQ = 0
OFF_AK = OFF_AQ + ATT_WIDTH
OFF_AV = OFF_AK + ATT_KV_WIDTH
OFF_AZ = OFF_AV + ATT_KV_WIDTH
OFF_RQ = OFF_AZ + ATT_WIDTH
OFF_RK = OFF_RQ + RET_QK_WIDTH
OFF_RV = OFF_RK + RET_QK_WIDTH
OFF_RZ = OFF_RV + RET_WIDTH

LANES = 128
MXU_COLS = 256
N_MXU = 2
PROJ_CHUNK = N_MXU * MXU_COLS
HEADS_PER_TILE = LANES // ATT_HEAD_DIM
ATT_PAIRS = ATT_HEADS // HEADS_PER_TILE
RET_PAIRS = RET_HEADS // HEADS_PER_TILE
GROUP = ATT_HEADS // ATT_KV_HEADS

SEQ_TILE = 1024
SUB_TILE = 256
N_SUB = SEQ_TILE // SUB_TILE
BLKS_PER_SUB = SUB_TILE // BLK
IN_CHUNKS = -(-IN_WIDTH // PROJ_CHUNK)
OUT_CHUNKS = D_MODEL // PROJ_CHUNK
WEIGHT_SLAB = 256
CAST_STEPS = D_MODEL // WEIGHT_SLAB
VMEM_LIMIT_BYTES = 56 * 1024 * 1024

F32 = jnp.float32
BF16 = jnp.bfloat16


def _silu(z):
    hz = 0.5 * z
    return hz + hz * jnp.tanh(hz)


def _interleave(units, fillers):
    done = 0
    for k, unit in enumerate(units):
        unit()
        due = -(-(k + 1) * len(fillers) // len(units))
        while done < due:
            fillers[done]()
            done += 1
    for f in fillers[done:]:
        f()


def _layer_kernel(tiles_per_seq, sinks_ref, x_ref, xnext_ref, ng_ref, win_slab_ref, gng_ref, wout_slab_ref, fg_ref,
                  cos_ref, sin_ref, din_ref, qdec_ref, kdec_ref, cdec_ref,
                  o_ref, state_ref, kprev_ref, vprev_ref, pfirst_ref, win_ref, wout_ref, *mix_refs):
    step = pl.program_id(0)

    @pl.when(step < CAST_STEPS)
    def _():
        rows = pl.ds(pl.multiple_of(step * WEIGHT_SLAB, WEIGHT_SLAB), WEIGHT_SLAB)
        win_ref[rows, :] = win_slab_ref[...].astype(BF16)
        wout_ref[rows, :] = wout_slab_ref[...].astype(BF16)

    @pl.when(step >= CAST_STEPS)
    def _():
        _tile_step(step - CAST_STEPS, tiles_per_seq, sinks_ref, x_ref, xnext_ref, ng_ref, win_ref, gng_ref, wout_ref,
                   fg_ref, cos_ref, sin_ref, din_ref, qdec_ref, kdec_ref, cdec_ref,
                   o_ref, state_ref, kprev_ref, vprev_ref, pfirst_ref, mix_refs)


def _tile_step(t, tiles_per_seq, sinks_ref, x_ref, xnext_ref, ng_ref, win_ref, gng_ref, wout_ref, fg_ref,
               cos_ref, sin_ref, din_ref, qdec_ref, kdec_ref, cdec_ref,
               o_ref, state_ref, kprev_ref, vprev_ref, pfirst_ref, mix_refs):
    j = lax.rem(t, tiles_per_seq)

    def project_first_subtile(src_ref):
        held = {}

        def chunk(c):
            def run():
                if not held:
                    xs = src_ref[0]
                    ms = jnp.mean(xs * xs, axis=-1, keepdims=True)
                    held["h"] = (xs * lax.rsqrt(ms + RMS_EPS) * ng_ref[...]).astype(BF16)
                c0, c1 = c * PROJ_CHUNK, min((c + 1) * PROJ_CHUNK, IN_WIDTH)
                pfirst_ref[:, c0:c1] = jnp.dot(held["h"], win_ref[:, c0:c1], preferred_element_type=F32)
            return run
        return [chunk(c) for c in range(IN_CHUNKS)]

    @pl.when(t == 0)
    def _():
        for f in project_first_subtile(x_ref.at[:, 0:SUB_TILE, :]):
            f()

    @pl.when(j == 0)
    def _():
        state_ref[...] = jnp.zeros_like(state_ref)
        kprev_ref[...] = jnp.zeros_like(kprev_ref)
        vprev_ref[...] = jnp.zeros_like(vprev_ref)

    lane = lax.broadcasted_iota(jnp.int32, (BLK, LANES), 1)
    lo_half = lane < ATT_HEAD_DIM
    hi_half = jnp.logical_not(lo_half)
    even_lane = (lane & 1) == 0
    lane2 = lax.broadcasted_iota(jnp.int32, (2 * BLK, LANES), 1)
    lo_half2 = lane2 < ATT_HEAD_DIM
    qi = lax.broadcasted_iota(jnp.int32, (BLK, 2 * BLK), 0)
    kj = lax.broadcasted_iota(jnp.int32, (BLK, 2 * BLK), 1)
    diff = qi + BLK - kj
    band = (diff >= 0) & (diff < WINDOW)
    in_cur = kj >= BLK
    ones_lo = jnp.where(lo_half2, 1.0, 0.0).astype(BF16)
    ones_hi = jnp.where(lo_half2, 0.0, 1.0).astype(BF16)
    zeros_bf = jnp.zeros((BLK, LANES), BF16)
    row_lo = lax.broadcasted_iota(jnp.int32, (LANES, LANES), 0) < RET_QK_DIM

    proj_chunks = [dict() for _ in range(N_SUB)]
    out_chunks = [dict() for _ in range(N_SUB)]
    normed = {}
    carry = {}
    att_rhs = {}

    def proj_tile(s, i, col):
        if s == 0:
            return pfirst_ref[i * BLK:(i + 1) * BLK, col:col + LANES]
        chunk, off = divmod(col, PROJ_CHUNK)
        return proj_chunks[s][chunk][i * BLK:(i + 1) * BLK, off:off + LANES]

    def in_proj_chunk(s, c):
        def run():
            if s not in normed:
                xs = x_ref[0, s * SUB_TILE:(s + 1) * SUB_TILE, :]
                ms = jnp.mean(xs * xs, axis=-1, keepdims=True)
                normed[s] = (xs * lax.rsqrt(ms + RMS_EPS) * ng_ref[...]).astype(BF16)
            proj_chunks[s][c] = jnp.dot(normed[s], win_ref[:, c * PROJ_CHUNK:min((c + 1) * PROJ_CHUNK, IN_WIDTH)],
                                        preferred_element_type=F32)
        return run

    def out_proj_chunk(s, c):
        def run():
            out_chunks[s][c] = jnp.dot(mix_refs[s][...], wout_ref[:, c * PROJ_CHUNK:(c + 1) * PROJ_CHUNK],
                                       preferred_element_type=F32)
        return run

    def finish(s):
        def run():
            rows = slice(s * SUB_TILE, (s + 1) * SUB_TILE)
            y = x_ref[0, rows, :] + jnp.concatenate([out_chunks[s][c] for c in range(OUT_CHUNKS)], axis=1)
            ms = jnp.mean(y * y, axis=-1, keepdims=True)
            o_ref[0, rows, :] = y * lax.rsqrt(ms + RMS_EPS) * fg_ref[...]
        return run

    def kv_prep(s, i):
        if s == 0 and i == 0:
            carry["k"] = [kprev_ref[g] for g in range(ATT_KV_HEADS)]
            carry["v"] = [vprev_ref[g] for g in range(ATT_KV_HEADS)]
        k_cur = proj_tile(s, i, OFF_AK)
        v_cur = proj_tile(s, i, OFF_AV)
        v_sw = pltpu.roll(v_cur, ATT_HEAD_DIM, 1)
        k_t = k_cur.T
        kdup_cur = [jnp.concatenate([k_t[g * ATT_HEAD_DIM:(g + 1) * ATT_HEAD_DIM]] * HEADS_PER_TILE,
                                    axis=0).astype(BF16) for g in range(ATT_KV_HEADS)]
        vdup_cur = [jnp.where(lo_half, v_cur, v_sw).astype(BF16),
                    jnp.where(lo_half, v_sw, v_cur).astype(BF16)]
        per_head = []
        for g in range(ATT_KV_HEADS):
            kdup = jnp.concatenate([carry["k"][g], kdup_cur[g]], axis=1)
            vdup = jnp.concatenate([carry["v"][g], vdup_cur[g]], axis=0)
            zero2 = jnp.zeros_like(vdup)
            rhs_v = jnp.concatenate([
                jnp.concatenate([jnp.where(lo_half2, vdup, zero2), ones_lo], axis=1),
                jnp.concatenate([jnp.where(lo_half2, zero2, vdup), ones_hi], axis=1),
            ], axis=0)
            per_head.append((kdup, rhs_v))
        att_rhs[(s, i)] = per_head
        carry["k"], carry["v"] = kdup_cur, vdup_cur

    stash = {}

    def att_scores(s, i):
        def run():
            kv_prep(s, i)
            valid = band & (in_cur | (j > 0)) if (s == 0 and i == 0) else band
            scores = []
            for p in range(ATT_PAIRS):
                kdup, _ = att_rhs[(s, i)][p // (GROUP // HEADS_PER_TILE)]
                q2 = (proj_tile(s, i, OFF_AQ + p * LANES) * (ATT_HEAD_DIM ** -0.5 * LOG2E)).astype(BF16)
                for e in range(HEADS_PER_TILE):
                    qm = jnp.where(lo_half if e == 0 else hi_half, q2, zeros_bf)
                    sc = jnp.dot(qm, kdup, preferred_element_type=F32)
                    scores.append(jnp.where(valid, sc, NEG_INF))
            stash[("sc", s, i)] = scores
        return run

    def att_rowmax(s, i):
        def run():
            stash[("m", s, i)] = [jnp.maximum(jnp.max(sc, axis=-1, keepdims=True), sinks_ref[head] * LOG2E)
                                  for head, sc in enumerate(stash[("sc", s, i)])]
        return run

    def att_softmax(s, i):
        def run():
            probs, sink_terms = [], []
            maxes = stash.pop(("m", s, i))
            for head, sc in enumerate(stash.pop(("sc", s, i))):
                m = maxes[head]
                probs.append(jnp.exp2(sc - m).astype(BF16))
                sink_terms.append(jnp.exp2(sinks_ref[head] * LOG2E - m))
            stash[("p", s, i)] = (probs, sink_terms)
        return run

    def att_values(s, i):
        def run():
            probs, sink_terms = stash.pop(("p", s, i))
            for p in range(ATT_PAIRS):
                _, rhs_v = att_rhs[(s, i)][p // (GROUP // HEADS_PER_TILE)]
                h0 = p * HEADS_PER_TILE
                lhs = jnp.concatenate(probs[h0:h0 + HEADS_PER_TILE], axis=1)
                res = jnp.dot(lhs, rhs_v, preferred_element_type=F32)
                den = res[:, LANES:] + jnp.where(lo_half, sink_terms[h0], sink_terms[h0 + 1])
                a2 = res[:, :LANES] / den
                z2 = proj_tile(s, i, OFF_AZ + p * LANES)
                mix_refs[s][i * BLK:(i + 1) * BLK, p * LANES:(p + 1) * LANES] = (a2 * _silu(z2)).astype(BF16)
        return run

    def ret_rotate(s, i):
        def run():
            r0 = (s * BLKS_PER_SUB + i) * BLK
            cos = cos_ref[r0:r0 + BLK, :]
            sin = sin_ref[r0:r0 + BLK, :]

            def rotate(t):
                swapped = jnp.where(even_lane, pltpu.roll(t, LANES - 1, 1), pltpu.roll(t, 1, 1))
                return t * cos + swapped * sin

            prepared = []
            for p in range(RET_PAIRS):
                q2 = rotate(proj_tile(s, i, OFF_RQ + p * LANES))
                k2 = rotate(proj_tile(s, i, OFF_RK + p * LANES)) * (RET_QK_DIM ** -0.5)
                q2b = q2.astype(BF16)
                qd2b = (q2 * qdec_ref[p]).astype(BF16)
                k2t = k2.T
                k2tb = k2t.astype(BF16)
                kd2tb = (k2t * kdec_ref[p]).astype(BF16)
                prepared.append((q2b, qd2b, k2tb, kd2tb))
            stash[("rot", s, i)] = prepared
        return run

    def ret_scores(s, i):
        def run():
            per_pair = []
            for p, (q2b, qd2b, k2tb, kd2tb) in enumerate(stash.pop(("rot", s, i))):
                rhs_qk = jnp.concatenate([jnp.where(row_lo, k2tb, zeros_bf),
                                          jnp.where(row_lo, zeros_bf, k2tb)], axis=1)
                sc2 = jnp.dot(q2b, rhs_qk, preferred_element_type=F32)
                lhs = []
                for e in range(HEADS_PER_TILE):
                    sel = lo_half if e == 0 else hi_half
                    sc = sc2[:, e * BLK:(e + 1) * BLK] * din_ref[p * HEADS_PER_TILE + e]
                    lhs.append(jnp.concatenate([sc.astype(BF16), jnp.where(sel, qd2b, zeros_bf)], axis=1))
                per_pair.append((lhs, kd2tb))
            stash[("ret", s, i)] = per_pair
        return run

    def ret_outputs(s, i):
        def run():
            per_pair = stash.pop(("ret", s, i))
            for p in range(RET_PAIRS):
                lhs, kd2tb = per_pair[p]
                st = state_ref[p * LANES:(p + 1) * LANES, :]
                st_b = st.astype(BF16)
                v_heads = [proj_tile(s, i, OFF_RV + (p * HEADS_PER_TILE + e) * LANES).astype(BF16)
                           for e in range(HEADS_PER_TILE)]
                for e in range(HEADS_PER_TILE):
                    head = p * HEADS_PER_TILE + e
                    rhs = jnp.concatenate([v_heads[e], st_b], axis=0)
                    o = jnp.dot(lhs[e], rhs, preferred_element_type=F32)
                    mu = jnp.mean(o, axis=-1, keepdims=True)
                    oc = o - mu
                    var = jnp.mean(oc * oc, axis=-1, keepdims=True)
                    on = oc * lax.rsqrt(var + GN_EPS) * gng_ref[:, head * LANES:(head + 1) * LANES]
                    rz = proj_tile(s, i, OFF_RZ + head * LANES)
                    mc = ATT_WIDTH + head * LANES
                    mix_refs[s][i * BLK:(i + 1) * BLK, mc:mc + LANES] = (on * _silu(rz)).astype(BF16)
                kv = jnp.dot(kd2tb, jnp.concatenate(v_heads, axis=1),
                             preferred_element_type=F32)
                kv_sel = jnp.where(row_lo, kv[:, :LANES], kv[:, LANES:])
                state_ref[p * LANES:(p + 1) * LANES, :] = st * cdec_ref[p * LANES:(p + 1) * LANES, :] + kv_sel
        return run

    def mixer_units(s):
        units = []
        for stage in (att_scores, ret_rotate, att_rowmax, ret_scores, att_softmax, ret_outputs, att_values):
            units += [stage(s, i) for i in range(BLKS_PER_SUB)]
        return units

    for s in range(N_SUB):
        fillers = []
        if s >= 1:
            fillers += [out_proj_chunk(s - 1, c) for c in range(OUT_CHUNKS)] + [finish(s - 1)]
        if s + 1 < N_SUB:
            fillers += [in_proj_chunk(s + 1, c) for c in range(IN_CHUNKS)]
        else:
            fillers += project_first_subtile(xnext_ref)
        if s == 0:
            fillers.pop(0)()
        _interleave(mixer_units(s), fillers)
    for c in range(OUT_CHUNKS):
        out_proj_chunk(N_SUB - 1, c)()
    finish(N_SUB - 1)()

    for g in range(ATT_KV_HEADS):
        kprev_ref[g] = carry["k"][g]
        vprev_ref[g] = carry["v"][g]


def _retention_tables(seq):
    pos = np.arange(seq, dtype=np.float64)
    theta = 1.0 / (ROT_BASE ** np.linspace(0.0, 1.0, RET_QK_DIM // 2))
    ang = pos[:, None] * theta[None, :]
    cos, sin = np.cos(ang), np.sin(ang)
    cos_l = np.tile(np.repeat(cos, 2, axis=1), (1, HEADS_PER_TILE))
    sign = np.tile(np.array([-1.0, 1.0]), RET_QK_DIM // 2)
    sin_l = np.tile(np.repeat(sin, 2, axis=1) * sign[None, :], (1, HEADS_PER_TILE))

    log_gamma = np.log(1.0 - 2.0 ** (-5.0 - np.arange(RET_HEADS, dtype=np.float64)))
    idx = np.arange(BLK, dtype=np.float64)
    rel = idx[:, None] - idx[None, :]
    decay_in = np.where(rel >= 0, np.exp(log_gamma[:, None, None] * np.maximum(rel, 0.0)), 0.0)
    k_dec = np.exp(log_gamma[:, None] * (BLK - 1 - idx)[None, :])
    q_dec = np.exp(log_gamma[:, None] * (idx + 1)[None, :])
    chunk_decay = np.exp(log_gamma * BLK)

    def per_pair(dec):
        t = np.repeat(dec.T[:, :, None], RET_QK_DIM, axis=2)
        t = t.reshape(BLK, RET_PAIRS, LANES)
        return np.transpose(t, (1, 0, 2))

    cdec = np.broadcast_to(np.repeat(chunk_decay, RET_QK_DIM)[:, None],
                           (RET_QK_WIDTH, RET_V_DIM))
    k_dec_t = np.transpose(per_pair(k_dec), (0, 2, 1))
    tables = (cos_l, sin_l, decay_in, per_pair(q_dec), k_dec_t, cdec)
    return tuple(jnp.asarray(np.ascontiguousarray(a, dtype=np.float32)) for a in tables)


def _full(shape):
    return pl.BlockSpec(shape, lambda step: (0,) * len(shape))


@jax.jit
def kernel(x, norm_g, w_in, att_sinks, ret_gn_g, w_out, final_g):
    batch, seq, d = x.shape
    depth = w_in.shape[0]
    assert depth == 1 and d == D_MODEL and seq % SEQ_TILE == 0
    cos_l, sin_l, decay_in, qdec, kdec, cdec = _retention_tables(seq)
    tile = SEQ_TILE
    n_seq = seq // tile
    n_tiles = batch * n_seq

    def tile_of(step):
        return jnp.maximum(step - CAST_STEPS, 0)

    def this_tile(step):
        t = tile_of(step)
        return (t // n_seq, t % n_seq, 0)

    def next_first_subtile(step):
        u = jnp.minimum(tile_of(step) + 1, n_tiles - 1)
        return (u // n_seq, (u % n_seq) * N_SUB, 0)

    def weight_slab(step):
        return (jnp.minimum(step, CAST_STEPS - 1), 0)

    call = pl.pallas_call(
        functools.partial(_layer_kernel, n_seq),
        grid=(CAST_STEPS + n_tiles,),
        in_specs=[
            pl.BlockSpec(memory_space=pltpu.SMEM),
            pl.BlockSpec((1, tile, D_MODEL), this_tile),
            pl.BlockSpec((1, SUB_TILE, D_MODEL), next_first_subtile),
            _full((1, D_MODEL)),
            pl.BlockSpec((WEIGHT_SLAB, IN_WIDTH), weight_slab),
            _full((1, RET_WIDTH)),
            pl.BlockSpec((WEIGHT_SLAB, D_MODEL), weight_slab),
            _full((1, D_MODEL)),
            pl.BlockSpec((tile, LANES), lambda step: (tile_of(step) % n_seq, 0)),
            pl.BlockSpec((tile, LANES), lambda step: (tile_of(step) % n_seq, 0)),
            _full((RET_HEADS, BLK, BLK)),
            _full((RET_PAIRS, BLK, LANES)),
            _full((RET_PAIRS, BLK, LANES)),
            _full((RET_QK_WIDTH, RET_V_DIM)),
        ],
        out_specs=pl.BlockSpec((1, tile, D_MODEL), this_tile),
        out_shape=jax.ShapeDtypeStruct(x.shape, x.dtype),
        scratch_shapes=[
            pltpu.VMEM((RET_QK_WIDTH, RET_V_DIM), F32),
            pltpu.VMEM((ATT_KV_HEADS, BLK, LANES), BF16),
            pltpu.VMEM((ATT_KV_HEADS, BLK, LANES), BF16),
            pltpu.VMEM((SUB_TILE, IN_WIDTH), F32),
            pltpu.VMEM((D_MODEL, IN_WIDTH), BF16),
            pltpu.VMEM((MIX_WIDTH, D_MODEL), BF16),
        ] + [pltpu.VMEM((SUB_TILE, MIX_WIDTH), BF16)] * N_SUB,
        compiler_params=pltpu.CompilerParams(
            dimension_semantics=("arbitrary",),
            vmem_limit_bytes=VMEM_LIMIT_BYTES),
        name="hymba_layer",
    )
    return call(att_sinks[0], x, x, norm_g[0][None, :], w_in[0],
                ret_gn_g[0][None, :], w_out[0], final_g[None, :],
                cos_l, sin_l, decay_in, qdec, kdec, cdec)
```

```python
import functools

import jax
import jax.numpy as jnp
import numpy as np
from jax import lax
from jax.experimental import pallas as pl
from jax.experimental.pallas import tpu as pltpu

D_MODEL = 1024
ATT_HEADS = 8
ATT_KV_HEADS = 2
ATT_HEAD_DIM = 64
WINDOW = 128
BLK = 128
RET_HEADS = 4
RET_QK_DIM = 64
RET_V_DIM = 128
ROT_BASE = 10000.0
RMS_EPS = 1e-6
GN_EPS = 1e-6
NEG_INF = -1e30
LOG2E = 1.4426950408889634

ATT_WIDTH = ATT_HEADS * ATT_HEAD_DIM
ATT_KV_WIDTH = ATT_KV_HEADS * ATT_HEAD_DIM
RET_QK_WIDTH = RET_HEADS * RET_QK_DIM
RET_WIDTH = RET_HEADS * RET_V_DIM
MIX_WIDTH = ATT_WIDTH + RET_WIDTH
IN_WIDTH = 2 * ATT_WIDTH + 2 * ATT_KV_WIDTH + 2 * RET_QK_WIDTH + 2 * RET_WIDTH

OFF_AQ = 0
OFF_AK = OFF_AQ + ATT_WIDTH
OFF_AV = OFF_AK + ATT_KV_WIDTH
OFF_AZ = OFF_AV + ATT_KV_WIDTH
OFF_RQ = OFF_AZ + ATT_WIDTH
OFF_RK = OFF_RQ + RET_QK_WIDTH
OFF_RV = OFF_RK + RET_QK_WIDTH
OFF_RZ = OFF_RV + RET_WIDTH

LANES = 128
MXU_COLS = 256
N_MXU = 2
PROJ_CHUNK = N_MXU * MXU_COLS
HEADS_PER_TILE = LANES // ATT_HEAD_DIM
ATT_PAIRS = ATT_HEADS // HEADS_PER_TILE
RET_PAIRS = RET_HEADS // HEADS_PER_TILE
GROUP = ATT_HEADS // ATT_KV_HEADS

SEQ_TILE = 1024
SUB_TILE = 256
N_SUB = SEQ_TILE // SUB_TILE
BLKS_PER_SUB = SUB_TILE // BLK
IN_CHUNKS = -(-IN_WIDTH // PROJ_CHUNK)
OUT_CHUNKS = D_MODEL // PROJ_CHUNK
WEIGHT_SLAB = 256
CAST_STEPS = D_MODEL // WEIGHT_SLAB
VMEM_LIMIT_BYTES = 56 * 1024 * 1024

F32 = jnp.float32
BF16 = jnp.bfloat16


def _silu(z):
    hz = 0.5 * z
    return hz + hz * jnp.tanh(hz)


def _interleave(units, fillers):
    done = 0
    for k, unit in enumerate(units):
        unit()
        due = -(-(k + 1) * len(fillers) // len(units))
        while done < due:
            fillers[done]()
            done += 1
    for f in fillers[done:]:
        f()


def _layer_kernel(tiles_per_seq, sinks_ref, x_ref, xnext_ref, ng_ref, win_slab_ref, gng_ref, wout_slab_ref, fg_ref,
                  cos_ref, sin_ref, din_ref, qdec_ref, kdec_ref, cdec_ref,
                  o_ref, state_ref, kprev_ref, vprev_ref, pfirst_ref, win_ref, wout_ref, *mix_refs):
    step = pl.program_id(0)

    @pl.when(step < CAST_STEPS)
    def _():
        rows = pl.ds(pl.multiple_of(step * WEIGHT_SLAB, WEIGHT_SLAB), WEIGHT_SLAB)
        win_ref[rows, :] = win_slab_ref[...].astype(BF16)
        wout_ref[rows, :] = wout_slab_ref[...].astype(BF16)

    @pl.when(step >= CAST_STEPS)
    def _():
        _tile_step(step - CAST_STEPS, tiles_per_seq, sinks_ref, x_ref, xnext_ref, ng_ref, win_ref, gng_ref, wout_ref,
                   fg_ref, cos_ref, sin_ref, din_ref, qdec_ref, kdec_ref, cdec_ref,
                   o_ref, state_ref, kprev_ref, vprev_ref, pfirst_ref, mix_refs)


def _tile_step(t, tiles_per_seq, sinks_ref, x_ref, xnext_ref, ng_ref, win_ref, gng_ref, wout_ref, fg_ref,
               cos_ref, sin_ref, din_ref, qdec_ref, kdec_ref, cdec_ref,
               o_ref, state_ref, kprev_ref, vprev_ref, pfirst_ref, mix_refs):
    j = lax.rem(t, tiles_per_seq)

    def project_first_subtile(src_ref):
        held = {}

        def chunk(c):
            def run():
                if not held:
                    xs = src_ref[0]
                    ms = jnp.mean(xs * xs, axis=-1, keepdims=True)
                    held["h"] = (xs * lax.rsqrt(ms + RMS_EPS) * ng_ref[...]).astype(BF16)
                c0, c1 = c * PROJ_CHUNK, min((c + 1) * PROJ_CHUNK, IN_WIDTH)
                pfirst_ref[:, c0:c1] = jnp.dot(held["h"], win_ref[:, c0:c1], preferred_element_type=F32)
            return run
        return [chunk(c) for c in range(IN_CHUNKS)]

    @pl.when(t == 0)
    def _():
        for f in project_first_subtile(x_ref.at[:, 0:SUB_TILE, :]):
            f()

    @pl.when(j == 0)
    def _():
        state_ref[...] = jnp.zeros_like(state_ref)
        kprev_ref[...] = jnp.zeros_like(kprev_ref)
        vprev_ref[...] = jnp.zeros_like(vprev_ref)

    lane = lax.broadcasted_iota(jnp.int32, (BLK, LANES), 1)
    lo_half = lane < ATT_HEAD_DIM
    hi_half = jnp.logical_not(lo_half)
    even_lane = (lane & 1) == 0
    lane2 = lax.broadcasted_iota(jnp.int32, (2 * BLK, LANES), 1)
    lo_half2 = lane2 < ATT_HEAD_DIM
    qi = lax.broadcasted_iota(jnp.int32, (BLK, 2 * BLK), 0)
    kj = lax.broadcasted_iota(jnp.int32, (BLK, 2 * BLK), 1)
    diff = qi + BLK - kj
    band = (diff >= 0) & (diff < WINDOW)
    in_cur = kj >= BLK
    ones_lo = jnp.where(lo_half2, 1.0, 0.0).astype(BF16)
    ones_hi = jnp.where(lo_half2, 0.0, 1.0).astype(BF16)
    zeros_bf = jnp.zeros((BLK, LANES), BF16)
    row_lo = lax.broadcasted_iota(jnp.int32, (LANES, LANES), 0) < RET_QK_DIM

    proj_chunks = [dict() for _ in range(N_SUB)]
    out_chunks = [dict() for _ in range(N_SUB)]
    normed = {}
    carry = {}
    att_rhs = {}

    def proj_tile(s, i, col):
        if s == 0:
            return pfirst_ref[i * BLK:(i + 1) * BLK, col:col + LANES]
        chunk, off = divmod(col, PROJ_CHUNK)
        return proj_chunks[s][chunk][i * BLK:(i + 1) * BLK, off:off + LANES]

    def in_proj_chunk(s, c):
        def run():
            if s not in normed:
                xs = x_ref[0, s * SUB_TILE:(s + 1) * SUB_TILE, :]
                ms = jnp.mean(xs * xs, axis=-1, keepdims=True)
                normed[s] = (xs * lax.rsqrt(ms + RMS_EPS) * ng_ref[...]).astype(BF16)
            proj_chunks[s][c] = jnp.dot(normed[s], win_ref[:, c * PROJ_CHUNK:min((c + 1) * PROJ_CHUNK, IN_WIDTH)],
                                        preferred_element_type=F32)
        return run

    def out_proj_chunk(s, c):
        def run():
            out_chunks[s][c] = jnp.dot(mix_refs[s][...], wout_ref[:, c * PROJ_CHUNK:(c + 1) * PROJ_CHUNK],
                                       preferred_element_type=F32)
        return run

    def finish(s):
        def run():
            rows = slice(s * SUB_TILE, (s + 1) * SUB_TILE)
            y = x_ref[0, rows, :] + jnp.concatenate([out_chunks[s][c] for c in range(OUT_CHUNKS)], axis=1)
            ms = jnp.mean(y * y, axis=-1, keepdims=True)
            o_ref[0, rows, :] = y * lax.rsqrt(ms + RMS_EPS) * fg_ref[...]
        return run

    def kv_prep(s, i):
        if s == 0 and i == 0:
            carry["k"] = [kprev_ref[g] for g in range(ATT_KV_HEADS)]
            carry["v"] = [vprev_ref[g] for g in range(ATT_KV_HEADS)]
        k_cur = proj_tile(s, i, OFF_AK)
        v_cur = proj_tile(s, i, OFF_AV)
        v_sw = pltpu.roll(v_cur, ATT_HEAD_DIM, 1)
        k_t = k_cur.T
        kdup_cur = [jnp.concatenate([k_t[g * ATT_HEAD_DIM:(g + 1) * ATT_HEAD_DIM]] * HEADS_PER_TILE,
                                    axis=0).astype(BF16) for g in range(ATT_KV_HEADS)]
        vdup_cur = [jnp.where(lo_half, v_cur, v_sw).astype(BF16),
                    jnp.where(lo_half, v_sw, v_cur).astype(BF16)]
        per_head = []
        for g in range(ATT_KV_HEADS):
            kdup = jnp.concatenate([carry["k"][g], kdup_cur[g]], axis=1)
            vdup = jnp.concatenate([carry["v"][g], vdup_cur[g]], axis=0)
            zero2 = jnp.zeros_like(vdup)
            rhs_v = jnp.concatenate([
                jnp.concatenate([jnp.where(lo_half2, vdup, zero2), ones_lo], axis=1),
                jnp.concatenate([jnp.where(lo_half2, zero2, vdup), ones_hi], axis=1),
            ], axis=0)
            per_head.append((kdup, rhs_v))
        att_rhs[(s, i)] = per_head
        carry["k"], carry["v"] = kdup_cur, vdup_cur

    stash = {}

    def att_scores(s, i):
        def run():
            kv_prep(s, i)
            valid = band & (in_cur | (j > 0)) if (s == 0 and i == 0) else band
            scores = []
            for p in range(ATT_PAIRS):
                kdup, _ = att_rhs[(s, i)][p // (GROUP // HEADS_PER_TILE)]
                q2 = (proj_tile(s, i, OFF_AQ + p * LANES) * (ATT_HEAD_DIM ** -0.5 * LOG2E)).astype(BF16)
                for e in range(HEADS_PER_TILE):
                    qm = jnp.where(lo_half if e == 0 else hi_half, q2, zeros_bf)
                    sc = jnp.dot(qm, kdup, preferred_element_type=F32)
                    scores.append(jnp.where(valid, sc, NEG_INF))
            stash[("sc", s, i)] = scores
        return run

    def att_softmax(s, i):
        def run():
            probs, sink_terms = [], []
            for head, sc in enumerate(stash.pop(("sc", s, i))):
                sink = sinks_ref[head] * LOG2E
                m = jnp.maximum(jnp.max(sc, axis=-1, keepdims=True), sink)
                probs.append(jnp.exp2(sc - m).astype(BF16))
                sink_terms.append(jnp.exp2(sink - m))
            stash[("p", s, i)] = (probs, sink_terms)
        return run

    def att_values(s, i):
        def run():
            probs, sink_terms = stash.pop(("p", s, i))
            for p in range(ATT_PAIRS):
                _, rhs_v = att_rhs[(s, i)][p // (GROUP // HEADS_PER_TILE)]
                h0 = p * HEADS_PER_TILE
                lhs = jnp.concatenate(probs[h0:h0 + HEADS_PER_TILE], axis=1)
                res = jnp.dot(lhs, rhs_v, preferred_element_type=F32)
                den = res[:, LANES:] + jnp.where(lo_half, sink_terms[h0], sink_terms[h0 + 1])
                a2 = res[:, :LANES] / den
                z2 = proj_tile(s, i, OFF_AZ + p * LANES)
                mix_refs[s][i * BLK:(i + 1) * BLK, p * LANES:(p + 1) * LANES] = (a2 * _silu(z2)).astype(BF16)
        return run

    def ret_scores(s, i):
        def run():
            r0 = (s * BLKS_PER_SUB + i) * BLK
            cos = cos_ref[r0:r0 + BLK, :]
            sin = sin_ref[r0:r0 + BLK, :]

            def rotate(t):
                swapped = jnp.where(even_lane, pltpu.roll(t, LANES - 1, 1), pltpu.roll(t, 1, 1))
                return t * cos + swapped * sin

            per_pair = []
            for p in range(RET_PAIRS):
                q2 = rotate(proj_tile(s, i, OFF_RQ + p * LANES))
                k2 = rotate(proj_tile(s, i, OFF_RK + p * LANES)) * (RET_QK_DIM ** -0.5)
                q2b = q2.astype(BF16)
                qd2b = (q2 * qdec_ref[p]).astype(BF16)
                k2t = k2.T
                k2tb = k2t.astype(BF16)
                kd2tb = (k2t * kdec_ref[p]).astype(BF16)
                rhs_qk = jnp.concatenate([jnp.where(row_lo, k2tb, zeros_bf),
                                          jnp.where(row_lo, zeros_bf, k2tb)], axis=1)
                sc2 = jnp.dot(q2b, rhs_qk, preferred_element_type=F32)
                lhs = []
                for e in range(HEADS_PER_TILE):
                    sel = lo_half if e == 0 else hi_half
                    sc = sc2[:, e * BLK:(e + 1) * BLK] * din_ref[p * HEADS_PER_TILE + e]
                    lhs.append(jnp.concatenate([sc.astype(BF16), jnp.where(sel, qd2b, zeros_bf)], axis=1))
                per_pair.append((lhs, kd2tb))
            stash[("ret", s, i)] = per_pair
        return run

    def ret_outputs(s, i):
        def run():
            per_pair = stash.pop(("ret", s, i))
            for p in range(RET_PAIRS):
                lhs, kd2tb = per_pair[p]
                st = state_ref[p * LANES:(p + 1) * LANES, :]
                st_b = st.astype(BF16)
                v_heads = [proj_tile(s, i, OFF_RV + (p * HEADS_PER_TILE + e) * LANES).astype(BF16)
                           for e in range(HEADS_PER_TILE)]
                for e in range(HEADS_PER_TILE):
                    head = p * HEADS_PER_TILE + e
                    rhs = jnp.concatenate([v_heads[e], st_b], axis=0)
                    o = jnp.dot(lhs[e], rhs, preferred_element_type=F32)
                    mu = jnp.mean(o, axis=-1, keepdims=True)
                    oc = o - mu
                    var = jnp.mean(oc * oc, axis=-1, keepdims=True)
                    on = oc * lax.rsqrt(var + GN_EPS) * gng_ref[:, head * LANES:(head + 1) * LANES]
                    rz = proj_tile(s, i, OFF_RZ + head * LANES)
                    mc = ATT_WIDTH + head * LANES
                    mix_refs[s][i * BLK:(i + 1) * BLK, mc:mc + LANES] = (on * _silu(rz)).astype(BF16)
                kv = jnp.dot(kd2tb, jnp.concatenate(v_heads, axis=1),
                             preferred_element_type=F32)
                kv_sel = jnp.where(row_lo, kv[:, :LANES], kv[:, LANES:])
                state_ref[p * LANES:(p + 1) * LANES, :] = st * cdec_ref[p * LANES:(p + 1) * LANES, :] + kv_sel
        return run

    def mixer_units(s):
        def both_blocks(stage):
            def run():
                for i in range(BLKS_PER_SUB):
                    stage(s, i)()
            return run
        return [both_blocks(stage) for stage in (att_scores, ret_scores, att_softmax, ret_outputs, att_values)]

    for s in range(N_SUB):
        fillers = []
        if s >= 1:
            fillers += [out_proj_chunk(s - 1, c) for c in range(OUT_CHUNKS)] + [finish(s - 1)]
        if s + 1 < N_SUB:
            fillers += [in_proj_chunk(s + 1, c) for c in range(IN_CHUNKS)]
        else:
            fillers += project_first_subtile(xnext_ref)
        if s == 0:
            fillers.pop(0)()
        _interleave(mixer_units(s), fillers)
    for c in range(OUT_CHUNKS):
        out_proj_chunk(N_SUB - 1, c)()
    finish(N_SUB - 1)()

    for g in range(ATT_KV_HEADS):
        kprev_ref[g] = carry["k"][g]
        vprev_ref[g] = carry["v"][g]


def _retention_tables(seq):
    pos = np.arange(seq, dtype=np.float64)
    theta = 1.0 / (ROT_BASE ** np.linspace(0.0, 1.0, RET_QK_DIM // 2))
    ang = pos[:, None] * theta[None, :]
    cos, sin = np.cos(ang), np.sin(ang)
    cos_l = np.tile(np.repeat(cos, 2, axis=1), (1, HEADS_PER_TILE))
    sign = np.tile(np.array([-1.0, 1.0]), RET_QK_DIM // 2)
    sin_l = np.tile(np.repeat(sin, 2, axis=1) * sign[None, :], (1, HEADS_PER_TILE))

    log_gamma = np.log(1.0 - 2.0 ** (-5.0 - np.arange(RET_HEADS, dtype=np.float64)))
    idx = np.arange(BLK, dtype=np.float64)
    rel = idx[:, None] - idx[None, :]
    decay_in = np.where(rel >= 0, np.exp(log_gamma[:, None, None] * np.maximum(rel, 0.0)), 0.0)
    k_dec = np.exp(log_gamma[:, None] * (BLK - 1 - idx)[None, :])
    q_dec = np.exp(log_gamma[:, None] * (idx + 1)[None, :])
    chunk_decay = np.exp(log_gamma * BLK)

    def per_pair(dec):
        t = np.repeat(dec.T[:, :, None], RET_QK_DIM, axis=2)
        t = t.reshape(BLK, RET_PAIRS, LANES)
        return np.transpose(t, (1, 0, 2))

    cdec = np.broadcast_to(np.repeat(chunk_decay, RET_QK_DIM)[:, None],
                           (RET_QK_WIDTH, RET_V_DIM))
    k_dec_t = np.transpose(per_pair(k_dec), (0, 2, 1))
    tables = (cos_l, sin_l, decay_in, per_pair(q_dec), k_dec_t, cdec)
    return tuple(jnp.asarray(np.ascontiguousarray(a, dtype=np.float32)) for a in tables)


def _full(shape):
    return pl.BlockSpec(shape, lambda step: (0,) * len(shape))


@jax.jit
def kernel(x, norm_g, w_in, att_sinks, ret_gn_g, w_out, final_g):
    batch, seq, d = x.shape
    depth = w_in.shape[0]
    assert depth == 1 and d == D_MODEL and seq % SEQ_TILE == 0
    cos_l, sin_l, decay_in, qdec, kdec, cdec = _retention_tables(seq)
    tile = SEQ_TILE
    n_seq = seq // tile
    n_tiles = batch * n_seq

    def tile_of(step):
        return jnp.maximum(step - CAST_STEPS, 0)

    def this_tile(step):
        t = tile_of(step)
        return (t // n_seq, t % n_seq, 0)

    def next_first_subtile(step):
        u = jnp.minimum(tile_of(step) + 1, n_tiles - 1)
        return (u // n_seq, (u % n_seq) * N_SUB, 0)

    def weight_slab(step):
        return (jnp.minimum(step, CAST_STEPS - 1), 0)

    call = pl.pallas_call(
        functools.partial(_layer_kernel, n_seq),
        grid=(CAST_STEPS + n_tiles,),
        in_specs=[
            pl.BlockSpec(memory_space=pltpu.SMEM),
            pl.BlockSpec((1, tile, D_MODEL), this_tile),
            pl.BlockSpec((1, SUB_TILE, D_MODEL), next_first_subtile),
            _full((1, D_MODEL)),
            pl.BlockSpec((WEIGHT_SLAB, IN_WIDTH), weight_slab),
            _full((1, RET_WIDTH)),
            pl.BlockSpec((WEIGHT_SLAB, D_MODEL), weight_slab),
            _full((1, D_MODEL)),
            pl.BlockSpec((tile, LANES), lambda step: (tile_of(step) % n_seq, 0)),
            pl.BlockSpec((tile, LANES), lambda step: (tile_of(step) % n_seq, 0)),
            _full((RET_HEADS, BLK, BLK)),
            _full((RET_PAIRS, BLK, LANES)),
            _full((RET_PAIRS, BLK, LANES)),
            _full((RET_QK_WIDTH, RET_V_DIM)),
        ],
        out_specs=pl.BlockSpec((1, tile, D_MODEL), this_tile),
        out_shape=jax.ShapeDtypeStruct(x.shape, x.dtype),
        scratch_shapes=[
            pltpu.VMEM((RET_QK_WIDTH, RET_V_DIM), F32),
            pltpu.VMEM((ATT_KV_HEADS, BLK, LANES), BF16),
            pltpu.VMEM((ATT_KV_HEADS, BLK, LANES), BF16),
            pltpu.VMEM((SUB_TILE, IN_WIDTH), F32),
            pltpu.VMEM((D_MODEL, IN_WIDTH), BF16),
            pltpu.VMEM((MIX_WIDTH, D_MODEL), BF16),
        ] + [pltpu.VMEM((SUB_TILE, MIX_WIDTH), BF16)] * N_SUB,
        compiler_params=pltpu.CompilerParams(
            dimension_semantics=("arbitrary",),
            vmem_limit_bytes=VMEM_LIMIT_BYTES),
        name="hymba_layer",
    )
    return call(att_sinks[0], x, x, norm_g[0][None, :], w_in[0],
                ret_gn_g[0][None, :], w_out[0], final_g[None, :],
                cos_l, sin_l, decay_in, qdec, kdec, cdec)
```

```python
import functools

import jax
import jax.numpy as jnp
import numpy as np
from jax import lax
from jax.experimental import pallas as pl
from jax.experimental.pallas import tpu as pltpu

D_MODEL = 1024
ATT_HEADS = 8
ATT_KV_HEADS = 2
ATT_HEAD_DIM = 64
WINDOW = 128
BLK = 128
RET_HEADS = 4
RET_QK_DIM = 64
RET_V_DIM = 128
ROT_BASE = 10000.0
RMS_EPS = 1e-6
GN_EPS = 1e-6
NEG_INF = -1e30
LOG2E = 1.4426950408889634

ATT_WIDTH = ATT_HEADS * ATT_HEAD_DIM
ATT_KV_WIDTH = ATT_KV_HEADS * ATT_HEAD_DIM
RET_QK_WIDTH = RET_HEADS * RET_QK_DIM
RET_WIDTH = RET_HEADS * RET_V_DIM
MIX_WIDTH = ATT_WIDTH + RET_WIDTH
IN_WIDTH = 2 * ATT_WIDTH + 2 * ATT_KV_WIDTH + 2 * RET_QK_WIDTH + 2 * RET_WIDTH

OFF_AQ = 0
OFF_AK = OFF_AQ + ATT_WIDTH
OFF_AV = OFF_AK + ATT_KV_WIDTH
OFF_AZ = OFF_AV + ATT_KV_WIDTH
OFF_RQ = OFF_AZ + ATT_WIDTH
OFF_RK = OFF_RQ + RET_QK_WIDTH
OFF_RV = OFF_RK + RET_QK_WIDTH
OFF_RZ = OFF_RV + RET_WIDTH

LANES = 128
MXU_COLS = 256
N_MXU = 2
PROJ_CHUNK = N_MXU * MXU_COLS
HEADS_PER_TILE = LANES // ATT_HEAD_DIM
ATT_PAIRS = ATT_HEADS // HEADS_PER_TILE
RET_PAIRS = RET_HEADS // HEADS_PER_TILE
GROUP = ATT_HEADS // ATT_KV_HEADS

SEQ_TILE = 1024
SUB_TILE = 256
N_SUB = SEQ_TILE // SUB_TILE
BLKS_PER_SUB = SUB_TILE // BLK
IN_CHUNKS = -(-IN_WIDTH // PROJ_CHUNK)
OUT_CHUNKS = D_MODEL // PROJ_CHUNK
WEIGHT_SLAB = 256
CAST_STEPS = D_MODEL // WEIGHT_SLAB
FINISH_SLOT = 6
VMEM_LIMIT_BYTES = 56 * 1024 * 1024

F32 = jnp.float32
BF16 = jnp.bfloat16


def _silu(z):
    hz = 0.5 * z
    return hz + hz * jnp.tanh(hz)


def _interleave(units, fillers):
    done = 0
    for k, unit in enumerate(units):
        unit()
        due = -(-(k + 1) * len(fillers) // len(units))
        while done < due:
            fillers[done]()
            done += 1
    for f in fillers[done:]:
        f()


def _layer_kernel(tiles_per_seq, sinks_ref, x_ref, xnext_ref, ng_ref, win_slab_ref, gng_ref, wout_slab_ref, fg_ref,
                  cos_ref, sin_ref, din_ref, qdec_ref, kdec_ref, cdec_ref,
                  o_ref, state_ref, kprev_ref, vprev_ref, pfirst_ref, win_ref, wout_ref, *mix_refs):
    step = pl.program_id(0)

    @pl.when(step < CAST_STEPS)
    def _():
        rows = pl.ds(pl.multiple_of(step * WEIGHT_SLAB, WEIGHT_SLAB), WEIGHT_SLAB)
        win_ref[rows, :] = win_slab_ref[...].astype(BF16)
        wout_ref[rows, :] = wout_slab_ref[...].astype(BF16)

    @pl.when(step >= CAST_STEPS)
    def _():
        _tile_step(step - CAST_STEPS, tiles_per_seq, sinks_ref, x_ref, xnext_ref, ng_ref, win_ref, gng_ref, wout_ref,
                   fg_ref, cos_ref, sin_ref, din_ref, qdec_ref, kdec_ref, cdec_ref,
                   o_ref, state_ref, kprev_ref, vprev_ref, pfirst_ref, mix_refs)


def _tile_step(t, tiles_per_seq, sinks_ref, x_ref, xnext_ref, ng_ref, win_ref, gng_ref, wout_ref, fg_ref,
               cos_ref, sin_ref, din_ref, qdec_ref, kdec_ref, cdec_ref,
               o_ref, state_ref, kprev_ref, vprev_ref, pfirst_ref, mix_refs):
    j = lax.rem(t, tiles_per_seq)

    def project_first_subtile(src_ref):
        held = {}

        def chunk(c):
            def run():
                if not held:
                    xs = src_ref[0]
                    ms = jnp.mean(xs * xs, axis=-1, keepdims=True)
                    held["h"] = (xs * lax.rsqrt(ms + RMS_EPS) * ng_ref[...]).astype(BF16)
                c0, c1 = c * PROJ_CHUNK, min((c + 1) * PROJ_CHUNK, IN_WIDTH)
                pfirst_ref[:, c0:c1] = jnp.dot(held["h"], win_ref[:, c0:c1], preferred_element_type=F32)
            return run
        return [chunk(c) for c in range(IN_CHUNKS)]

    @pl.when(t == 0)
    def _():
        for f in project_first_subtile(x_ref.at[:, 0:SUB_TILE, :]):
            f()

    @pl.when(j == 0)
    def _():
        state_ref[...] = jnp.zeros_like(state_ref)
        kprev_ref[...] = jnp.zeros_like(kprev_ref)
        vprev_ref[...] = jnp.zeros_like(vprev_ref)

    lane = lax.broadcasted_iota(jnp.int32, (BLK, LANES), 1)
    lo_half = lane < ATT_HEAD_DIM
    hi_half = jnp.logical_not(lo_half)
    even_lane = (lane & 1) == 0
    lane2 = lax.broadcasted_iota(jnp.int32, (2 * BLK, LANES), 1)
    lo_half2 = lane2 < ATT_HEAD_DIM
    qi = lax.broadcasted_iota(jnp.int32, (BLK, 2 * BLK), 0)
    kj = lax.broadcasted_iota(jnp.int32, (BLK, 2 * BLK), 1)
    diff = qi + BLK - kj
    band = (diff >= 0) & (diff < WINDOW)
    in_cur = kj >= BLK
    ones_lo = jnp.where(lo_half2, 1.0, 0.0).astype(BF16)
    ones_hi = jnp.where(lo_half2, 0.0, 1.0).astype(BF16)
    zeros_bf = jnp.zeros((BLK, LANES), BF16)
    row_lo = lax.broadcasted_iota(jnp.int32, (LANES, LANES), 0) < RET_QK_DIM

    proj_chunks = [dict() for _ in range(N_SUB)]
    out_chunks = [dict() for _ in range(N_SUB)]
    normed = {}
    carry = {}
    att_rhs = {}

    def proj_tile(s, i, col):
        if s == 0:
            return pfirst_ref[i * BLK:(i + 1) * BLK, col:col + LANES]
        chunk, off = divmod(col, PROJ_CHUNK)
        return proj_chunks[s][chunk][i * BLK:(i + 1) * BLK, off:off + LANES]

    def in_proj_chunk(s, c):
        def run():
            if s not in normed:
                xs = x_ref[0, s * SUB_TILE:(s + 1) * SUB_TILE, :]
                ms = jnp.mean(xs * xs, axis=-1, keepdims=True)
                normed[s] = (xs * lax.rsqrt(ms + RMS_EPS) * ng_ref[...]).astype(BF16)
            proj_chunks[s][c] = jnp.dot(normed[s], win_ref[:, c * PROJ_CHUNK:min((c + 1) * PROJ_CHUNK, IN_WIDTH)],
                                        preferred_element_type=F32)
        return run

    def out_proj_chunk(s, c):
        def run():
            out_chunks[s][c] = jnp.dot(mix_refs[s][...], wout_ref[:, c * PROJ_CHUNK:(c + 1) * PROJ_CHUNK],
                                       preferred_element_type=F32)
        return run

    def finish(s):
        def run():
            rows = slice(s * SUB_TILE, (s + 1) * SUB_TILE)
            y = x_ref[0, rows, :] + jnp.concatenate([out_chunks[s][c] for c in range(OUT_CHUNKS)], axis=1)
            ms = jnp.mean(y * y, axis=-1, keepdims=True)
            o_ref[0, rows, :] = y * lax.rsqrt(ms + RMS_EPS) * fg_ref[...]
        return run

    def kv_prep(s, i):
        if s == 0 and i == 0:
            carry["k"] = [kprev_ref[g] for g in range(ATT_KV_HEADS)]
            carry["v"] = [vprev_ref[g] for g in range(ATT_KV_HEADS)]
        k_cur = proj_tile(s, i, OFF_AK)
        v_cur = proj_tile(s, i, OFF_AV)
        v_sw = pltpu.roll(v_cur, ATT_HEAD_DIM, 1)
        k_t = k_cur.T
        kdup_cur = [jnp.concatenate([k_t[g * ATT_HEAD_DIM:(g + 1) * ATT_HEAD_DIM]] * HEADS_PER_TILE,
                                    axis=0).astype(BF16) for g in range(ATT_KV_HEADS)]
        vdup_cur = [jnp.where(lo_half, v_cur, v_sw).astype(BF16),
                    jnp.where(lo_half, v_sw, v_cur).astype(BF16)]
        per_head = []
        for g in range(ATT_KV_HEADS):
            kdup = jnp.concatenate([carry["k"][g], kdup_cur[g]], axis=1)
            vdup = jnp.concatenate([carry["v"][g], vdup_cur[g]], axis=0)
            zero2 = jnp.zeros_like(vdup)
            rhs_v = jnp.concatenate([
                jnp.concatenate([jnp.where(lo_half2, vdup, zero2), ones_lo], axis=1),
                jnp.concatenate([jnp.where(lo_half2, zero2, vdup), ones_hi], axis=1),
            ], axis=0)
            per_head.append((kdup, rhs_v))
        att_rhs[(s, i)] = per_head
        carry["k"], carry["v"] = kdup_cur, vdup_cur

    stash = {}

    def att_scores(s, i):
        def run():
            kv_prep(s, i)
            valid = band & (in_cur | (j > 0)) if (s == 0 and i == 0) else band
            scores = []
            for p in range(ATT_PAIRS):
                kdup, _ = att_rhs[(s, i)][p // (GROUP // HEADS_PER_TILE)]
                q2 = (proj_tile(s, i, OFF_AQ + p * LANES) * (ATT_HEAD_DIM ** -0.5 * LOG2E)).astype(BF16)
                for e in range(HEADS_PER_TILE):
                    qm = jnp.where(lo_half if e == 0 else hi_half, q2, zeros_bf)
                    sc = jnp.dot(qm, kdup, preferred_element_type=F32)
                    scores.append(jnp.where(valid, sc, NEG_INF))
            stash[("sc", s, i)] = scores
        return run

    def att_softmax(s, i):
        def run():
            probs, sink_terms = [], []
            for head, sc in enumerate(stash.pop(("sc", s, i))):
                sink = sinks_ref[head] * LOG2E
                m = jnp.maximum(jnp.max(sc, axis=-1, keepdims=True), sink)
                probs.append(jnp.exp2(sc - m).astype(BF16))
                sink_terms.append(jnp.exp2(sink - m))
            stash[("p", s, i)] = (probs, sink_terms)
        return run

    def att_values(s, i):
        def run():
            probs, sink_terms = stash.pop(("p", s, i))
            for p in range(ATT_PAIRS):
                _, rhs_v = att_rhs[(s, i)][p // (GROUP // HEADS_PER_TILE)]
                h0 = p * HEADS_PER_TILE
                lhs = jnp.concatenate(probs[h0:h0 + HEADS_PER_TILE], axis=1)
                res = jnp.dot(lhs, rhs_v, preferred_element_type=F32)
                den = res[:, LANES:] + jnp.where(lo_half, sink_terms[h0], sink_terms[h0 + 1])
                a2 = res[:, :LANES] / den
                z2 = proj_tile(s, i, OFF_AZ + p * LANES)
                mix_refs[s][i * BLK:(i + 1) * BLK, p * LANES:(p + 1) * LANES] = (a2 * _silu(z2)).astype(BF16)
        return run

    def ret_scores(s, i):
        def run():
            r0 = (s * BLKS_PER_SUB + i) * BLK
            cos = cos_ref[r0:r0 + BLK, :]
            sin = sin_ref[r0:r0 + BLK, :]

            def rotate(t):
                swapped = jnp.where(even_lane, pltpu.roll(t, LANES - 1, 1), pltpu.roll(t, 1, 1))
                return t * cos + swapped * sin

            per_pair = []
            for p in range(RET_PAIRS):
                q2 = rotate(proj_tile(s, i, OFF_RQ + p * LANES))
                k2 = rotate(proj_tile(s, i, OFF_RK + p * LANES)) * (RET_QK_DIM ** -0.5)
                q2b = q2.astype(BF16)
                qd2b = (q2 * qdec_ref[p]).astype(BF16)
                k2t = k2.T
                k2tb = k2t.astype(BF16)
                kd2tb = (k2t * kdec_ref[p]).astype(BF16)
                rhs_qk = jnp.concatenate([jnp.where(row_lo, k2tb, zeros_bf),
                                          jnp.where(row_lo, zeros_bf, k2tb)], axis=1)
                sc2 = jnp.dot(q2b, rhs_qk, preferred_element_type=F32)
                lhs = []
                for e in range(HEADS_PER_TILE):
                    sel = lo_half if e == 0 else hi_half
                    sc = sc2[:, e * BLK:(e + 1) * BLK] * din_ref[p * HEADS_PER_TILE + e]
                    lhs.append(jnp.concatenate([sc.astype(BF16), jnp.where(sel, qd2b, zeros_bf)], axis=1))
                per_pair.append((lhs, kd2tb))
            stash[("ret", s, i)] = per_pair
        return run

    def ret_outputs(s, i):
        def run():
            per_pair = stash.pop(("ret", s, i))
            for p in range(RET_PAIRS):
                lhs, kd2tb = per_pair[p]
                st = state_ref[p * LANES:(p + 1) * LANES, :]
                st_b = st.astype(BF16)
                v_heads = [proj_tile(s, i, OFF_RV + (p * HEADS_PER_TILE + e) * LANES).astype(BF16)
                           for e in range(HEADS_PER_TILE)]
                for e in range(HEADS_PER_TILE):
                    head = p * HEADS_PER_TILE + e
                    rhs = jnp.concatenate([v_heads[e], st_b], axis=0)
                    o = jnp.dot(lhs[e], rhs, preferred_element_type=F32)
                    mu = jnp.mean(o, axis=-1, keepdims=True)
                    oc = o - mu
                    var = jnp.mean(oc * oc, axis=-1, keepdims=True)
                    on = oc * lax.rsqrt(var + GN_EPS) * gng_ref[:, head * LANES:(head + 1) * LANES]
                    rz = proj_tile(s, i, OFF_RZ + head * LANES)
                    mc = ATT_WIDTH + head * LANES
                    mix_refs[s][i * BLK:(i + 1) * BLK, mc:mc + LANES] = (on * _silu(rz)).astype(BF16)
                kv = jnp.dot(kd2tb, jnp.concatenate(v_heads, axis=1),
                             preferred_element_type=F32)
                kv_sel = jnp.where(row_lo, kv[:, :LANES], kv[:, LANES:])
                state_ref[p * LANES:(p + 1) * LANES, :] = st * cdec_ref[p * LANES:(p + 1) * LANES, :] + kv_sel
        return run

    def mixer_units(s):
        units = []
        for stage in (att_scores, ret_scores, att_softmax, ret_outputs, att_values):
            units += [stage(s, i) for i in range(BLKS_PER_SUB)]
        return units

    for s in range(N_SUB):
        fillers = []
        if s >= 1:
            fillers += [out_proj_chunk(s - 1, c) for c in range(OUT_CHUNKS)] + [finish(s - 1)]
        if s + 1 < N_SUB:
            fillers += [in_proj_chunk(s + 1, c) for c in range(IN_CHUNKS)]
        else:
            fillers += project_first_subtile(xnext_ref)
        if s == 0:
            fillers.pop(0)()
        else:
            fillers.insert(FINISH_SLOT, fillers.pop(OUT_CHUNKS))
        _interleave(mixer_units(s), fillers)
    for c in range(OUT_CHUNKS):
        out_proj_chunk(N_SUB - 1, c)()
    finish(N_SUB - 1)()

    for g in range(ATT_KV_HEADS):
        kprev_ref[g] = carry["k"][g]
        vprev_ref[g] = carry["v"][g]


def _retention_tables(seq):
    pos = np.arange(seq, dtype=np.float64)
    theta = 1.0 / (ROT_BASE ** np.linspace(0.0, 1.0, RET_QK_DIM // 2))
    ang = pos[:, None] * theta[None, :]
    cos, sin = np.cos(ang), np.sin(ang)
    cos_l = np.tile(np.repeat(cos, 2, axis=1), (1, HEADS_PER_TILE))
    sign = np.tile(np.array([-1.0, 1.0]), RET_QK_DIM // 2)
    sin_l = np.tile(np.repeat(sin, 2, axis=1) * sign[None, :], (1, HEADS_PER_TILE))

    log_gamma = np.log(1.0 - 2.0 ** (-5.0 - np.arange(RET_HEADS, dtype=np.float64)))
    idx = np.arange(BLK, dtype=np.float64)
    rel = idx[:, None] - idx[None, :]
    decay_in = np.where(rel >= 0, np.exp(log_gamma[:, None, None] * np.maximum(rel, 0.0)), 0.0)
    k_dec = np.exp(log_gamma[:, None] * (BLK - 1 - idx)[None, :])
    q_dec = np.exp(log_gamma[:, None] * (idx + 1)[None, :])
    chunk_decay = np.exp(log_gamma * BLK)

    def per_pair(dec):
        t = np.repeat(dec.T[:, :, None], RET_QK_DIM, axis=2)
        t = t.reshape(BLK, RET_PAIRS, LANES)
        return np.transpose(t, (1, 0, 2))

    cdec = np.broadcast_to(np.repeat(chunk_decay, RET_QK_DIM)[:, None],
                           (RET_QK_WIDTH, RET_V_DIM))
    k_dec_t = np.transpose(per_pair(k_dec), (0, 2, 1))
    tables = (cos_l, sin_l, decay_in, per_pair(q_dec), k_dec_t, cdec)
    return tuple(jnp.asarray(np.ascontiguousarray(a, dtype=np.float32)) for a in tables)


def _full(shape):
    return pl.BlockSpec(shape, lambda step: (0,) * len(shape))


@jax.jit
def kernel(x, norm_g, w_in, att_sinks, ret_gn_g, w_out, final_g):
    batch, seq, d = x.shape
    depth = w_in.shape[0]
    assert depth == 1 and d == D_MODEL and seq % SEQ_TILE == 0
    cos_l, sin_l, decay_in, qdec, kdec, cdec = _retention_tables(seq)
    tile = SEQ_TILE
    n_seq = seq // tile
    n_tiles = batch * n_seq

    def tile_of(step):
        return jnp.maximum(step - CAST_STEPS, 0)

    def this_tile(step):
        t = tile_of(step)
        return (t // n_seq, t % n_seq, 0)

    def next_first_subtile(step):
        u = jnp.minimum(tile_of(step) + 1, n_tiles - 1)
        return (u // n_seq, (u % n_seq) * N_SUB, 0)

    def weight_slab(step):
        return (jnp.minimum(step, CAST_STEPS - 1), 0)

    call = pl.pallas_call(
        functools.partial(_layer_kernel, n_seq),
        grid=(CAST_STEPS + n_tiles,),
        in_specs=[
            pl.BlockSpec(memory_space=pltpu.SMEM),
            pl.BlockSpec((1, tile, D_MODEL), this_tile),
            pl.BlockSpec((1, SUB_TILE, D_MODEL), next_first_subtile),
            _full((1, D_MODEL)),
            pl.BlockSpec((WEIGHT_SLAB, IN_WIDTH), weight_slab),
            _full((1, RET_WIDTH)),
            pl.BlockSpec((WEIGHT_SLAB, D_MODEL), weight_slab),
            _full((1, D_MODEL)),
            pl.BlockSpec((tile, LANES), lambda step: (tile_of(step) % n_seq, 0)),
            pl.BlockSpec((tile, LANES), lambda step: (tile_of(step) % n_seq, 0)),
            _full((RET_HEADS, BLK, BLK)),
            _full((RET_PAIRS, BLK, LANES)),
            _full((RET_PAIRS, BLK, LANES)),
            _full((RET_QK_WIDTH, RET_V_DIM)),
        ],
        out_specs=pl.BlockSpec((1, tile, D_MODEL), this_tile),
        out_shape=jax.ShapeDtypeStruct(x.shape, x.dtype),
        scratch_shapes=[
            pltpu.VMEM((RET_QK_WIDTH, RET_V_DIM), F32),
            pltpu.VMEM((ATT_KV_HEADS, BLK, LANES), BF16),
            pltpu.VMEM((ATT_KV_HEADS, BLK, LANES), BF16),
            pltpu.VMEM((SUB_TILE, IN_WIDTH), F32),
            pltpu.VMEM((D_MODEL, IN_WIDTH), BF16),
            pltpu.VMEM((MIX_WIDTH, D_MODEL), BF16),
        ] + [pltpu.VMEM((SUB_TILE, MIX_WIDTH), BF16)] * N_SUB,
        compiler_params=pltpu.CompilerParams(
            dimension_semantics=("arbitrary",),
            vmem_limit_bytes=VMEM_LIMIT_BYTES),
        name="hymba_layer",
    )
    return call(att_sinks[0], x, x, norm_g[0][None, :], w_in[0],
                ret_gn_g[0][None, :], w_out[0], final_g[None, :],
                cos_l, sin_l, decay_in, qdec, kdec, cdec)
```

```python
import functools

import jax
import jax.numpy as jnp
import numpy as np
from jax import lax
from jax.experimental import pallas as pl
from jax.experimental.pallas import tpu as pltpu

D_MODEL = 1024
ATT_HEADS = 8
ATT_KV_HEADS = 2
ATT_HEAD_DIM = 64
WINDOW = 128
BLK = 128
RET_HEADS = 4
RET_QK_DIM = 64
RET_V_DIM = 128
ROT_BASE = 10000.0
RMS_EPS = 1e-6
GN_EPS = 1e-6
NEG_INF = -1e30
LOG2E = 1.4426950408889634

ATT_WIDTH = ATT_HEADS * ATT_HEAD_DIM
ATT_KV_WIDTH = ATT_KV_HEADS * ATT_HEAD_DIM
RET_QK_WIDTH = RET_HEADS * RET_QK_DIM
RET_WIDTH = RET_HEADS * RET_V_DIM
MIX_WIDTH = ATT_WIDTH + RET_WIDTH
IN_WIDTH = 2 * ATT_WIDTH + 2 * ATT_KV_WIDTH + 2 * RET_QK_WIDTH + 2 * RET_WIDTH

OFF_AQ = 0
OFF_AK = OFF_AQ + ATT_WIDTH
OFF_AV = OFF_AK + ATT_KV_WIDTH
OFF_AZ = OFF_AV + ATT_KV_WIDTH
OFF_RQ = OFF_AZ + ATT_WIDTH
OFF_RK = OFF_RQ + RET_QK_WIDTH
OFF_RV = OFF_RK + RET_QK_WIDTH
OFF_RZ = OFF_RV + RET_WIDTH

LANES = 128
MXU_COLS = 256
N_MXU = 2
PROJ_CHUNK = N_MXU * MXU_COLS
HEADS_PER_TILE = LANES // ATT_HEAD_DIM
ATT_PAIRS = ATT_HEADS // HEADS_PER_TILE
RET_PAIRS = RET_HEADS // HEADS_PER_TILE
GROUP = ATT_HEADS // ATT_KV_HEADS

SEQ_TILE = 1024
SUB_TILE = 256
N_SUB = SEQ_TILE // SUB_TILE
BLKS_PER_SUB = SUB_TILE // BLK
IN_CHUNKS = -(-IN_WIDTH // PROJ_CHUNK)
OUT_CHUNKS = D_MODEL // PROJ_CHUNK
WEIGHT_SLAB = 256
CAST_STEPS = D_MODEL // WEIGHT_SLAB
VMEM_LIMIT_BYTES = 56 * 1024 * 1024

F32 = jnp.float32
BF16 = jnp.bfloat16


def _silu(z):
    hz = 0.5 * z
    return hz + hz * jnp.tanh(hz)


def _interleave(units, fillers):
    done = 0
    for k, unit in enumerate(units):
        unit()
        due = -(-(k + 1) * len(fillers) // len(units))
        while done < due:
            fillers[done]()
            done += 1
    for f in fillers[done:]:
        f()


def _layer_kernel(tiles_per_seq, sinks_ref, x_ref, xnext_ref, ng_ref, win_slab_ref, gng_ref, wout_slab_ref, fg_ref,
                  cos_ref, sin_ref, din_ref, qdec_ref, kdec_ref, cdec_ref,
                  o_ref, state_ref, kprev_ref, vprev_ref, pfirst_ref, hsecond_ref, win_ref, wout_ref, *mix_refs):
    step = pl.program_id(0)

    @pl.when(step < CAST_STEPS)
    def _():
        rows = pl.ds(pl.multiple_of(step * WEIGHT_SLAB, WEIGHT_SLAB), WEIGHT_SLAB)
        win_ref[rows, :] = win_slab_ref[...].astype(BF16)
        wout_ref[rows, :] = wout_slab_ref[...].astype(BF16)

    @pl.when(step >= CAST_STEPS)
    def _():
        _tile_step(step - CAST_STEPS, tiles_per_seq, sinks_ref, x_ref, xnext_ref, ng_ref, win_ref, gng_ref, wout_ref,
                   fg_ref, cos_ref, sin_ref, din_ref, qdec_ref, kdec_ref, cdec_ref,
                   o_ref, state_ref, kprev_ref, vprev_ref, pfirst_ref, hsecond_ref, mix_refs)


def _tile_step(t, tiles_per_seq, sinks_ref, x_ref, xnext_ref, ng_ref, win_ref, gng_ref, wout_ref, fg_ref,
               cos_ref, sin_ref, din_ref, qdec_ref, kdec_ref, cdec_ref,
               o_ref, state_ref, kprev_ref, vprev_ref, pfirst_ref, hsecond_ref, mix_refs):
    j = lax.rem(t, tiles_per_seq)

    def normalise(xs):
        ms = jnp.mean(xs * xs, axis=-1, keepdims=True)
        return (xs * lax.rsqrt(ms + RMS_EPS) * ng_ref[...]).astype(BF16)

    def project_first_subtile(src_ref):
        held = {}

        def chunk(c):
            def run():
                if not held:
                    held["h"] = normalise(src_ref[0, 0:SUB_TILE, :])
                c0, c1 = c * PROJ_CHUNK, min((c + 1) * PROJ_CHUNK, IN_WIDTH)
                pfirst_ref[:, c0:c1] = jnp.dot(held["h"], win_ref[:, c0:c1], preferred_element_type=F32)
            return run

        def second():
            hsecond_ref[...] = normalise(src_ref[0, SUB_TILE:2 * SUB_TILE, :])
        return [chunk(c) for c in range(IN_CHUNKS)] + [second]

    @pl.when(t == 0)
    def _():
        for f in project_first_subtile(x_ref):
            f()

    @pl.when(j == 0)
    def _():
        state_ref[...] = jnp.zeros_like(state_ref)
        kprev_ref[...] = jnp.zeros_like(kprev_ref)
        vprev_ref[...] = jnp.zeros_like(vprev_ref)

    lane = lax.broadcasted_iota(jnp.int32, (BLK, LANES), 1)
    lo_half = lane < ATT_HEAD_DIM
    hi_half = jnp.logical_not(lo_half)
    even_lane = (lane & 1) == 0
    lane2 = lax.broadcasted_iota(jnp.int32, (2 * BLK, LANES), 1)
    lo_half2 = lane2 < ATT_HEAD_DIM
    qi = lax.broadcasted_iota(jnp.int32, (BLK, 2 * BLK), 0)
    kj = lax.broadcasted_iota(jnp.int32, (BLK, 2 * BLK), 1)
    diff = qi + BLK - kj
    band = (diff >= 0) & (diff < WINDOW)
    in_cur = kj >= BLK
    ones_lo = jnp.where(lo_half2, 1.0, 0.0).astype(BF16)
    ones_hi = jnp.where(lo_half2, 0.0, 1.0).astype(BF16)
    zeros_bf = jnp.zeros((BLK, LANES), BF16)
    row_lo = lax.broadcasted_iota(jnp.int32, (LANES, LANES), 0) < RET_QK_DIM

    proj_chunks = [dict() for _ in range(N_SUB)]
    out_chunks = [dict() for _ in range(N_SUB)]
    normed = {}
    carry = {}
    att_rhs = {}

    def proj_tile(s, i, col):
        if s == 0:
            return pfirst_ref[i * BLK:(i + 1) * BLK, col:col + LANES]
        chunk, off = divmod(col, PROJ_CHUNK)
        return proj_chunks[s][chunk][i * BLK:(i + 1) * BLK, off:off + LANES]

    def in_proj_chunk(s, c):
        def run():
            if s not in normed:
                normed[s] = hsecond_ref[...] if s == 1 else normalise(x_ref[0, s * SUB_TILE:(s + 1) * SUB_TILE, :])
            proj_chunks[s][c] = jnp.dot(normed[s], win_ref[:, c * PROJ_CHUNK:min((c + 1) * PROJ_CHUNK, IN_WIDTH)],
                                        preferred_element_type=F32)
        return run

    def out_proj_chunk(s, c):
        def run():
            out_chunks[s][c] = jnp.dot(mix_refs[s][...], wout_ref[:, c * PROJ_CHUNK:(c + 1) * PROJ_CHUNK],
                                       preferred_element_type=F32)
        return run

    def finish(s):
        def run():
            rows = slice(s * SUB_TILE, (s + 1) * SUB_TILE)
            y = x_ref[0, rows, :] + jnp.concatenate([out_chunks[s][c] for c in range(OUT_CHUNKS)], axis=1)
            ms = jnp.mean(y * y, axis=-1, keepdims=True)
            o_ref[0, rows, :] = y * lax.rsqrt(ms + RMS_EPS) * fg_ref[...]
        return run

    def kv_prep(s, i):
        if s == 0 and i == 0:
            carry["k"] = [kprev_ref[g] for g in range(ATT_KV_HEADS)]
            carry["v"] = [vprev_ref[g] for g in range(ATT_KV_HEADS)]
        k_cur = proj_tile(s, i, OFF_AK)
        v_cur = proj_tile(s, i, OFF_AV)
        v_sw = pltpu.roll(v_cur, ATT_HEAD_DIM, 1)
        k_t = k_cur.T
        kdup_cur = [jnp.concatenate([k_t[g * ATT_HEAD_DIM:(g + 1) * ATT_HEAD_DIM]] * HEADS_PER_TILE,
                                    axis=0).astype(BF16) for g in range(ATT_KV_HEADS)]
        vdup_cur = [jnp.where(lo_half, v_cur, v_sw).astype(BF16),
                    jnp.where(lo_half, v_sw, v_cur).astype(BF16)]
        per_head = []
        for g in range(ATT_KV_HEADS):
            kdup = jnp.concatenate([carry["k"][g], kdup_cur[g]], axis=1)
            vdup = jnp.concatenate([carry["v"][g], vdup_cur[g]], axis=0)
            zero2 = jnp.zeros_like(vdup)
            rhs_v = jnp.concatenate([
                jnp.concatenate([jnp.where(lo_half2, vdup, zero2), ones_lo], axis=1),
                jnp.concatenate([jnp.where(lo_half2, zero2, vdup), ones_hi], axis=1),
            ], axis=0)
            per_head.append((kdup, rhs_v))
        att_rhs[(s, i)] = per_head
        carry["k"], carry["v"] = kdup_cur, vdup_cur

    stash = {}

    def att_scores(s, i):
        def run():
            kv_prep(s, i)
            valid = band & (in_cur | (j > 0)) if (s == 0 and i == 0) else band
            scores = []
            for p in range(ATT_PAIRS):
                kdup, _ = att_rhs[(s, i)][p // (GROUP // HEADS_PER_TILE)]
                q2 = (proj_tile(s, i, OFF_AQ + p * LANES) * (ATT_HEAD_DIM ** -0.5 * LOG2E)).astype(BF16)
                for e in range(HEADS_PER_TILE):
                    qm = jnp.where(lo_half if e == 0 else hi_half, q2, zeros_bf)
                    sc = jnp.dot(qm, kdup, preferred_element_type=F32)
                    scores.append(jnp.where(valid, sc, NEG_INF))
            stash[("sc", s, i)] = scores
        return run

    def att_softmax(s, i):
        def run():
            probs, sink_terms = [], []
            for head, sc in enumerate(stash.pop(("sc", s, i))):
                sink = sinks_ref[head] * LOG2E
                m = jnp.maximum(jnp.max(sc, axis=-1, keepdims=True), sink)
                probs.append(jnp.exp2(sc - m).astype(BF16))
                sink_terms.append(jnp.exp2(sink - m))
            stash[("p", s, i)] = (probs, sink_terms)
        return run

    def att_values(s, i):
        def run():
            probs, sink_terms = stash.pop(("p", s, i))
            for p in range(ATT_PAIRS):
                _, rhs_v = att_rhs[(s, i)][p // (GROUP // HEADS_PER_TILE)]
                h0 = p * HEADS_PER_TILE
                lhs = jnp.concatenate(probs[h0:h0 + HEADS_PER_TILE], axis=1)
                res = jnp.dot(lhs, rhs_v, preferred_element_type=F32)
                den = res[:, LANES:] + jnp.where(lo_half, sink_terms[h0], sink_terms[h0 + 1])
                a2 = res[:, :LANES] / den
                z2 = proj_tile(s, i, OFF_AZ + p * LANES)
                mix_refs[s][i * BLK:(i + 1) * BLK, p * LANES:(p + 1) * LANES] = (a2 * _silu(z2)).astype(BF16)
        return run

    def ret_scores(s, i):
        def run():
            r0 = (s * BLKS_PER_SUB + i) * BLK
            cos = cos_ref[r0:r0 + BLK, :]
            sin = sin_ref[r0:r0 + BLK, :]

            def rotate(t):
                swapped = jnp.where(even_lane, pltpu.roll(t, LANES - 1, 1), pltpu.roll(t, 1, 1))
                return t * cos + swapped * sin

            per_pair = []
            for p in range(RET_PAIRS):
                q2 = rotate(proj_tile(s, i, OFF_RQ + p * LANES))
                k2 = rotate(proj_tile(s, i, OFF_RK + p * LANES)) * (RET_QK_DIM ** -0.5)
                q2b = q2.astype(BF16)
                qd2b = (q2 * qdec_ref[p]).astype(BF16)
                k2t = k2.T
                k2tb = k2t.astype(BF16)
                kd2tb = (k2t * kdec_ref[p]).astype(BF16)
                rhs_qk = jnp.concatenate([jnp.where(row_lo, k2tb, zeros_bf),
                                          jnp.where(row_lo, zeros_bf, k2tb)], axis=1)
                sc2 = jnp.dot(q2b, rhs_qk, preferred_element_type=F32)
                lhs = []
                for e in range(HEADS_PER_TILE):
                    sel = lo_half if e == 0 else hi_half
                    sc = sc2[:, e * BLK:(e + 1) * BLK] * din_ref[p * HEADS_PER_TILE + e]
                    lhs.append(jnp.concatenate([sc.astype(BF16), jnp.where(sel, qd2b, zeros_bf)], axis=1))
                per_pair.append((lhs, kd2tb))
            stash[("ret", s, i)] = per_pair
        return run

    def ret_outputs(s, i):
        def run():
            per_pair = stash.pop(("ret", s, i))
            for p in range(RET_PAIRS):
                lhs, kd2tb = per_pair[p]
                st = state_ref[p * LANES:(p + 1) * LANES, :]
                st_b = st.astype(BF16)
                v_heads = [proj_tile(s, i, OFF_RV + (p * HEADS_PER_TILE + e) * LANES).astype(BF16)
                           for e in range(HEADS_PER_TILE)]
                for e in range(HEADS_PER_TILE):
                    head = p * HEADS_PER_TILE + e
                    rhs = jnp.concatenate([v_heads[e], st_b], axis=0)
                    o = jnp.dot(lhs[e], rhs, preferred_element_type=F32)
                    mu = jnp.mean(o, axis=-1, keepdims=True)
                    oc = o - mu
                    var = jnp.mean(oc * oc, axis=-1, keepdims=True)
                    on = oc * lax.rsqrt(var + GN_EPS) * gng_ref[:, head * LANES:(head + 1) * LANES]
                    rz = proj_tile(s, i, OFF_RZ + head * LANES)
                    mc = ATT_WIDTH + head * LANES
                    mix_refs[s][i * BLK:(i + 1) * BLK, mc:mc + LANES] = (on * _silu(rz)).astype(BF16)
                kv = jnp.dot(kd2tb, jnp.concatenate(v_heads, axis=1),
                             preferred_element_type=F32)
                kv_sel = jnp.where(row_lo, kv[:, :LANES], kv[:, LANES:])
                state_ref[p * LANES:(p + 1) * LANES, :] = st * cdec_ref[p * LANES:(p + 1) * LANES, :] + kv_sel
        return run

    def mixer_units(s):
        units = []
        for stage in (att_scores, ret_scores, att_softmax, ret_outputs, att_values):
            units += [stage(s, i) for i in range(BLKS_PER_SUB)]
        return units

    for s in range(N_SUB):
        fillers = []
        if s >= 1:
            fillers += [out_proj_chunk(s - 1, c) for c in range(OUT_CHUNKS)] + [finish(s - 1)]
        if s + 1 < N_SUB:
            fillers += [in_proj_chunk(s + 1, c) for c in range(IN_CHUNKS)]
        else:
            fillers += project_first_subtile(xnext_ref)
        if s == 0:
            fillers.pop(0)()
        _interleave(mixer_units(s), fillers)
    for c in range(OUT_CHUNKS):
        out_proj_chunk(N_SUB - 1, c)()
    finish(N_SUB - 1)()

    for g in range(ATT_KV_HEADS):
        kprev_ref[g] = carry["k"][g]
        vprev_ref[g] = carry["v"][g]


def _retention_tables(seq):
    pos = np.arange(seq, dtype=np.float64)
    theta = 1.0 / (ROT_BASE ** np.linspace(0.0, 1.0, RET_QK_DIM // 2))
    ang = pos[:, None] * theta[None, :]
    cos, sin = np.cos(ang), np.sin(ang)
    cos_l = np.tile(np.repeat(cos, 2, axis=1), (1, HEADS_PER_TILE))
    sign = np.tile(np.array([-1.0, 1.0]), RET_QK_DIM // 2)
    sin_l = np.tile(np.repeat(sin, 2, axis=1) * sign[None, :], (1, HEADS_PER_TILE))

    log_gamma = np.log(1.0 - 2.0 ** (-5.0 - np.arange(RET_HEADS, dtype=np.float64)))
    idx = np.arange(BLK, dtype=np.float64)
    rel = idx[:, None] - idx[None, :]
    decay_in = np.where(rel >= 0, np.exp(log_gamma[:, None, None] * np.maximum(rel, 0.0)), 0.0)
    k_dec = np.exp(log_gamma[:, None] * (BLK - 1 - idx)[None, :])
    q_dec = np.exp(log_gamma[:, None] * (idx + 1)[None, :])
    chunk_decay = np.exp(log_gamma * BLK)

    def per_pair(dec):
        t = np.repeat(dec.T[:, :, None], RET_QK_DIM, axis=2)
        t = t.reshape(BLK, RET_PAIRS, LANES)
        return np.transpose(t, (1, 0, 2))

    cdec = np.broadcast_to(np.repeat(chunk_decay, RET_QK_DIM)[:, None],
                           (RET_QK_WIDTH, RET_V_DIM))
    k_dec_t = np.transpose(per_pair(k_dec), (0, 2, 1))
    tables = (cos_l, sin_l, decay_in, per_pair(q_dec), k_dec_t, cdec)
    return tuple(jnp.asarray(np.ascontiguousarray(a, dtype=np.float32)) for a in tables)


def _full(shape):
    return pl.BlockSpec(shape, lambda step: (0,) * len(shape))


@jax.jit
def kernel(x, norm_g, w_in, att_sinks, ret_gn_g, w_out, final_g):
    batch, seq, d = x.shape
    depth = w_in.shape[0]
    assert depth == 1 and d == D_MODEL and seq % SEQ_TILE == 0
    cos_l, sin_l, decay_in, qdec, kdec, cdec = _retention_tables(seq)
    tile = SEQ_TILE
    n_seq = seq // tile
    n_tiles = batch * n_seq

    def tile_of(step):
        return jnp.maximum(step - CAST_STEPS, 0)

    def this_tile(step):
        t = tile_of(step)
        return (t // n_seq, t % n_seq, 0)

    def next_head(step):
        u = jnp.minimum(tile_of(step) + 1, n_tiles - 1)
        return (u // n_seq, (u % n_seq) * (N_SUB // 2), 0)

    def weight_slab(step):
        return (jnp.minimum(step, CAST_STEPS - 1), 0)

    call = pl.pallas_call(
        functools.partial(_layer_kernel, n_seq),
        grid=(CAST_STEPS + n_tiles,),
        in_specs=[
            pl.BlockSpec(memory_space=pltpu.SMEM),
            pl.BlockSpec((1, tile, D_MODEL), this_tile),
            pl.BlockSpec((1, 2 * SUB_TILE, D_MODEL), next_head),
            _full((1, D_MODEL)),
            pl.BlockSpec((WEIGHT_SLAB, IN_WIDTH), weight_slab),
            _full((1, RET_WIDTH)),
            pl.BlockSpec((WEIGHT_SLAB, D_MODEL), weight_slab),
            _full((1, D_MODEL)),
            pl.BlockSpec((tile, LANES), lambda step: (tile_of(step) % n_seq, 0)),
            pl.BlockSpec((tile, LANES), lambda step: (tile_of(step) % n_seq, 0)),
            _full((RET_HEADS, BLK, BLK)),
            _full((RET_PAIRS, BLK, LANES)),
            _full((RET_PAIRS, BLK, LANES)),
            _full((RET_QK_WIDTH, RET_V_DIM)),
        ],
        out_specs=pl.BlockSpec((1, tile, D_MODEL), this_tile),
        out_shape=jax.ShapeDtypeStruct(x.shape, x.dtype),
        scratch_shapes=[
            pltpu.VMEM((RET_QK_WIDTH, RET_V_DIM), F32),
            pltpu.VMEM((ATT_KV_HEADS, BLK, LANES), BF16),
            pltpu.VMEM((ATT_KV_HEADS, BLK, LANES), BF16),
            pltpu.VMEM((SUB_TILE, IN_WIDTH), F32),
            pltpu.VMEM((SUB_TILE, D_MODEL), BF16),
            pltpu.VMEM((D_MODEL, IN_WIDTH), BF16),
            pltpu.VMEM((MIX_WIDTH, D_MODEL), BF16),
        ] + [pltpu.VMEM((SUB_TILE, MIX_WIDTH), BF16)] * N_SUB,
        compiler_params=pltpu.CompilerParams(
            dimension_semantics=("arbitrary",),
            vmem_limit_bytes=VMEM_LIMIT_BYTES),
        name="hymba_layer",
    )
    return call(att_sinks[0], x, x, norm_g[0][None, :], w_in[0],
                ret_gn_g[0][None, :], w_out[0], final_g[None, :],
                cos_l, sin_l, decay_in, qdec, kdec, cdec)
```

```python
import functools

import jax
import jax.numpy as jnp
import numpy as np
from jax import lax
from jax.experimental import pallas as pl
from jax.experimental.pallas import tpu as pltpu

D_MODEL = 1024
ATT_HEADS = 8
ATT_KV_HEADS = 2
ATT_HEAD_DIM = 64
WINDOW = 128
BLK = 128
RET_HEADS = 4
RET_QK_DIM = 64
RET_V_DIM = 128
ROT_BASE = 10000.0
RMS_EPS = 1e-6
GN_EPS = 1e-6
NEG_INF = -1e30
LOG2E = 1.4426950408889634

ATT_WIDTH = ATT_HEADS * ATT_HEAD_DIM
ATT_KV_WIDTH = ATT_KV_HEADS * ATT_HEAD_DIM
RET_QK_WIDTH = RET_HEADS * RET_QK_DIM
RET_WIDTH = RET_HEADS * RET_V_DIM
MIX_WIDTH = ATT_WIDTH + RET_WIDTH
IN_WIDTH = 2 * ATT_WIDTH + 2 * ATT_KV_WIDTH + 2 * RET_QK_WIDTH + 2 * RET_WIDTH

OFF_AQ = 0
OFF_AK = OFF_AQ + ATT_WIDTH
OFF_AV = OFF_AK + ATT_KV_WIDTH
OFF_AZ = OFF_AV + ATT_KV_WIDTH
OFF_RQ = OFF_AZ + ATT_WIDTH
OFF_RK = OFF_RQ + RET_QK_WIDTH
OFF_RV = OFF_RK + RET_QK_WIDTH
OFF_RZ = OFF_RV + RET_WIDTH

LANES = 128
MXU_COLS = 256
N_MXU = 2
PROJ_CHUNK = N_MXU * MXU_COLS
HEADS_PER_TILE = LANES // ATT_HEAD_DIM
ATT_PAIRS = ATT_HEADS // HEADS_PER_TILE
RET_PAIRS = RET_HEADS // HEADS_PER_TILE
GROUP = ATT_HEADS // ATT_KV_HEADS

SEQ_TILE = 1024
SUB_TILE = 256
N_SUB = SEQ_TILE // SUB_TILE
BLKS_PER_SUB = SUB_TILE // BLK
IN_CHUNKS = -(-IN_WIDTH // PROJ_CHUNK)
OUT_CHUNKS = D_MODEL // PROJ_CHUNK
WEIGHT_SLAB = 256
CAST_STEPS = D_MODEL // WEIGHT_SLAB
VMEM_LIMIT_BYTES = 56 * 1024 * 1024

F32 = jnp.float32
BF16 = jnp.bfloat16


def _silu(z):
    hz = 0.5 * z
    return hz + hz * jnp.tanh(hz)


def _project(h, w_ref, c0, c1):
    if c1 - c0 > MXU_COLS:
        return jnp.dot(h, w_ref[:, c0:c1], preferred_element_type=F32)
    half = h.shape[0] // N_MXU
    return jnp.concatenate([jnp.dot(h[r * half:(r + 1) * half], w_ref[:, c0:c1], preferred_element_type=F32)
                            for r in range(N_MXU)], axis=0)


def _interleave(units, fillers):
    done = 0
    for k, unit in enumerate(units):
        unit()
        due = -(-(k + 1) * len(fillers) // len(units))
        while done < due:
            fillers[done]()
            done += 1
    for f in fillers[done:]:
        f()


def _layer_kernel(tiles_per_seq, sinks_ref, x_ref, xnext_ref, ng_ref, win_slab_ref, gng_ref, wout_slab_ref, fg_ref,
                  cos_ref, sin_ref, din_ref, qdec_ref, kdec_ref, cdec_ref,
                  o_ref, state_ref, kprev_ref, vprev_ref, pfirst_ref, win_ref, wout_ref, *mix_refs):
    step = pl.program_id(0)

    @pl.when(step < CAST_STEPS)
    def _():
        rows = pl.ds(pl.multiple_of(step * WEIGHT_SLAB, WEIGHT_SLAB), WEIGHT_SLAB)
        win_ref[rows, :] = win_slab_ref[...].astype(BF16)
        wout_ref[rows, :] = wout_slab_ref[...].astype(BF16)

    @pl.when(step >= CAST_STEPS)
    def _():
        _tile_step(step - CAST_STEPS, tiles_per_seq, sinks_ref, x_ref, xnext_ref, ng_ref, win_ref, gng_ref, wout_ref,
                   fg_ref, cos_ref, sin_ref, din_ref, qdec_ref, kdec_ref, cdec_ref,
                   o_ref, state_ref, kprev_ref, vprev_ref, pfirst_ref, mix_refs)


def _tile_step(t, tiles_per_seq, sinks_ref, x_ref, xnext_ref, ng_ref, win_ref, gng_ref, wout_ref, fg_ref,
               cos_ref, sin_ref, din_ref, qdec_ref, kdec_ref, cdec_ref,
               o_ref, state_ref, kprev_ref, vprev_ref, pfirst_ref, mix_refs):
    j = lax.rem(t, tiles_per_seq)

    def project_first_subtile(src_ref):
        held = {}

        def chunk(c):
            def run():
                if not held:
                    xs = src_ref[0]
                    ms = jnp.mean(xs * xs, axis=-1, keepdims=True)
                    held["h"] = (xs * lax.rsqrt(ms + RMS_EPS) * ng_ref[...]).astype(BF16)
                c0, c1 = c * PROJ_CHUNK, min((c + 1) * PROJ_CHUNK, IN_WIDTH)
                pfirst_ref[:, c0:c1] = _project(held["h"], win_ref, c0, c1)
            return run
        return [chunk(c) for c in range(IN_CHUNKS)]

    @pl.when(t == 0)
    def _():
        for f in project_first_subtile(x_ref.at[:, 0:SUB_TILE, :]):
            f()

    @pl.when(j == 0)
    def _():
        state_ref[...] = jnp.zeros_like(state_ref)
        kprev_ref[...] = jnp.zeros_like(kprev_ref)
        vprev_ref[...] = jnp.zeros_like(vprev_ref)

    lane = lax.broadcasted_iota(jnp.int32, (BLK, LANES), 1)
    lo_half = lane < ATT_HEAD_DIM
    hi_half = jnp.logical_not(lo_half)
    even_lane = (lane & 1) == 0
    lane2 = lax.broadcasted_iota(jnp.int32, (2 * BLK, LANES), 1)
    lo_half2 = lane2 < ATT_HEAD_DIM
    qi = lax.broadcasted_iota(jnp.int32, (BLK, 2 * BLK), 0)
    kj = lax.broadcasted_iota(jnp.int32, (BLK, 2 * BLK), 1)
    diff = qi + BLK - kj
    band = (diff >= 0) & (diff < WINDOW)
    in_cur = kj >= BLK
    ones_lo = jnp.where(lo_half2, 1.0, 0.0).astype(BF16)
    ones_hi = jnp.where(lo_half2, 0.0, 1.0).astype(BF16)
    zeros_bf = jnp.zeros((BLK, LANES), BF16)
    row_lo = lax.broadcasted_iota(jnp.int32, (LANES, LANES), 0) < RET_QK_DIM

    proj_chunks = [dict() for _ in range(N_SUB)]
    out_chunks = [dict() for _ in range(N_SUB)]
    normed = {}
    carry = {}
    att_rhs = {}

    def proj_tile(s, i, col):
        if s == 0:
            return pfirst_ref[i * BLK:(i + 1) * BLK, col:col + LANES]
        chunk, off = divmod(col, PROJ_CHUNK)
        return proj_chunks[s][chunk][i * BLK:(i + 1) * BLK, off:off + LANES]

    def in_proj_chunk(s, c):
        def run():
            if s not in normed:
                xs = x_ref[0, s * SUB_TILE:(s + 1) * SUB_TILE, :]
                ms = jnp.mean(xs * xs, axis=-1, keepdims=True)
                normed[s] = (xs * lax.rsqrt(ms + RMS_EPS) * ng_ref[...]).astype(BF16)
            proj_chunks[s][c] = _project(normed[s], win_ref, c * PROJ_CHUNK, min((c + 1) * PROJ_CHUNK, IN_WIDTH))
        return run

    def out_proj_chunk(s, c):
        def run():
            out_chunks[s][c] = jnp.dot(mix_refs[s][...], wout_ref[:, c * PROJ_CHUNK:(c + 1) * PROJ_CHUNK],
                                       preferred_element_type=F32)
        return run

    def finish(s):
        def run():
            rows = slice(s * SUB_TILE, (s + 1) * SUB_TILE)
            y = x_ref[0, rows, :] + jnp.concatenate([out_chunks[s][c] for c in range(OUT_CHUNKS)], axis=1)
            ms = jnp.mean(y * y, axis=-1, keepdims=True)
            o_ref[0, rows, :] = y * lax.rsqrt(ms + RMS_EPS) * fg_ref[...]
        return run

    def kv_prep(s, i):
        if s == 0 and i == 0:
            carry["k"] = [kprev_ref[g] for g in range(ATT_KV_HEADS)]
            carry["v"] = [vprev_ref[g] for g in range(ATT_KV_HEADS)]
        k_cur = proj_tile(s, i, OFF_AK)
        v_cur = proj_tile(s, i, OFF_AV)
        v_sw = pltpu.roll(v_cur, ATT_HEAD_DIM, 1)
        k_t = k_cur.T
        kdup_cur = [jnp.concatenate([k_t[g * ATT_HEAD_DIM:(g + 1) * ATT_HEAD_DIM]] * HEADS_PER_TILE,
                                    axis=0).astype(BF16) for g in range(ATT_KV_HEADS)]
        vdup_cur = [jnp.where(lo_half, v_cur, v_sw).astype(BF16),
                    jnp.where(lo_half, v_sw, v_cur).astype(BF16)]
        per_head = []
        for g in range(ATT_KV_HEADS):
            kdup = jnp.concatenate([carry["k"][g], kdup_cur[g]], axis=1)
            vdup = jnp.concatenate([carry["v"][g], vdup_cur[g]], axis=0)
            zero2 = jnp.zeros_like(vdup)
            rhs_v = jnp.concatenate([
                jnp.concatenate([jnp.where(lo_half2, vdup, zero2), ones_lo], axis=1),
                jnp.concatenate([jnp.where(lo_half2, zero2, vdup), ones_hi], axis=1),
            ], axis=0)
            per_head.append((kdup, rhs_v))
        att_rhs[(s, i)] = per_head
        carry["k"], carry["v"] = kdup_cur, vdup_cur

    stash = {}

    def att_scores(s, i):
        def run():
            kv_prep(s, i)
            valid = band & (in_cur | (j > 0)) if (s == 0 and i == 0) else band
            scores = []
            for p in range(ATT_PAIRS):
                kdup, _ = att_rhs[(s, i)][p // (GROUP // HEADS_PER_TILE)]
                q2 = (proj_tile(s, i, OFF_AQ + p * LANES) * (ATT_HEAD_DIM ** -0.5 * LOG2E)).astype(BF16)
                for e in range(HEADS_PER_TILE):
                    qm = jnp.where(lo_half if e == 0 else hi_half, q2, zeros_bf)
                    sc = jnp.dot(qm, kdup, preferred_element_type=F32)
                    scores.append(jnp.where(valid, sc, NEG_INF))
            stash[("sc", s, i)] = scores
        return run

    def att_softmax(s, i):
        def run():
            probs, sink_terms = [], []
            for head, sc in enumerate(stash.pop(("sc", s, i))):
                sink = sinks_ref[head] * LOG2E
                m = jnp.maximum(jnp.max(sc, axis=-1, keepdims=True), sink)
                probs.append(jnp.exp2(sc - m).astype(BF16))
                sink_terms.append(jnp.exp2(sink - m))
            stash[("p", s, i)] = (probs, sink_terms)
        return run

    def att_values(s, i):
        def run():
            probs, sink_terms = stash.pop(("p", s, i))
            for p in range(ATT_PAIRS):
                _, rhs_v = att_rhs[(s, i)][p // (GROUP // HEADS_PER_TILE)]
                h0 = p * HEADS_PER_TILE
                lhs = jnp.concatenate(probs[h0:h0 + HEADS_PER_TILE], axis=1)
                res = jnp.dot(lhs, rhs_v, preferred_element_type=F32)
                den = res[:, LANES:] + jnp.where(lo_half, sink_terms[h0], sink_terms[h0 + 1])
                a2 = res[:, :LANES] / den
                z2 = proj_tile(s, i, OFF_AZ + p * LANES)
                mix_refs[s][i * BLK:(i + 1) * BLK, p * LANES:(p + 1) * LANES] = (a2 * _silu(z2)).astype(BF16)
        return run

    def ret_scores(s, i):
        def run():
            r0 = (s * BLKS_PER_SUB + i) * BLK
            cos = cos_ref[r0:r0 + BLK, :]
            sin = sin_ref[r0:r0 + BLK, :]

            def rotate(t):
                swapped = jnp.where(even_lane, pltpu.roll(t, LANES - 1, 1), pltpu.roll(t, 1, 1))
                return t * cos + swapped * sin

            per_pair = []
            for p in range(RET_PAIRS):
                q2 = rotate(proj_tile(s, i, OFF_RQ + p * LANES))
                k2 = rotate(proj_tile(s, i, OFF_RK + p * LANES)) * (RET_QK_DIM ** -0.5)
                q2b = q2.astype(BF16)
                qd2b = (q2 * qdec_ref[p]).astype(BF16)
                k2t = k2.T
                k2tb = k2t.astype(BF16)
                kd2tb = (k2t * kdec_ref[p]).astype(BF16)
                rhs_qk = jnp.concatenate([jnp.where(row_lo, k2tb, zeros_bf),
                                          jnp.where(row_lo, zeros_bf, k2tb)], axis=1)
                sc2 = jnp.dot(q2b, rhs_qk, preferred_element_type=F32)
                lhs = []
                for e in range(HEADS_PER_TILE):
                    sel = lo_half if e == 0 else hi_half
                    sc = sc2[:, e * BLK:(e + 1) * BLK] * din_ref[p * HEADS_PER_TILE + e]
                    lhs.append(jnp.concatenate([sc.astype(BF16), jnp.where(sel, qd2b, zeros_bf)], axis=1))
                per_pair.append((lhs, kd2tb))
            stash[("ret", s, i)] = per_pair
        return run

    def ret_outputs(s, i):
        def run():
            per_pair = stash.pop(("ret", s, i))
            for p in range(RET_PAIRS):
                lhs, kd2tb = per_pair[p]
                st = state_ref[p * LANES:(p + 1) * LANES, :]
                st_b = st.astype(BF16)
                v_heads = [proj_tile(s, i, OFF_RV + (p * HEADS_PER_TILE + e) * LANES).astype(BF16)
                           for e in range(HEADS_PER_TILE)]
                for e in range(HEADS_PER_TILE):
                    head = p * HEADS_PER_TILE + e
                    rhs = jnp.concatenate([v_heads[e], st_b], axis=0)
                    o = jnp.dot(lhs[e], rhs, preferred_element_type=F32)
                    mu = jnp.mean(o, axis=-1, keepdims=True)
                    oc = o - mu
                    var = jnp.mean(oc * oc, axis=-1, keepdims=True)
                    on = oc * lax.rsqrt(var + GN_EPS) * gng_ref[:, head * LANES:(head + 1) * LANES]
                    rz = proj_tile(s, i, OFF_RZ + head * LANES)
                    mc = ATT_WIDTH + head * LANES
                    mix_refs[s][i * BLK:(i + 1) * BLK, mc:mc + LANES] = (on * _silu(rz)).astype(BF16)
                kv = jnp.dot(kd2tb, jnp.concatenate(v_heads, axis=1),
                             preferred_element_type=F32)
                kv_sel = jnp.where(row_lo, kv[:, :LANES], kv[:, LANES:])
                state_ref[p * LANES:(p + 1) * LANES, :] = st * cdec_ref[p * LANES:(p + 1) * LANES, :] + kv_sel
        return run

    def mixer_units(s):
        units = []
        for stage in (att_scores, ret_scores, att_softmax, ret_outputs, att_values):
            units += [stage(s, i) for i in range(BLKS_PER_SUB)]
        return units

    for s in range(N_SUB):
        fillers = []
        if s >= 1:
            fillers += [out_proj_chunk(s - 1, c) for c in range(OUT_CHUNKS)] + [finish(s - 1)]
        if s + 1 < N_SUB:
            fillers += [in_proj_chunk(s + 1, c) for c in range(IN_CHUNKS)]
        else:
            fillers += project_first_subtile(xnext_ref)
        if s == 0:
            fillers.pop(0)()
        _interleave(mixer_units(s), fillers)
    for c in range(OUT_CHUNKS):
        out_proj_chunk(N_SUB - 1, c)()
    finish(N_SUB - 1)()

    for g in range(ATT_KV_HEADS):
        kprev_ref[g] = carry["k"][g]
        vprev_ref[g] = carry["v"][g]


def _retention_tables(seq):
    pos = np.arange(seq, dtype=np.float64)
    theta = 1.0 / (ROT_BASE ** np.linspace(0.0, 1.0, RET_QK_DIM // 2))
    ang = pos[:, None] * theta[None, :]
    cos, sin = np.cos(ang), np.sin(ang)
    cos_l = np.tile(np.repeat(cos, 2, axis=1), (1, HEADS_PER_TILE))
    sign = np.tile(np.array([-1.0, 1.0]), RET_QK_DIM // 2)
    sin_l = np.tile(np.repeat(sin, 2, axis=1) * sign[None, :], (1, HEADS_PER_TILE))

    log_gamma = np.log(1.0 - 2.0 ** (-5.0 - np.arange(RET_HEADS, dtype=np.float64)))
    idx = np.arange(BLK, dtype=np.float64)
    rel = idx[:, None] - idx[None, :]
    decay_in = np.where(rel >= 0, np.exp(log_gamma[:, None, None] * np.maximum(rel, 0.0)), 0.0)
    k_dec = np.exp(log_gamma[:, None] * (BLK - 1 - idx)[None, :])
    q_dec = np.exp(log_gamma[:, None] * (idx + 1)[None, :])
    chunk_decay = np.exp(log_gamma * BLK)

    def per_pair(dec):
        t = np.repeat(dec.T[:, :, None], RET_QK_DIM, axis=2)
        t = t.reshape(BLK, RET_PAIRS, LANES)
        return np.transpose(t, (1, 0, 2))

    cdec = np.broadcast_to(np.repeat(chunk_decay, RET_QK_DIM)[:, None],
                           (RET_QK_WIDTH, RET_V_DIM))
    k_dec_t = np.transpose(per_pair(k_dec), (0, 2, 1))
    tables = (cos_l, sin_l, decay_in, per_pair(q_dec), k_dec_t, cdec)
    return tuple(jnp.asarray(np.ascontiguousarray(a, dtype=np.float32)) for a in tables)


def _full(shape):
    return pl.BlockSpec(shape, lambda step: (0,) * len(shape))


@jax.jit
def kernel(x, norm_g, w_in, att_sinks, ret_gn_g, w_out, final_g):
    batch, seq, d = x.shape
    depth = w_in.shape[0]
    assert depth == 1 and d == D_MODEL and seq % SEQ_TILE == 0
    cos_l, sin_l, decay_in, qdec, kdec, cdec = _retention_tables(seq)
    tile = SEQ_TILE
    n_seq = seq // tile
    n_tiles = batch * n_seq

    def tile_of(step):
        return jnp.maximum(step - CAST_STEPS, 0)

    def this_tile(step):
        t = tile_of(step)
        return (t // n_seq, t % n_seq, 0)

    def next_first_subtile(step):
        u = jnp.minimum(tile_of(step) + 1, n_tiles - 1)
        return (u // n_seq, (u % n_seq) * N_SUB, 0)

    def weight_slab(step):
        return (jnp.minimum(step, CAST_STEPS - 1), 0)

    call = pl.pallas_call(
        functools.partial(_layer_kernel, n_seq),
        grid=(CAST_STEPS + n_tiles,),
        in_specs=[
            pl.BlockSpec(memory_space=pltpu.SMEM),
            pl.BlockSpec((1, tile, D_MODEL), this_tile),
            pl.BlockSpec((1, SUB_TILE, D_MODEL), next_first_subtile),
            _full((1, D_MODEL)),
            pl.BlockSpec((WEIGHT_SLAB, IN_WIDTH), weight_slab),
            _full((1, RET_WIDTH)),
            pl.BlockSpec((WEIGHT_SLAB, D_MODEL), weight_slab),
            _full((1, D_MODEL)),
            pl.BlockSpec((tile, LANES), lambda step: (tile_of(step) % n_seq, 0)),
            pl.BlockSpec((tile, LANES), lambda step: (tile_of(step) % n_seq, 0)),
            _full((RET_HEADS, BLK, BLK)),
            _full((RET_PAIRS, BLK, LANES)),
            _full((RET_PAIRS, BLK, LANES)),
            _full((RET_QK_WIDTH, RET_V_DIM)),
        ],
        out_specs=pl.BlockSpec((1, tile, D_MODEL), this_tile),
        out_shape=jax.ShapeDtypeStruct(x.shape, x.dtype),
        scratch_shapes=[
            pltpu.VMEM((RET_QK_WIDTH, RET_V_DIM), F32),
            pltpu.VMEM((ATT_KV_HEADS, BLK, LANES), BF16),
            pltpu.VMEM((ATT_KV_HEADS, BLK, LANES), BF16),
            pltpu.VMEM((SUB_TILE, IN_WIDTH), F32),
            pltpu.VMEM((D_MODEL, IN_WIDTH), BF16),
            pltpu.VMEM((MIX_WIDTH, D_MODEL), BF16),
        ] + [pltpu.VMEM((SUB_TILE, MIX_WIDTH), BF16)] * N_SUB,
        compiler_params=pltpu.CompilerParams(
            dimension_semantics=("arbitrary",),
            vmem_limit_bytes=VMEM_LIMIT_BYTES),
        name="hymba_layer",
    )
    return call(att_sinks[0], x, x, norm_g[0][None, :], w_in[0],
                ret_gn_g[0][None, :], w_out[0], final_g[None, :],
                cos_l, sin_l, decay_in, qdec, kdec, cdec)
```

```python
import functools

import jax
import jax.numpy as jnp
import numpy as np
from jax import lax
from jax.experimental import pallas as pl
from jax.experimental.pallas import tpu as pltpu

D_MODEL = 1024
ATT_HEADS = 8
ATT_KV_HEADS = 2
ATT_HEAD_DIM = 64
WINDOW = 128
BLK = 128
RET_HEADS = 4
RET_QK_DIM = 64
RET_V_DIM = 128
ROT_BASE = 10000.0
RMS_EPS = 1e-6
GN_EPS = 1e-6
NEG_INF = -1e30
LOG2E = 1.4426950408889634

ATT_WIDTH = ATT_HEADS * ATT_HEAD_DIM
ATT_KV_WIDTH = ATT_KV_HEADS * ATT_HEAD_DIM
RET_QK_WIDTH = RET_HEADS * RET_QK_DIM
RET_WIDTH = RET_HEADS * RET_V_DIM
MIX_WIDTH = ATT_WIDTH + RET_WIDTH
IN_WIDTH = 2 * ATT_WIDTH + 2 * ATT_KV_WIDTH + 2 * RET_QK_WIDTH + 2 * RET_WIDTH

OFF_AQ = 0
OFF_AK = OFF_AQ + ATT_WIDTH
OFF_AV = OFF_AK + ATT_KV_WIDTH
OFF_AZ = OFF_AV + ATT_KV_WIDTH
OFF_RQ = OFF_AZ + ATT_WIDTH
OFF_RK = OFF_RQ + RET_QK_WIDTH
OFF_RV = OFF_RK + RET_QK_WIDTH
OFF_RZ = OFF_RV + RET_WIDTH

LANES = 128
MXU_COLS = 256
N_MXU = 2
PROJ_CHUNK = N_MXU * MXU_COLS
HEADS_PER_TILE = LANES // ATT_HEAD_DIM
ATT_PAIRS = ATT_HEADS // HEADS_PER_TILE
RET_PAIRS = RET_HEADS // HEADS_PER_TILE
GROUP = ATT_HEADS // ATT_KV_HEADS

SEQ_TILE = 1024
SUB_TILE = 256
N_SUB = SEQ_TILE // SUB_TILE
BLKS_PER_SUB = SUB_TILE // BLK
IN_CHUNKS = -(-IN_WIDTH // PROJ_CHUNK)
OUT_CHUNKS = D_MODEL // PROJ_CHUNK
WEIGHT_SLAB = 256
CAST_STEPS = D_MODEL // WEIGHT_SLAB
VMEM_LIMIT_BYTES = 56 * 1024 * 1024

F32 = jnp.float32
BF16 = jnp.bfloat16


def _silu(z):
    hz = 0.5 * z
    return hz + hz * jnp.tanh(hz)


def _interleave(units, fillers):
    done = 0
    for k, unit in enumerate(units):
        unit()
        due = -(-(k + 1) * len(fillers) // len(units))
        while done < due:
            fillers[done]()
            done += 1
    for f in fillers[done:]:
        f()


def _layer_kernel(tiles_per_seq, sinks_ref, x_ref, xnext_ref, ng_ref, win_slab_ref, gng_ref, wout_slab_ref, fg_ref,
                  cos_ref, sin_ref, din_ref, qdec_ref, kdec_ref, cdec_ref,
                  o_ref, state_ref, kprev_ref, vprev_ref, pfirst_ref, win_ref, wout_ref, *mix_refs):
    step = pl.program_id(0)

    @pl.when(step < CAST_STEPS)
    def _():
        rows = pl.ds(pl.multiple_of(step * WEIGHT_SLAB, WEIGHT_SLAB), WEIGHT_SLAB)
        win_ref[rows, :] = win_slab_ref[...].astype(BF16)
        wout_ref[rows, :] = wout_slab_ref[...].astype(BF16)

    @pl.when(step >= CAST_STEPS)
    def _():
        _tile_step(step - CAST_STEPS, tiles_per_seq, sinks_ref, x_ref, xnext_ref, ng_ref, win_ref, gng_ref, wout_ref,
                   fg_ref, cos_ref, sin_ref, din_ref, qdec_ref, kdec_ref, cdec_ref,
                   o_ref, state_ref, kprev_ref, vprev_ref, pfirst_ref, mix_refs)


def _tile_step(t, tiles_per_seq, sinks_ref, x_ref, xnext_ref, ng_ref, win_ref, gng_ref, wout_ref, fg_ref,
               cos_ref, sin_ref, din_ref, qdec_ref, kdec_ref, cdec_ref,
               o_ref, state_ref, kprev_ref, vprev_ref, pfirst_ref, mix_refs):
    j = lax.rem(t, tiles_per_seq)

    def project_first_subtile(src_ref):
        held = {}

        def chunk(c):
            def run():
                if not held:
                    xs = src_ref[0]
                    ms = jnp.mean(xs * xs, axis=-1, keepdims=True)
                    held["h"] = (xs * lax.rsqrt(ms + RMS_EPS) * ng_ref[...]).astype(BF16)
                c0, c1 = c * PROJ_CHUNK, min((c + 1) * PROJ_CHUNK, IN_WIDTH)
                pfirst_ref[:, c0:c1] = jnp.dot(held["h"], win_ref[:, c0:c1], preferred_element_type=F32)
            return run
        return [chunk(c) for c in range(IN_CHUNKS)]

    @pl.when(t == 0)
    def _():
        for f in project_first_subtile(x_ref.at[:, 0:SUB_TILE, :]):
            f()

    @pl.when(j == 0)
    def _():
        state_ref[...] = jnp.zeros_like(state_ref)
        kprev_ref[...] = jnp.zeros_like(kprev_ref)
        vprev_ref[...] = jnp.zeros_like(vprev_ref)

    lane = lax.broadcasted_iota(jnp.int32, (BLK, LANES), 1)
    lo_half = lane < ATT_HEAD_DIM
    hi_half = jnp.logical_not(lo_half)
    even_lane = (lane & 1) == 0
    lane2 = lax.broadcasted_iota(jnp.int32, (2 * BLK, LANES), 1)
    lo_half2 = lane2 < ATT_HEAD_DIM
    qi = lax.broadcasted_iota(jnp.int32, (BLK, 2 * BLK), 0)
    kj = lax.broadcasted_iota(jnp.int32, (BLK, 2 * BLK), 1)
    diff = qi + BLK - kj
    band = (diff >= 0) & (diff < WINDOW)
    in_cur = kj >= BLK
    ones_lo = jnp.where(lo_half2, 1.0, 0.0).astype(BF16)
    ones_hi = jnp.where(lo_half2, 0.0, 1.0).astype(BF16)
    zeros_bf = jnp.zeros((BLK, LANES), BF16)
    row_lo = lax.broadcasted_iota(jnp.int32, (LANES, LANES), 0) < RET_QK_DIM

    proj_chunks = [dict() for _ in range(N_SUB)]
    out_chunks = [dict() for _ in range(N_SUB)]
    normed = {}
    carry = {}
    att_rhs = {}

    def proj_tile(s, i, col):
        if s == 0:
            return pfirst_ref[i * BLK:(i + 1) * BLK, col:col + LANES]
        chunk, off = divmod(col, PROJ_CHUNK)
        return proj_chunks[s][chunk][i * BLK:(i + 1) * BLK, off:off + LANES]

    def in_proj_chunk(s, c):
        def run():
            if s not in normed:
                xs = x_ref[0, s * SUB_TILE:(s + 1) * SUB_TILE, :]
                ms = jnp.mean(xs * xs, axis=-1, keepdims=True)
                normed[s] = (xs * lax.rsqrt(ms + RMS_EPS) * ng_ref[...]).astype(BF16)
            proj_chunks[s][c] = jnp.dot(normed[s], win_ref[:, c * PROJ_CHUNK:min((c + 1) * PROJ_CHUNK, IN_WIDTH)],
                                        preferred_element_type=F32)
        return run

    def out_proj_chunk(s, c):
        def run():
            out_chunks[s][c] = jnp.dot(mix_refs[s][...], wout_ref[:, c * PROJ_CHUNK:(c + 1) * PROJ_CHUNK],
                                       preferred_element_type=F32)
        return run

    def finish(s):
        def run():
            rows = slice(s * SUB_TILE, (s + 1) * SUB_TILE)
            y = x_ref[0, rows, :] + jnp.concatenate([out_chunks[s][c] for c in range(OUT_CHUNKS)], axis=1)
            ms = jnp.mean(y * y, axis=-1, keepdims=True)
            o_ref[0, rows, :] = y * lax.rsqrt(ms + RMS_EPS) * fg_ref[...]
        return run

    def kv_prep(s, i):
        if s == 0 and i == 0:
            carry["k"] = [kprev_ref[g] for g in range(ATT_KV_HEADS)]
            carry["v"] = [vprev_ref[g] for g in range(ATT_KV_HEADS)]
        k_cur = proj_tile(s, i, OFF_AK)
        v_cur = proj_tile(s, i, OFF_AV)
        v_sw = pltpu.roll(v_cur, ATT_HEAD_DIM, 1)
        k_t = k_cur.T
        kdup_cur = [jnp.concatenate([k_t[g * ATT_HEAD_DIM:(g + 1) * ATT_HEAD_DIM]] * HEADS_PER_TILE,
                                    axis=0).astype(BF16) for g in range(ATT_KV_HEADS)]
        vdup_cur = [jnp.where(lo_half, v_cur, v_sw).astype(BF16),
                    jnp.where(lo_half, v_sw, v_cur).astype(BF16)]
        per_head = []
        for g in range(ATT_KV_HEADS):
            kdup = jnp.concatenate([carry["k"][g], kdup_cur[g]], axis=1)
            vdup = jnp.concatenate([carry["v"][g], vdup_cur[g]], axis=0)
            zero2 = jnp.zeros_like(vdup)
            rhs_v = jnp.concatenate([
                jnp.concatenate([jnp.where(lo_half2, vdup, zero2), ones_lo], axis=1),
                jnp.concatenate([jnp.where(lo_half2, zero2, vdup), ones_hi], axis=1),
            ], axis=0)
            per_head.append((kdup, rhs_v))
        att_rhs[(s, i)] = per_head
        carry["k"], carry["v"] = kdup_cur, vdup_cur

    stash = {}

    def att_scores(s, i):
        def run():
            kv_prep(s, i)
            valid = band & (in_cur | (j > 0)) if (s == 0 and i == 0) else band
            scores = []
            for p in range(ATT_PAIRS):
                kdup, _ = att_rhs[(s, i)][p // (GROUP // HEADS_PER_TILE)]
                q2 = (proj_tile(s, i, OFF_AQ + p * LANES) * (ATT_HEAD_DIM ** -0.5 * LOG2E)).astype(BF16)
                for e in range(HEADS_PER_TILE):
                    qm = jnp.where(lo_half if e == 0 else hi_half, q2, zeros_bf)
                    sc = jnp.dot(qm, kdup, preferred_element_type=F32)
                    scores.append(jnp.where(valid, sc, NEG_INF))
            stash[("sc", s, i)] = scores
        return run

    def att_softmax(s, i):
        def run():
            probs, sink_terms = [], []
            for head, sc in enumerate(stash.pop(("sc", s, i))):
                sink = sinks_ref[head] * LOG2E
                m = jnp.maximum(jnp.max(sc, axis=-1, keepdims=True), sink)
                probs.append(jnp.exp2(sc - m).astype(BF16))
                sink_terms.append(jnp.exp2(sink - m))
            stash[("p", s, i)] = (probs, sink_terms)
        return run

    def att_values(s, i):
        def run():
            probs, sink_terms = stash.pop(("p", s, i))
            for p in range(ATT_PAIRS):
                _, rhs_v = att_rhs[(s, i)][p // (GROUP // HEADS_PER_TILE)]
                h0 = p * HEADS_PER_TILE
                lhs = jnp.concatenate(probs[h0:h0 + HEADS_PER_TILE], axis=1)
                res = jnp.dot(lhs, rhs_v, preferred_element_type=F32)
                den = res[:, LANES:] + jnp.where(lo_half, sink_terms[h0], sink_terms[h0 + 1])
                a2 = res[:, :LANES] / den
                z2 = proj_tile(s, i, OFF_AZ + p * LANES)
                mix_refs[s][i * BLK:(i + 1) * BLK, p * LANES:(p + 1) * LANES] = (a2 * _silu(z2)).astype(BF16)
        return run

    def ret_scores(s, i):
        def run():
            r0 = (s * BLKS_PER_SUB + i) * BLK
            cos = cos_ref[r0:r0 + BLK, :]
            sin = sin_ref[r0:r0 + BLK, :]

            def rotate(t):
                swapped = jnp.where(even_lane, pltpu.roll(t, LANES - 1, 1), pltpu.roll(t, 1, 1))
                return t * cos + swapped * sin

            per_pair = []
            for p in range(RET_PAIRS):
                q2 = rotate(proj_tile(s, i, OFF_RQ + p * LANES))
                k2 = rotate(proj_tile(s, i, OFF_RK + p * LANES)) * (RET_QK_DIM ** -0.5)
                q2b = q2.astype(BF16)
                qd2b = (q2 * qdec_ref[p]).astype(BF16)
                k2t = k2.T
                k2tb = k2t.astype(BF16)
                kd2tb = (k2t * kdec_ref[p]).astype(BF16)
                rhs_qk = jnp.concatenate([jnp.where(row_lo, k2tb, zeros_bf),
                                          jnp.where(row_lo, zeros_bf, k2tb)], axis=1)
                sc2 = jnp.dot(q2b, rhs_qk, preferred_element_type=F32)
                lhs = []
                for e in range(HEADS_PER_TILE):
                    sel = lo_half if e == 0 else hi_half
                    sc = sc2[:, e * BLK:(e + 1) * BLK] * din_ref[p * HEADS_PER_TILE + e]
                    lhs.append(jnp.concatenate([sc.astype(BF16), jnp.where(sel, qd2b, zeros_bf)], axis=1))
                per_pair.append((lhs, kd2tb))
            stash[("ret", s, i)] = per_pair
        return run

    def ret_outputs(s, i):
        def run():
            per_pair = stash.pop(("ret", s, i))
            for p in range(RET_PAIRS):
                lhs, kd2tb = per_pair[p]
                st = state_ref[p * LANES:(p + 1) * LANES, :]
                st_b = st.astype(BF16)
                v_heads = [proj_tile(s, i, OFF_RV + (p * HEADS_PER_TILE + e) * LANES).astype(BF16)
                           for e in range(HEADS_PER_TILE)]
                for e in range(HEADS_PER_TILE):
                    head = p * HEADS_PER_TILE + e
                    rhs = jnp.concatenate([v_heads[e], st_b], axis=0)
                    o = jnp.dot(lhs[e], rhs, preferred_element_type=F32)
                    mu = jnp.mean(o, axis=-1, keepdims=True)
                    oc = o - mu
                    var = jnp.mean(oc * oc, axis=-1, keepdims=True)
                    on = oc * lax.rsqrt(var + GN_EPS) * gng_ref[:, head * LANES:(head + 1) * LANES]
                    rz = proj_tile(s, i, OFF_RZ + head * LANES)
                    mc = ATT_WIDTH + head * LANES
                    mix_refs[s][i * BLK:(i + 1) * BLK, mc:mc + LANES] = (on * _silu(rz)).astype(BF16)
                kv = jnp.dot(kd2tb, jnp.concatenate(v_heads, axis=1),
                             preferred_element_type=F32)
                kv_sel = jnp.where(row_lo, kv[:, :LANES], kv[:, LANES:])
                state_ref[p * LANES:(p + 1) * LANES, :] = st * cdec_ref[p * LANES:(p + 1) * LANES, :] + kv_sel
        return run

    def mixer_units(s):
        units = []
        for stage in (att_scores, ret_scores, att_softmax, ret_outputs, att_values):
            units += [stage(s, i) for i in range(BLKS_PER_SUB)]
        return units

    for s in range(N_SUB):
        fillers = []
        if s >= 1:
            def whole_out_proj(prev=s - 1):
                for c in range(OUT_CHUNKS):
                    out_proj_chunk(prev, c)()
            fillers += [whole_out_proj, finish(s - 1)]
        if s + 1 < N_SUB:
            fillers += [in_proj_chunk(s + 1, c) for c in range(IN_CHUNKS)]
        else:
            fillers += project_first_subtile(xnext_ref)
        if s == 0:
            fillers.pop(0)()
        _interleave(mixer_units(s), fillers)
    for c in range(OUT_CHUNKS):
        out_proj_chunk(N_SUB - 1, c)()
    finish(N_SUB - 1)()

    for g in range(ATT_KV_HEADS):
        kprev_ref[g] = carry["k"][g]
        vprev_ref[g] = carry["v"][g]


def _retention_tables(seq):
    pos = np.arange(seq, dtype=np.float64)
    theta = 1.0 / (ROT_BASE ** np.linspace(0.0, 1.0, RET_QK_DIM // 2))
    ang = pos[:, None] * theta[None, :]
    cos, sin = np.cos(ang), np.sin(ang)
    cos_l = np.tile(np.repeat(cos, 2, axis=1), (1, HEADS_PER_TILE))
    sign = np.tile(np.array([-1.0, 1.0]), RET_QK_DIM // 2)
    sin_l = np.tile(np.repeat(sin, 2, axis=1) * sign[None, :], (1, HEADS_PER_TILE))

    log_gamma = np.log(1.0 - 2.0 ** (-5.0 - np.arange(RET_HEADS, dtype=np.float64)))
    idx = np.arange(BLK, dtype=np.float64)
    rel = idx[:, None] - idx[None, :]
    decay_in = np.where(rel >= 0, np.exp(log_gamma[:, None, None] * np.maximum(rel, 0.0)), 0.0)
    k_dec = np.exp(log_gamma[:, None] * (BLK - 1 - idx)[None, :])
    q_dec = np.exp(log_gamma[:, None] * (idx + 1)[None, :])
    chunk_decay = np.exp(log_gamma * BLK)

    def per_pair(dec):
        t = np.repeat(dec.T[:, :, None], RET_QK_DIM, axis=2)
        t = t.reshape(BLK, RET_PAIRS, LANES)
        return np.transpose(t, (1, 0, 2))

    cdec = np.broadcast_to(np.repeat(chunk_decay, RET_QK_DIM)[:, None],
                           (RET_QK_WIDTH, RET_V_DIM))
    k_dec_t = np.transpose(per_pair(k_dec), (0, 2, 1))
    tables = (cos_l, sin_l, decay_in, per_pair(q_dec), k_dec_t, cdec)
    return tuple(jnp.asarray(np.ascontiguousarray(a, dtype=np.float32)) for a in tables)


def _full(shape):
    return pl.BlockSpec(shape, lambda step: (0,) * len(shape))


@jax.jit
def kernel(x, norm_g, w_in, att_sinks, ret_gn_g, w_out, final_g):
    batch, seq, d = x.shape
    depth = w_in.shape[0]
    assert depth == 1 and d == D_MODEL and seq % SEQ_TILE == 0
    cos_l, sin_l, decay_in, qdec, kdec, cdec = _retention_tables(seq)
    tile = SEQ_TILE
    n_seq = seq // tile
    n_tiles = batch * n_seq

    def tile_of(step):
        return jnp.maximum(step - CAST_STEPS, 0)

    def this_tile(step):
        t = tile_of(step)
        return (t // n_seq, t % n_seq, 0)

    def next_first_subtile(step):
        u = jnp.minimum(tile_of(step) + 1, n_tiles - 1)
        return (u // n_seq, (u % n_seq) * N_SUB, 0)

    def weight_slab(step):
        return (jnp.minimum(step, CAST_STEPS - 1), 0)

    call = pl.pallas_call(
        functools.partial(_layer_kernel, n_seq),
        grid=(CAST_STEPS + n_tiles,),
        in_specs=[
            pl.BlockSpec(memory_space=pltpu.SMEM),
            pl.BlockSpec((1, tile, D_MODEL), this_tile),
            pl.BlockSpec((1, SUB_TILE, D_MODEL), next_first_subtile),
            _full((1, D_MODEL)),
            pl.BlockSpec((WEIGHT_SLAB, IN_WIDTH), weight_slab),
            _full((1, RET_WIDTH)),
            pl.BlockSpec((WEIGHT_SLAB, D_MODEL), weight_slab),
            _full((1, D_MODEL)),
            pl.BlockSpec((tile, LANES), lambda step: (tile_of(step) % n_seq, 0)),
            pl.BlockSpec((tile, LANES), lambda step: (tile_of(step) % n_seq, 0)),
            _full((RET_HEADS, BLK, BLK)),
            _full((RET_PAIRS, BLK, LANES)),
            _full((RET_PAIRS, BLK, LANES)),
            _full((RET_QK_WIDTH, RET_V_DIM)),
        ],
        out_specs=pl.BlockSpec((1, tile, D_MODEL), this_tile),
        out_shape=jax.ShapeDtypeStruct(x.shape, x.dtype),
        scratch_shapes=[
            pltpu.VMEM((RET_QK_WIDTH, RET_V_DIM), F32),
            pltpu.VMEM((ATT_KV_HEADS, BLK, LANES), BF16),
            pltpu.VMEM((ATT_KV_HEADS, BLK, LANES), BF16),
            pltpu.VMEM((SUB_TILE, IN_WIDTH), F32),
            pltpu.VMEM((D_MODEL, IN_WIDTH), BF16),
            pltpu.VMEM((MIX_WIDTH, D_MODEL), BF16),
        ] + [pltpu.VMEM((SUB_TILE, MIX_WIDTH), BF16)] * N_SUB,
        compiler_params=pltpu.CompilerParams(
            dimension_semantics=("arbitrary",),
            vmem_limit_bytes=VMEM_LIMIT_BYTES),
        name="hymba_layer",
    )
    return call(att_sinks[0], x, x, norm_g[0][None, :], w_in[0],
                ret_gn_g[0][None, :], w_out[0], final_g[None, :],
                cos_l, sin_l, decay_in, qdec, kdec, cdec)
```

```python
import functools

import jax
import jax.numpy as jnp
import numpy as np
from jax import lax
from jax.experimental import pallas as pl
from jax.experimental.pallas import tpu as pltpu

D_MODEL = 1024
ATT_HEADS = 8
ATT_KV_HEADS = 2
ATT_HEAD_DIM = 64
WINDOW = 128
BLK = 128
RET_HEADS = 4
RET_QK_DIM = 64
RET_V_DIM = 128
ROT_BASE = 10000.0
RMS_EPS = 1e-6
GN_EPS = 1e-6
NEG_INF = -1e30
LOG2E = 1.4426950408889634

ATT_WIDTH = ATT_HEADS * ATT_HEAD_DIM
ATT_KV_WIDTH = ATT_KV_HEADS * ATT_HEAD_DIM
RET_QK_WIDTH = RET_HEADS * RET_QK_DIM
RET_WIDTH = RET_HEADS * RET_V_DIM
MIX_WIDTH = ATT_WIDTH + RET_WIDTH
IN_WIDTH = 2 * ATT_WIDTH + 2 * ATT_KV_WIDTH + 2 * RET_QK_WIDTH + 2 * RET_WIDTH

OFF_AQ = 0
OFF_AK = OFF_AQ + ATT_WIDTH
OFF_AV = OFF_AK + ATT_KV_WIDTH
OFF_AZ = OFF_AV + ATT_KV_WIDTH
OFF_RQ = OFF_AZ + ATT_WIDTH
OFF_RK = OFF_RQ + RET_QK_WIDTH
OFF_RV = OFF_RK + RET_QK_WIDTH
OFF_RZ = OFF_RV + RET_WIDTH

LANES = 128
MXU_COLS = 256
N_MXU = 2
PROJ_CHUNK = N_MXU * MXU_COLS
HEADS_PER_TILE = LANES // ATT_HEAD_DIM
ATT_PAIRS = ATT_HEADS // HEADS_PER_TILE
RET_PAIRS = RET_HEADS // HEADS_PER_TILE
GROUP = ATT_HEADS // ATT_KV_HEADS

SEQ_TILE = 1024
SUB_TILE = 256
N_SUB = SEQ_TILE // SUB_TILE
BLKS_PER_SUB = SUB_TILE // BLK
IN_CHUNKS = -(-IN_WIDTH // PROJ_CHUNK)
OUT_CHUNKS = D_MODEL // PROJ_CHUNK
WEIGHT_SLAB = 256
CAST_STEPS = D_MODEL // WEIGHT_SLAB
VMEM_LIMIT_BYTES = 56 * 1024 * 1024

F32 = jnp.float32
BF16 = jnp.bfloat16


def _silu(z):
    hz = 0.5 * z
    return hz + hz * jnp.tanh(hz)


def _interleave(units, fillers):
    done = 0
    for k, unit in enumerate(units):
        unit()
        due = -(-(k + 1) * len(fillers) // len(units))
        while done < due:
            fillers[done]()
            done += 1
    for f in fillers[done:]:
        f()


def _layer_kernel(tiles_per_seq, sinks_ref, x_ref, xnext_ref, ng_ref, win_slab_ref, gng_ref, wout_slab_ref, fg_ref,
                  cos_ref, sin_ref, din_ref, qdec_ref, kdec_ref, cdec_ref,
                  o_ref, state_ref, kprev_ref, vprev_ref, pfirst_ref, win_ref, wout_ref, *mix_refs):
    step = pl.program_id(0)

    @pl.when(step < CAST_STEPS)
    def _():
        rows = pl.ds(pl.multiple_of(step * WEIGHT_SLAB, WEIGHT_SLAB), WEIGHT_SLAB)
        win_ref[rows, :] = win_slab_ref[...].astype(BF16)
        wout_ref[rows, :] = wout_slab_ref[...].astype(BF16)

    @pl.when(step >= CAST_STEPS)
    def _():
        _tile_step(step - CAST_STEPS, tiles_per_seq, sinks_ref, x_ref, xnext_ref, ng_ref, win_ref, gng_ref, wout_ref,
                   fg_ref, cos_ref, sin_ref, din_ref, qdec_ref, kdec_ref, cdec_ref,
                   o_ref, state_ref, kprev_ref, vprev_ref, pfirst_ref, mix_refs)


def _tile_step(t, tiles_per_seq, sinks_ref, x_ref, xnext_ref, ng_ref, win_ref, gng_ref, wout_ref, fg_ref,
               cos_ref, sin_ref, din_ref, qdec_ref, kdec_ref, cdec_ref,
               o_ref, state_ref, kprev_ref, vprev_ref, pfirst_ref, mix_refs):
    j = lax.rem(t, tiles_per_seq)

    def project_first_subtile(src_ref):
        held = {}

        def chunk(c):
            def run():
                if not held:
                    xs = src_ref[0]
                    ms = jnp.mean(xs * xs, axis=-1, keepdims=True)
                    held["h"] = (xs * lax.rsqrt(ms + RMS_EPS) * ng_ref[...]).astype(BF16)
                c0, c1 = c * PROJ_CHUNK, min((c + 1) * PROJ_CHUNK, IN_WIDTH)
                pfirst_ref[:, c0:c1] = jnp.dot(held["h"], win_ref[:, c0:c1], preferred_element_type=F32)
            return run
        return [chunk(c) for c in range(IN_CHUNKS)]

    @pl.when(t == 0)
    def _():
        for f in project_first_subtile(x_ref.at[:, 0:SUB_TILE, :]):
            f()

    @pl.when(j == 0)
    def _():
        state_ref[...] = jnp.zeros_like(state_ref)
        kprev_ref[...] = jnp.zeros_like(kprev_ref)
        vprev_ref[...] = jnp.zeros_like(vprev_ref)

    lane = lax.broadcasted_iota(jnp.int32, (BLK, LANES), 1)
    lo_half = lane < ATT_HEAD_DIM
    hi_half = jnp.logical_not(lo_half)
    even_lane = (lane & 1) == 0
    lane2 = lax.broadcasted_iota(jnp.int32, (2 * BLK, LANES), 1)
    lo_half2 = lane2 < ATT_HEAD_DIM
    qi = lax.broadcasted_iota(jnp.int32, (BLK, 2 * BLK), 0)
    kj = lax.broadcasted_iota(jnp.int32, (BLK, 2 * BLK), 1)
    diff = qi + BLK - kj
    band = (diff >= 0) & (diff < WINDOW)
    in_cur = kj >= BLK
    ones_lo = jnp.where(lo_half2, 1.0, 0.0).astype(BF16)
    ones_hi = jnp.where(lo_half2, 0.0, 1.0).astype(BF16)
    zeros_bf = jnp.zeros((BLK, LANES), BF16)
    row_lo = lax.broadcasted_iota(jnp.int32, (LANES, LANES), 0) < RET_QK_DIM

    proj_chunks = [dict() for _ in range(N_SUB)]
    out_chunks = [dict() for _ in range(N_SUB)]
    normed = {}
    carry = {}
    att_rhs = {}

    def proj_tile(s, i, col):
        if s == 0:
            return pfirst_ref[i * BLK:(i + 1) * BLK, col:col + LANES]
        chunk, off = divmod(col, PROJ_CHUNK)
        return proj_chunks[s][chunk][i * BLK:(i + 1) * BLK, off:off + LANES]

    def in_proj_chunk(s, c):
        def run():
            if s not in normed:
                xs = x_ref[0, s * SUB_TILE:(s + 1) * SUB_TILE, :]
                ms = jnp.mean(xs * xs, axis=-1, keepdims=True)
                normed[s] = (xs * lax.rsqrt(ms + RMS_EPS) * ng_ref[...]).astype(BF16)
            proj_chunks[s][c] = jnp.dot(normed[s], win_ref[:, c * PROJ_CHUNK:min((c + 1) * PROJ_CHUNK, IN_WIDTH)],
                                        preferred_element_type=F32)
        return run

    def out_proj_chunk(s, c):
        def run():
            out_chunks[s][c] = jnp.dot(mix_refs[s][...], wout_ref[:, c * PROJ_CHUNK:(c + 1) * PROJ_CHUNK],
                                       preferred_element_type=F32)
        return run

    def finish(s):
        def run():
            rows = slice(s * SUB_TILE, (s + 1) * SUB_TILE)
            y = x_ref[0, rows, :] + jnp.concatenate([out_chunks[s][c] for c in range(OUT_CHUNKS)], axis=1)
            ms = jnp.mean(y * y, axis=-1, keepdims=True)
            o_ref[0, rows, :] = y * lax.rsqrt(ms + RMS_EPS) * fg_ref[...]
        return run

    def kv_prep(s, i):
        if s == 0 and i == 0:
            carry["k"] = [kprev_ref[g] for g in range(ATT_KV_HEADS)]
            carry["v"] = [vprev_ref[g] for g in range(ATT_KV_HEADS)]
        k_cur = proj_tile(s, i, OFF_AK)
        v_cur = proj_tile(s, i, OFF_AV)
        v_sw = pltpu.roll(v_cur, ATT_HEAD_DIM, 1)
        k_t = k_cur.T
        kdup_cur = [jnp.concatenate([k_t[g * ATT_HEAD_DIM:(g + 1) * ATT_HEAD_DIM]] * HEADS_PER_TILE,
                                    axis=0).astype(BF16) for g in range(ATT_KV_HEADS)]
        vdup_cur = [jnp.where(lo_half, v_cur, v_sw).astype(BF16),
                    jnp.where(lo_half, v_sw, v_cur).astype(BF16)]
        per_head = []
        for g in range(ATT_KV_HEADS):
            kdup = jnp.concatenate([carry["k"][g], kdup_cur[g]], axis=1)
            vdup = jnp.concatenate([carry["v"][g], vdup_cur[g]], axis=0)
            zero2 = jnp.zeros_like(vdup)
            rhs_v = jnp.concatenate([
                jnp.concatenate([jnp.where(lo_half2, vdup, zero2), ones_lo], axis=1),
                jnp.concatenate([jnp.where(lo_half2, zero2, vdup), ones_hi], axis=1),
            ], axis=0)
            per_head.append((kdup, rhs_v))
        att_rhs[(s, i)] = per_head
        carry["k"], carry["v"] = kdup_cur, vdup_cur

    stash = {}

    def att_scores(s, i):
        def run():
            kv_prep(s, i)
            valid = band & (in_cur | (j > 0)) if (s == 0 and i == 0) else band
            scores = []
            for g in range(ATT_KV_HEADS):
                kdup, _ = att_rhs[(s, i)][g]
                q_rows = []
                for p in range(g * GROUP // HEADS_PER_TILE, (g + 1) * GROUP // HEADS_PER_TILE):
                    q2 = (proj_tile(s, i, OFF_AQ + p * LANES) * (ATT_HEAD_DIM ** -0.5 * LOG2E)).astype(BF16)
                    q_rows += [jnp.where(lo_half, q2, zeros_bf), jnp.where(hi_half, q2, zeros_bf)]
                sc_g = jnp.dot(jnp.concatenate(q_rows, axis=0), kdup, preferred_element_type=F32)
                scores += [jnp.where(valid, sc_g[h * BLK:(h + 1) * BLK], NEG_INF) for h in range(GROUP)]
            stash[("sc", s, i)] = scores
        return run

    def att_softmax(s, i):
        def run():
            probs, sink_terms = [], []
            for head, sc in enumerate(stash.pop(("sc", s, i))):
                sink = sinks_ref[head] * LOG2E
                m = jnp.maximum(jnp.max(sc, axis=-1, keepdims=True), sink)
                probs.append(jnp.exp2(sc - m).astype(BF16))
                sink_terms.append(jnp.exp2(sink - m))
            stash[("p", s, i)] = (probs, sink_terms)
        return run

    def att_values(s, i):
        def run():
            probs, sink_terms = stash.pop(("p", s, i))
            for p in range(ATT_PAIRS):
                _, rhs_v = att_rhs[(s, i)][p // (GROUP // HEADS_PER_TILE)]
                h0 = p * HEADS_PER_TILE
                lhs = jnp.concatenate(probs[h0:h0 + HEADS_PER_TILE], axis=1)
                res = jnp.dot(lhs, rhs_v, preferred_element_type=F32)
                den = res[:, LANES:] + jnp.where(lo_half, sink_terms[h0], sink_terms[h0 + 1])
                a2 = res[:, :LANES] / den
                z2 = proj_tile(s, i, OFF_AZ + p * LANES)
                mix_refs[s][i * BLK:(i + 1) * BLK, p * LANES:(p + 1) * LANES] = (a2 * _silu(z2)).astype(BF16)
        return run

    def ret_scores(s, i):
        def run():
            r0 = (s * BLKS_PER_SUB + i) * BLK
            cos = cos_ref[r0:r0 + BLK, :]
            sin = sin_ref[r0:r0 + BLK, :]

            def rotate(t):
                swapped = jnp.where(even_lane, pltpu.roll(t, LANES - 1, 1), pltpu.roll(t, 1, 1))
                return t * cos + swapped * sin

            per_pair = []
            for p in range(RET_PAIRS):
                q2 = rotate(proj_tile(s, i, OFF_RQ + p * LANES))
                k2 = rotate(proj_tile(s, i, OFF_RK + p * LANES)) * (RET_QK_DIM ** -0.5)
                q2b = q2.astype(BF16)
                qd2b = (q2 * qdec_ref[p]).astype(BF16)
                k2t = k2.T
                k2tb = k2t.astype(BF16)
                kd2tb = (k2t * kdec_ref[p]).astype(BF16)
                rhs_qk = jnp.concatenate([jnp.where(row_lo, k2tb, zeros_bf),
                                          jnp.where(row_lo, zeros_bf, k2tb)], axis=1)
                sc2 = jnp.dot(q2b, rhs_qk, preferred_element_type=F32)
                lhs = []
                for e in range(HEADS_PER_TILE):
                    sel = lo_half if e == 0 else hi_half
                    sc = sc2[:, e * BLK:(e + 1) * BLK] * din_ref[p * HEADS_PER_TILE + e]
                    lhs.append(jnp.concatenate([sc.astype(BF16), jnp.where(sel, qd2b, zeros_bf)], axis=1))
                per_pair.append((lhs, kd2tb))
            stash[("ret", s, i)] = per_pair
        return run

    def ret_outputs(s, i):
        def run():
            per_pair = stash.pop(("ret", s, i))
            for p in range(RET_PAIRS):
                lhs, kd2tb = per_pair[p]
                st = state_ref[p * LANES:(p + 1) * LANES, :]
                st_b = st.astype(BF16)
                v_heads = [proj_tile(s, i, OFF_RV + (p * HEADS_PER_TILE + e) * LANES).astype(BF16)
                           for e in range(HEADS_PER_TILE)]
                for e in range(HEADS_PER_TILE):
                    head = p * HEADS_PER_TILE + e
                    rhs = jnp.concatenate([v_heads[e], st_b], axis=0)
                    o = jnp.dot(lhs[e], rhs, preferred_element_type=F32)
                    mu = jnp.mean(o, axis=-1, keepdims=True)
                    oc = o - mu
                    var = jnp.mean(oc * oc, axis=-1, keepdims=True)
                    on = oc * lax.rsqrt(var + GN_EPS) * gng_ref[:, head * LANES:(head + 1) * LANES]
                    rz = proj_tile(s, i, OFF_RZ + head * LANES)
                    mc = ATT_WIDTH + head * LANES
                    mix_refs[s][i * BLK:(i + 1) * BLK, mc:mc + LANES] = (on * _silu(rz)).astype(BF16)
                kv = jnp.dot(kd2tb, jnp.concatenate(v_heads, axis=1),
                             preferred_element_type=F32)
                kv_sel = jnp.where(row_lo, kv[:, :LANES], kv[:, LANES:])
                state_ref[p * LANES:(p + 1) * LANES, :] = st * cdec_ref[p * LANES:(p + 1) * LANES, :] + kv_sel
        return run

    def mixer_units(s):
        units = []
        for stage in (att_scores, ret_scores, att_softmax, ret_outputs, att_values):
            units += [stage(s, i) for i in range(BLKS_PER_SUB)]
        return units

    for s in range(N_SUB):
        fillers = []
        if s >= 1:
            fillers += [out_proj_chunk(s - 1, c) for c in range(OUT_CHUNKS)] + [finish(s - 1)]
        if s + 1 < N_SUB:
            fillers += [in_proj_chunk(s + 1, c) for c in range(IN_CHUNKS)]
        else:
            fillers += project_first_subtile(xnext_ref)
        if s == 0:
            fillers.pop(0)()
        _interleave(mixer_units(s), fillers)
    for c in range(OUT_CHUNKS):
        out_proj_chunk(N_SUB - 1, c)()
    finish(N_SUB - 1)()

    for g in range(ATT_KV_HEADS):
        kprev_ref[g] = carry["k"][g]
        vprev_ref[g] = carry["v"][g]


def _retention_tables(seq):
    pos = np.arange(seq, dtype=np.float64)
    theta = 1.0 / (ROT_BASE ** np.linspace(0.0, 1.0, RET_QK_DIM // 2))
    ang = pos[:, None] * theta[None, :]
    cos, sin = np.cos(ang), np.sin(ang)
    cos_l = np.tile(np.repeat(cos, 2, axis=1), (1, HEADS_PER_TILE))
    sign = np.tile(np.array([-1.0, 1.0]), RET_QK_DIM // 2)
    sin_l = np.tile(np.repeat(sin, 2, axis=1) * sign[None, :], (1, HEADS_PER_TILE))

    log_gamma = np.log(1.0 - 2.0 ** (-5.0 - np.arange(RET_HEADS, dtype=np.float64)))
    idx = np.arange(BLK, dtype=np.float64)
    rel = idx[:, None] - idx[None, :]
    decay_in = np.where(rel >= 0, np.exp(log_gamma[:, None, None] * np.maximum(rel, 0.0)), 0.0)
    k_dec = np.exp(log_gamma[:, None] * (BLK - 1 - idx)[None, :])
    q_dec = np.exp(log_gamma[:, None] * (idx + 1)[None, :])
    chunk_decay = np.exp(log_gamma * BLK)

    def per_pair(dec):
        t = np.repeat(dec.T[:, :, None], RET_QK_DIM, axis=2)
        t = t.reshape(BLK, RET_PAIRS, LANES)
        return np.transpose(t, (1, 0, 2))

    cdec = np.broadcast_to(np.repeat(chunk_decay, RET_QK_DIM)[:, None],
                           (RET_QK_WIDTH, RET_V_DIM))
    k_dec_t = np.transpose(per_pair(k_dec), (0, 2, 1))
    tables = (cos_l, sin_l, decay_in, per_pair(q_dec), k_dec_t, cdec)
    return tuple(jnp.asarray(np.ascontiguousarray(a, dtype=np.float32)) for a in tables)


def _full(shape):
    return pl.BlockSpec(shape, lambda step: (0,) * len(shape))


@jax.jit
def kernel(x, norm_g, w_in, att_sinks, ret_gn_g, w_out, final_g):
    batch, seq, d = x.shape
    depth = w_in.shape[0]
    assert depth == 1 and d == D_MODEL and seq % SEQ_TILE == 0
    cos_l, sin_l, decay_in, qdec, kdec, cdec = _retention_tables(seq)
    tile = SEQ_TILE
    n_seq = seq // tile
    n_tiles = batch * n_seq

    def tile_of(step):
        return jnp.maximum(step - CAST_STEPS, 0)

    def this_tile(step):
        t = tile_of(step)
        return (t // n_seq, t % n_seq, 0)

    def next_first_subtile(step):
        u = jnp.minimum(tile_of(step) + 1, n_tiles - 1)
        return (u // n_seq, (u % n_seq) * N_SUB, 0)

    def weight_slab(step):
        return (jnp.minimum(step, CAST_STEPS - 1), 0)

    call = pl.pallas_call(
        functools.partial(_layer_kernel, n_seq),
        grid=(CAST_STEPS + n_tiles,),
        in_specs=[
            pl.BlockSpec(memory_space=pltpu.SMEM),
            pl.BlockSpec((1, tile, D_MODEL), this_tile),
            pl.BlockSpec((1, SUB_TILE, D_MODEL), next_first_subtile),
            _full((1, D_MODEL)),
            pl.BlockSpec((WEIGHT_SLAB, IN_WIDTH), weight_slab),
            _full((1, RET_WIDTH)),
            pl.BlockSpec((WEIGHT_SLAB, D_MODEL), weight_slab),
            _full((1, D_MODEL)),
            pl.BlockSpec((tile, LANES), lambda step: (tile_of(step) % n_seq, 0)),
            pl.BlockSpec((tile, LANES), lambda step: (tile_of(step) % n_seq, 0)),
            _full((RET_HEADS, BLK, BLK)),
            _full((RET_PAIRS, BLK, LANES)),
            _full((RET_PAIRS, BLK, LANES)),
            _full((RET_QK_WIDTH, RET_V_DIM)),
        ],
        out_specs=pl.BlockSpec((1, tile, D_MODEL), this_tile),
        out_shape=jax.ShapeDtypeStruct(x.shape, x.dtype),
        scratch_shapes=[
            pltpu.VMEM((RET_QK_WIDTH, RET_V_DIM), F32),
            pltpu.VMEM((ATT_KV_HEADS, BLK, LANES), BF16),
            pltpu.VMEM((ATT_KV_HEADS, BLK, LANES), BF16),
            pltpu.VMEM((SUB_TILE, IN_WIDTH), F32),
            pltpu.VMEM((D_MODEL, IN_WIDTH), BF16),
            pltpu.VMEM((MIX_WIDTH, D_MODEL), BF16),
        ] + [pltpu.VMEM((SUB_TILE, MIX_WIDTH), BF16)] * N_SUB,
        compiler_params=pltpu.CompilerParams(
            dimension_semantics=("arbitrary",),
            vmem_limit_bytes=VMEM_LIMIT_BYTES),
        name="hymba_layer",
    )
    return call(att_sinks[0], x, x, norm_g[0][None, :], w_in[0],
                ret_gn_g[0][None, :], w_out[0], final_g[None, :],
                cos_l, sin_l, decay_in, qdec, kdec, cdec)
```

```python
import functools

import jax
import jax.numpy as jnp
import numpy as np
from jax import lax
from jax.experimental import pallas as pl
from jax.experimental.pallas import tpu as pltpu

D_MODEL = 1024
ATT_HEADS = 8
ATT_KV_HEADS = 2
ATT_HEAD_DIM = 64
WINDOW = 128
BLK = 128
RET_HEADS = 4
RET_QK_DIM = 64
RET_V_DIM = 128
ROT_BASE = 10000.0
RMS_EPS = 1e-6
GN_EPS = 1e-6
NEG_INF = -1e30
LOG2E = 1.4426950408889634

ATT_WIDTH = ATT_HEADS * ATT_HEAD_DIM
ATT_KV_WIDTH = ATT_KV_HEADS * ATT_HEAD_DIM
RET_QK_WIDTH = RET_HEADS * RET_QK_DIM
RET_WIDTH = RET_HEADS * RET_V_DIM
MIX_WIDTH = ATT_WIDTH + RET_WIDTH
IN_WIDTH = 2 * ATT_WIDTH + 2 * ATT_KV_WIDTH + 2 * RET_QK_WIDTH + 2 * RET_WIDTH

OFF_AQ = 0
OFF_AK = OFF_AQ + ATT_WIDTH
OFF_AV = OFF_AK + ATT_KV_WIDTH
OFF_AZ = OFF_AV + ATT_KV_WIDTH
OFF_RQ = OFF_AZ + ATT_WIDTH
OFF_RK = OFF_RQ + RET_QK_WIDTH
OFF_RV = OFF_RK + RET_QK_WIDTH
OFF_RZ = OFF_RV + RET_WIDTH

LANES = 128
MXU_COLS = 256
N_MXU = 2
PROJ_CHUNK = N_MXU * MXU_COLS
HEADS_PER_TILE = LANES // ATT_HEAD_DIM
ATT_PAIRS = ATT_HEADS // HEADS_PER_TILE
RET_PAIRS = RET_HEADS // HEADS_PER_TILE
GROUP = ATT_HEADS // ATT_KV_HEADS

SEQ_TILE = 1024
SUB_TILE = 256
N_SUB = SEQ_TILE // SUB_TILE
BLKS_PER_SUB = SUB_TILE // BLK
IN_CHUNKS = -(-IN_WIDTH // PROJ_CHUNK)
OUT_CHUNKS = D_MODEL // PROJ_CHUNK
WEIGHT_SLAB = 256
CAST_STEPS = D_MODEL // WEIGHT_SLAB
VMEM_LIMIT_BYTES = 56 * 1024 * 1024

F32 = jnp.float32
BF16 = jnp.bfloat16


def _silu(z):
    hz = 0.5 * z
    return hz + hz * jnp.tanh(hz)


def _interleave(units, fillers):
    done = 0
    for k, unit in enumerate(units):
        unit()
        due = -(-(k + 1) * len(fillers) // len(units))
        while done < due:
            fillers[done]()
            done += 1
    for f in fillers[done:]:
        f()


def _layer_kernel(tiles_per_seq, sinks_ref, x_ref, xnext_ref, ng_ref, win_slab_ref, gng_ref, wout_slab_ref, fg_ref,
                  cos_ref, sin_ref, din_ref, qdec_ref, kdec_ref, cdec_ref,
                  o_ref, state_ref, kprev_ref, vprev_ref, pfirst_ref, win_ref, wout_ref, *mix_refs):
    step = pl.program_id(0)

    @pl.when(step < CAST_STEPS)
    def _():
        rows = pl.ds(pl.multiple_of(step * WEIGHT_SLAB, WEIGHT_SLAB), WEIGHT_SLAB)
        win_ref[rows, :] = win_slab_ref[...].astype(BF16)
        wout_ref[rows, :] = wout_slab_ref[...].astype(BF16)

    @pl.when(step >= CAST_STEPS)
    def _():
        _tile_step(step - CAST_STEPS, tiles_per_seq, sinks_ref, x_ref, xnext_ref, ng_ref, win_ref, gng_ref, wout_ref,
                   fg_ref, cos_ref, sin_ref, din_ref, qdec_ref, kdec_ref, cdec_ref,
                   o_ref, state_ref, kprev_ref, vprev_ref, pfirst_ref, mix_refs)


def _tile_step(t, tiles_per_seq, sinks_ref, x_ref, xnext_ref, ng_ref, win_ref, gng_ref, wout_ref, fg_ref,
               cos_ref, sin_ref, din_ref, qdec_ref, kdec_ref, cdec_ref,
               o_ref, state_ref, kprev_ref, vprev_ref, pfirst_ref, mix_refs):
    j = lax.rem(t, tiles_per_seq)

    def project_first_subtile(src_ref):
        held = {}

        def chunk(c):
            def run():
                if not held:
                    xs = src_ref[0]
                    ms = jnp.mean(xs * xs, axis=-1, keepdims=True)
                    held["h"] = (xs * lax.rsqrt(ms + RMS_EPS) * ng_ref[...]).astype(BF16)
                c0, c1 = c * PROJ_CHUNK, min((c + 1) * PROJ_CHUNK, IN_WIDTH)
                pfirst_ref[:, c0:c1] = jnp.dot(held["h"], win_ref[:, c0:c1], preferred_element_type=F32)
            return run
        return [chunk(c) for c in range(IN_CHUNKS)]

    @pl.when(t == 0)
    def _():
        for f in project_first_subtile(x_ref.at[:, 0:SUB_TILE, :]):
            f()

    @pl.when(j == 0)
    def _():
        state_ref[...] = jnp.zeros_like(state_ref)
        kprev_ref[...] = jnp.zeros_like(kprev_ref)
        vprev_ref[...] = jnp.zeros_like(vprev_ref)

    lane = lax.broadcasted_iota(jnp.int32, (BLK, LANES), 1)
    lo_half = lane < ATT_HEAD_DIM
    hi_half = jnp.logical_not(lo_half)
    even_lane = (lane & 1) == 0
    lane2 = lax.broadcasted_iota(jnp.int32, (2 * BLK, LANES), 1)
    lo_half2 = lane2 < ATT_HEAD_DIM
    qi = lax.broadcasted_iota(jnp.int32, (BLK, 2 * BLK), 0)
    kj = lax.broadcasted_iota(jnp.int32, (BLK, 2 * BLK), 1)
    diff = qi + BLK - kj
    band = (diff >= 0) & (diff < WINDOW)
    in_cur = kj >= BLK
    ones_lo = jnp.where(lo_half2, 1.0, 0.0).astype(BF16)
    ones_hi = jnp.where(lo_half2, 0.0, 1.0).astype(BF16)
    zeros_bf = jnp.zeros((BLK, LANES), BF16)
    row_lo = lax.broadcasted_iota(jnp.int32, (LANES, LANES), 0) < RET_QK_DIM

    proj_chunks = [dict() for _ in range(N_SUB)]
    out_chunks = [dict() for _ in range(N_SUB)]
    normed = {}
    carry = {}
    att_rhs = {}

    def proj_tile(s, i, col):
        if s == 0:
            return pfirst_ref[i * BLK:(i + 1) * BLK, col:col + LANES]
        chunk, off = divmod(col, PROJ_CHUNK)
        return proj_chunks[s][chunk][i * BLK:(i + 1) * BLK, off:off + LANES]

    def in_proj_chunk(s, c):
        def run():
            if s not in normed:
                xs = x_ref[0, s * SUB_TILE:(s + 1) * SUB_TILE, :]
                ms = jnp.mean(xs * xs, axis=-1, keepdims=True)
                normed[s] = (xs * lax.rsqrt(ms + RMS_EPS) * ng_ref[...]).astype(BF16)
            proj_chunks[s][c] = jnp.dot(normed[s], win_ref[:, c * PROJ_CHUNK:min((c + 1) * PROJ_CHUNK, IN_WIDTH)],
                                        preferred_element_type=F32)
        return run

    def out_proj_chunk(s, c):
        def run():
            out_chunks[s][c] = jnp.dot(mix_refs[s][...], wout_ref[:, c * PROJ_CHUNK:(c + 1) * PROJ_CHUNK],
                                       preferred_element_type=F32)
        return run

    def finish(s):
        def run():
            rows = slice(s * SUB_TILE, (s + 1) * SUB_TILE)
            y = x_ref[0, rows, :] + jnp.concatenate([out_chunks[s][c] for c in range(OUT_CHUNKS)], axis=1)
            ms = jnp.mean(y * y, axis=-1, keepdims=True)
            o_ref[0, rows, :] = y * lax.rsqrt(ms + RMS_EPS) * fg_ref[...]
        return run

    def kv_prep(s, i):
        if s == 0 and i == 0:
            carry["k"] = [kprev_ref[g] for g in range(ATT_KV_HEADS)]
            carry["v"] = [vprev_ref[g] for g in range(ATT_KV_HEADS)]
        k_cur = proj_tile(s, i, OFF_AK)
        v_cur = proj_tile(s, i, OFF_AV)
        v_sw = pltpu.roll(v_cur, ATT_HEAD_DIM, 1)
        k_t = k_cur.T
        kdup_cur = [jnp.concatenate([k_t[g * ATT_HEAD_DIM:(g + 1) * ATT_HEAD_DIM]] * HEADS_PER_TILE,
                                    axis=0).astype(BF16) for g in range(ATT_KV_HEADS)]
        vdup_cur = [jnp.where(lo_half, v_cur, v_sw).astype(BF16),
                    jnp.where(lo_half, v_sw, v_cur).astype(BF16)]
        per_head = []
        for g in range(ATT_KV_HEADS):
            kdup = jnp.concatenate([carry["k"][g], kdup_cur[g]], axis=1)
            vdup = jnp.concatenate([carry["v"][g], vdup_cur[g]], axis=0)
            zero2 = jnp.zeros_like(vdup)
            rhs_v = jnp.concatenate([
                jnp.concatenate([jnp.where(lo_half2, vdup, zero2), ones_lo], axis=1),
                jnp.concatenate([jnp.where(lo_half2, zero2, vdup), ones_hi], axis=1),
            ], axis=0)
            per_head.append((kdup, rhs_v))
        att_rhs[(s, i)] = per_head
        carry["k"], carry["v"] = kdup_cur, vdup_cur

    stash = {}

    def att_scores(s, i):
        def run():
            kv_prep(s, i)
            valid = band & (in_cur | (j > 0)) if (s == 0 and i == 0) else band
            scores = []
            for g in range(ATT_KV_HEADS):
                kdup, _ = att_rhs[(s, i)][g]
                q_rows = []
                for p in range(g * GROUP // HEADS_PER_TILE, (g + 1) * GROUP // HEADS_PER_TILE):
                    q2 = (proj_tile(s, i, OFF_AQ + p * LANES) * (ATT_HEAD_DIM ** -0.5 * LOG2E)).astype(BF16)
                    q_rows += [jnp.where(lo_half, q2, zeros_bf), jnp.where(hi_half, q2, zeros_bf)]
                sc_g = jnp.dot(jnp.concatenate(q_rows, axis=0), kdup, preferred_element_type=F32)
                scores += [jnp.where(valid, sc_g[h * BLK:(h + 1) * BLK], NEG_INF) for h in range(GROUP)]
            stash[("sc", s, i)] = scores
        return run

    def att_softmax(s, i):
        def run():
            probs, sink_terms = [], []
            for head, sc in enumerate(stash.pop(("sc", s, i))):
                sink = sinks_ref[head] * LOG2E
                m = jnp.maximum(jnp.max(sc, axis=-1, keepdims=True), sink)
                probs.append(jnp.exp2(sc - m).astype(BF16))
                sink_terms.append(jnp.exp2(sink - m))
            stash[("p", s, i)] = (probs, sink_terms)
        return run

    def att_values(s, i):
        def run():
            probs, sink_terms = stash.pop(("p", s, i))
            pairs_per_group = GROUP // HEADS_PER_TILE
            for g in range(ATT_KV_HEADS):
                _, rhs_v = att_rhs[(s, i)][g]
                lhs = jnp.concatenate(
                    [jnp.concatenate(probs[(g * pairs_per_group + pp) * HEADS_PER_TILE:
                                           (g * pairs_per_group + pp + 1) * HEADS_PER_TILE], axis=1)
                     for pp in range(pairs_per_group)], axis=0)
                res_g = jnp.dot(lhs, rhs_v, preferred_element_type=F32)
                for pp in range(pairs_per_group):
                    p = g * pairs_per_group + pp
                    h0 = p * HEADS_PER_TILE
                    res = res_g[pp * BLK:(pp + 1) * BLK]
                    den = res[:, LANES:] + jnp.where(lo_half, sink_terms[h0], sink_terms[h0 + 1])
                    a2 = res[:, :LANES] / den
                    z2 = proj_tile(s, i, OFF_AZ + p * LANES)
                    mix_refs[s][i * BLK:(i + 1) * BLK, p * LANES:(p + 1) * LANES] = (a2 * _silu(z2)).astype(BF16)
        return run

    def ret_scores(s, i):
        def run():
            r0 = (s * BLKS_PER_SUB + i) * BLK
            cos = cos_ref[r0:r0 + BLK, :]
            sin = sin_ref[r0:r0 + BLK, :]

            def rotate(t):
                swapped = jnp.where(even_lane, pltpu.roll(t, LANES - 1, 1), pltpu.roll(t, 1, 1))
                return t * cos + swapped * sin

            per_pair = []
            for p in range(RET_PAIRS):
                q2 = rotate(proj_tile(s, i, OFF_RQ + p * LANES))
                k2 = rotate(proj_tile(s, i, OFF_RK + p * LANES)) * (RET_QK_DIM ** -0.5)
                q2b = q2.astype(BF16)
                qd2b = (q2 * qdec_ref[p]).astype(BF16)
                k2t = k2.T
                k2tb = k2t.astype(BF16)
                kd2tb = (k2t * kdec_ref[p]).astype(BF16)
                rhs_qk = jnp.concatenate([jnp.where(row_lo, k2tb, zeros_bf),
                                          jnp.where(row_lo, zeros_bf, k2tb)], axis=1)
                sc2 = jnp.dot(q2b, rhs_qk, preferred_element_type=F32)
                lhs = []
                for e in range(HEADS_PER_TILE):
                    sel = lo_half if e == 0 else hi_half
                    sc = sc2[:, e * BLK:(e + 1) * BLK] * din_ref[p * HEADS_PER_TILE + e]
                    lhs.append(jnp.concatenate([sc.astype(BF16), jnp.where(sel, qd2b, zeros_bf)], axis=1))
                per_pair.append((lhs, kd2tb))
            stash[("ret", s, i)] = per_pair
        return run

    def ret_outputs(s, i):
        def run():
            per_pair = stash.pop(("ret", s, i))
            for p in range(RET_PAIRS):
                lhs, kd2tb = per_pair[p]
                st = state_ref[p * LANES:(p + 1) * LANES, :]
                st_b = st.astype(BF16)
                v_heads = [proj_tile(s, i, OFF_RV + (p * HEADS_PER_TILE + e) * LANES).astype(BF16)
                           for e in range(HEADS_PER_TILE)]
                for e in range(HEADS_PER_TILE):
                    head = p * HEADS_PER_TILE + e
                    rhs = jnp.concatenate([v_heads[e], st_b], axis=0)
                    o = jnp.dot(lhs[e], rhs, preferred_element_type=F32)
                    mu = jnp.mean(o, axis=-1, keepdims=True)
                    oc = o - mu
                    var = jnp.mean(oc * oc, axis=-1, keepdims=True)
                    on = oc * lax.rsqrt(var + GN_EPS) * gng_ref[:, head * LANES:(head + 1) * LANES]
                    rz = proj_tile(s, i, OFF_RZ + head * LANES)
                    mc = ATT_WIDTH + head * LANES
                    mix_refs[s][i * BLK:(i + 1) * BLK, mc:mc + LANES] = (on * _silu(rz)).astype(BF16)
                kv = jnp.dot(kd2tb, jnp.concatenate(v_heads, axis=1),
                             preferred_element_type=F32)
                kv_sel = jnp.where(row_lo, kv[:, :LANES], kv[:, LANES:])
                state_ref[p * LANES:(p + 1) * LANES, :] = st * cdec_ref[p * LANES:(p + 1) * LANES, :] + kv_sel
        return run

    def mixer_units(s):
        units = []
        for stage in (att_scores, ret_scores, att_softmax, ret_outputs, att_values):
            units += [stage(s, i) for i in range(BLKS_PER_SUB)]
        return units

    for s in range(N_SUB):
        fillers = []
        if s >= 1:
            fillers += [out_proj_chunk(s - 1, c) for c in range(OUT_CHUNKS)] + [finish(s - 1)]
        if s + 1 < N_SUB:
            fillers += [in_proj_chunk(s + 1, c) for c in range(IN_CHUNKS)]
        else:
            fillers += project_first_subtile(xnext_ref)
        if s == 0:
            fillers.pop(0)()
        _interleave(mixer_units(s), fillers)
    for c in range(OUT_CHUNKS):
        out_proj_chunk(N_SUB - 1, c)()
    finish(N_SUB - 1)()

    for g in range(ATT_KV_HEADS):
        kprev_ref[g] = carry["k"][g]
        vprev_ref[g] = carry["v"][g]


def _retention_tables(seq):
    pos = np.arange(seq, dtype=np.float64)
    theta = 1.0 / (ROT_BASE ** np.linspace(0.0, 1.0, RET_QK_DIM // 2))
    ang = pos[:, None] * theta[None, :]
    cos, sin = np.cos(ang), np.sin(ang)
    cos_l = np.tile(np.repeat(cos, 2, axis=1), (1, HEADS_PER_TILE))
    sign = np.tile(np.array([-1.0, 1.0]), RET_QK_DIM // 2)
    sin_l = np.tile(np.repeat(sin, 2, axis=1) * sign[None, :], (1, HEADS_PER_TILE))

    log_gamma = np.log(1.0 - 2.0 ** (-5.0 - np.arange(RET_HEADS, dtype=np.float64)))
    idx = np.arange(BLK, dtype=np.float64)
    rel = idx[:, None] - idx[None, :]
    decay_in = np.where(rel >= 0, np.exp(log_gamma[:, None, None] * np.maximum(rel, 0.0)), 0.0)
    k_dec = np.exp(log_gamma[:, None] * (BLK - 1 - idx)[None, :])
    q_dec = np.exp(log_gamma[:, None] * (idx + 1)[None, :])
    chunk_decay = np.exp(log_gamma * BLK)

    def per_pair(dec):
        t = np.repeat(dec.T[:, :, None], RET_QK_DIM, axis=2)
        t = t.reshape(BLK, RET_PAIRS, LANES)
        return np.transpose(t, (1, 0, 2))

    cdec = np.broadcast_to(np.repeat(chunk_decay, RET_QK_DIM)[:, None],
                           (RET_QK_WIDTH, RET_V_DIM))
    k_dec_t = np.transpose(per_pair(k_dec), (0, 2, 1))
    tables = (cos_l, sin_l, decay_in, per_pair(q_dec), k_dec_t, cdec)
    return tuple(jnp.asarray(np.ascontiguousarray(a, dtype=np.float32)) for a in tables)


def _full(shape):
    return pl.BlockSpec(shape, lambda step: (0,) * len(shape))


@jax.jit
def kernel(x, norm_g, w_in, att_sinks, ret_gn_g, w_out, final_g):
    batch, seq, d = x.shape
    depth = w_in.shape[0]
    assert depth == 1 and d == D_MODEL and seq % SEQ_TILE == 0
    cos_l, sin_l, decay_in, qdec, kdec, cdec = _retention_tables(seq)
    tile = SEQ_TILE
    n_seq = seq // tile
    n_tiles = batch * n_seq

    def tile_of(step):
        return jnp.maximum(step - CAST_STEPS, 0)

    def this_tile(step):
        t = tile_of(step)
        return (t // n_seq, t % n_seq, 0)

    def next_first_subtile(step):
        u = jnp.minimum(tile_of(step) + 1, n_tiles - 1)
        return (u // n_seq, (u % n_seq) * N_SUB, 0)

    def weight_slab(step):
        return (jnp.minimum(step, CAST_STEPS - 1), 0)

    call = pl.pallas_call(
        functools.partial(_layer_kernel, n_seq),
        grid=(CAST_STEPS + n_tiles,),
        in_specs=[
            pl.BlockSpec(memory_space=pltpu.SMEM),
            pl.BlockSpec((1, tile, D_MODEL), this_tile),
            pl.BlockSpec((1, SUB_TILE, D_MODEL), next_first_subtile),
            _full((1, D_MODEL)),
            pl.BlockSpec((WEIGHT_SLAB, IN_WIDTH), weight_slab),
            _full((1, RET_WIDTH)),
            pl.BlockSpec((WEIGHT_SLAB, D_MODEL), weight_slab),
            _full((1, D_MODEL)),
            pl.BlockSpec((tile, LANES), lambda step: (tile_of(step) % n_seq, 0)),
            pl.BlockSpec((tile, LANES), lambda step: (tile_of(step) % n_seq, 0)),
            _full((RET_HEADS, BLK, BLK)),
            _full((RET_PAIRS, BLK, LANES)),
            _full((RET_PAIRS, BLK, LANES)),
            _full((RET_QK_WIDTH, RET_V_DIM)),
        ],
        out_specs=pl.BlockSpec((1, tile, D_MODEL), this_tile),
        out_shape=jax.ShapeDtypeStruct(x.shape, x.dtype),
        scratch_shapes=[
            pltpu.VMEM((RET_QK_WIDTH, RET_V_DIM), F32),
            pltpu.VMEM((ATT_KV_HEADS, BLK, LANES), BF16),
            pltpu.VMEM((ATT_KV_HEADS, BLK, LANES), BF16),
            pltpu.VMEM((SUB_TILE, IN_WIDTH), F32),
            pltpu.VMEM((D_MODEL, IN_WIDTH), BF16),
            pltpu.VMEM((MIX_WIDTH, D_MODEL), BF16),
        ] + [pltpu.VMEM((SUB_TILE, MIX_WIDTH), BF16)] * N_SUB,
        compiler_params=pltpu.CompilerParams(
            dimension_semantics=("arbitrary",),
            vmem_limit_bytes=VMEM_LIMIT_BYTES),
        name="hymba_layer",
    )
    return call(att_sinks[0], x, x, norm_g[0][None, :], w_in[0],
                ret_gn_g[0][None, :], w_out[0], final_g[None, :],
                cos_l, sin_l, decay_in, qdec, kdec, cdec)
```

```python
import functools

import jax
import jax.numpy as jnp
import numpy as np
from jax import lax
from jax.experimental import pallas as pl
from jax.experimental.pallas import tpu as pltpu

D_MODEL = 1024
ATT_HEADS = 8
ATT_KV_HEADS = 2
ATT_HEAD_DIM = 64
WINDOW = 128
BLK = 128
RET_HEADS = 4
RET_QK_DIM = 64
RET_V_DIM = 128
ROT_BASE = 10000.0
RMS_EPS = 1e-6
GN_EPS = 1e-6
NEG_INF = -1e30
LOG2E = 1.4426950408889634

ATT_WIDTH = ATT_HEADS * ATT_HEAD_DIM
ATT_KV_WIDTH = ATT_KV_HEADS * ATT_HEAD_DIM
RET_QK_WIDTH = RET_HEADS * RET_QK_DIM
RET_WIDTH = RET_HEADS * RET_V_DIM
MIX_WIDTH = ATT_WIDTH + RET_WIDTH
IN_WIDTH = 2 * ATT_WIDTH + 2 * ATT_KV_WIDTH + 2 * RET_QK_WIDTH + 2 * RET_WIDTH

OFF_AQ = 0
OFF_AK = OFF_AQ + ATT_WIDTH
OFF_AV = OFF_AK + ATT_KV_WIDTH
OFF_AZ = OFF_AV + ATT_KV_WIDTH
OFF_RQ = OFF_AZ + ATT_WIDTH
OFF_RK = OFF_RQ + RET_QK_WIDTH
OFF_RV = OFF_RK + RET_QK_WIDTH
OFF_RZ = OFF_RV + RET_WIDTH

LANES = 128
MXU_COLS = 256
N_MXU = 2
PROJ_CHUNK = N_MXU * MXU_COLS
HEADS_PER_TILE = LANES // ATT_HEAD_DIM
ATT_PAIRS = ATT_HEADS // HEADS_PER_TILE
RET_PAIRS = RET_HEADS // HEADS_PER_TILE
GROUP = ATT_HEADS // ATT_KV_HEADS

SEQ_TILE = 1024
SUB_TILE = 256
N_SUB = SEQ_TILE // SUB_TILE
BLKS_PER_SUB = SUB_TILE // BLK
IN_CHUNKS = -(-IN_WIDTH // PROJ_CHUNK)
OUT_CHUNKS = D_MODEL // PROJ_CHUNK
WEIGHT_SLAB = 256
CAST_STEPS = D_MODEL // WEIGHT_SLAB
VMEM_LIMIT_BYTES = 56 * 1024 * 1024

F32 = jnp.float32
BF16 = jnp.bfloat16


def _silu(z):
    hz = 0.5 * z
    return hz + hz * jnp.tanh(hz)


def _interleave(units, fillers):
    done = 0
    for k, unit in enumerate(units):
        unit()
        due = -(-(k + 1) * len(fillers) // len(units))
        while done < due:
            fillers[done]()
            done += 1
    for f in fillers[done:]:
        f()


def _layer_kernel(tiles_per_seq, sinks_ref, x_ref, xnext_ref, ng_ref, win_slab_ref, gng_ref, wout_slab_ref, fg_ref,
                  cos_ref, sin_ref, din_ref, qdec_ref, kdec_ref, cdec_ref,
                  o_ref, state_ref, kprev_ref, vprev_ref, pfirst_ref, win_ref, wout_ref, *mix_refs):
    step = pl.program_id(0)

    @pl.when(step < CAST_STEPS)
    def _():
        rows = pl.ds(pl.multiple_of(step * WEIGHT_SLAB, WEIGHT_SLAB), WEIGHT_SLAB)
        win_ref[rows, :] = win_slab_ref[...].astype(BF16)
        wout_ref[rows, :] = wout_slab_ref[...].astype(BF16)

    @pl.when(step >= CAST_STEPS)
    def _():
        _tile_step(step - CAST_STEPS, tiles_per_seq, sinks_ref, x_ref, xnext_ref, ng_ref, win_ref, gng_ref, wout_ref,
                   fg_ref, cos_ref, sin_ref, din_ref, qdec_ref, kdec_ref, cdec_ref,
                   o_ref, state_ref, kprev_ref, vprev_ref, pfirst_ref, mix_refs)


def _tile_step(t, tiles_per_seq, sinks_ref, x_ref, xnext_ref, ng_ref, win_ref, gng_ref, wout_ref, fg_ref,
               cos_ref, sin_ref, din_ref, qdec_ref, kdec_ref, cdec_ref,
               o_ref, state_ref, kprev_ref, vprev_ref, pfirst_ref, mix_refs):
    j = lax.rem(t, tiles_per_seq)

    def project_first_subtile(src_ref):
        held = {}

        def chunk(c):
            def run():
                if not held:
                    xs = src_ref[0]
                    ms = jnp.mean(xs * xs, axis=-1, keepdims=True)
                    held["h"] = (xs * lax.rsqrt(ms + RMS_EPS) * ng_ref[...]).astype(BF16)
                c0, c1 = c * PROJ_CHUNK, min((c + 1) * PROJ_CHUNK, IN_WIDTH)
                pfirst_ref[:, c0:c1] = jnp.dot(held["h"], win_ref[:, c0:c1], preferred_element_type=F32)
            return run
        return [chunk(c) for c in range(IN_CHUNKS)]

    @pl.when(t == 0)
    def _():
        for f in project_first_subtile(x_ref.at[:, 0:SUB_TILE, :]):
            f()

    @pl.when(j == 0)
    def _():
        state_ref[...] = jnp.zeros_like(state_ref)
        kprev_ref[...] = jnp.zeros_like(kprev_ref)
        vprev_ref[...] = jnp.zeros_like(vprev_ref)

    lane = lax.broadcasted_iota(jnp.int32, (BLK, LANES), 1)
    lo_half = lane < ATT_HEAD_DIM
    hi_half = jnp.logical_not(lo_half)
    even_lane = (lane & 1) == 0
    lane2 = lax.broadcasted_iota(jnp.int32, (2 * BLK, LANES), 1)
    lo_half2 = lane2 < ATT_HEAD_DIM
    qi = lax.broadcasted_iota(jnp.int32, (BLK, 2 * BLK), 0)
    kj = lax.broadcasted_iota(jnp.int32, (BLK, 2 * BLK), 1)
    diff = qi + BLK - kj
    band = (diff >= 0) & (diff < WINDOW)
    in_cur = kj >= BLK
    ones_lo = jnp.where(lo_half2, 1.0, 0.0).astype(BF16)
    ones_hi = jnp.where(lo_half2, 0.0, 1.0).astype(BF16)
    zeros_bf = jnp.zeros((BLK, LANES), BF16)
    row_lo = lax.broadcasted_iota(jnp.int32, (LANES, LANES), 0) < RET_QK_DIM

    proj_chunks = [dict() for _ in range(N_SUB)]
    out_chunks = [dict() for _ in range(N_SUB)]
    normed = {}
    carry = {}
    att_rhs = {}

    def proj_tile(s, i, col):
        if s == 0:
            return pfirst_ref[i * BLK:(i + 1) * BLK, col:col + LANES]
        chunk, off = divmod(col, PROJ_CHUNK)
        return proj_chunks[s][chunk][i * BLK:(i + 1) * BLK, off:off + LANES]

    def in_proj_chunk(s, c):
        def run():
            if s not in normed:
                xs = x_ref[0, s * SUB_TILE:(s + 1) * SUB_TILE, :]
                ms = jnp.mean(xs * xs, axis=-1, keepdims=True)
                normed[s] = (xs * lax.rsqrt(ms + RMS_EPS) * ng_ref[...]).astype(BF16)
            proj_chunks[s][c] = jnp.dot(normed[s], win_ref[:, c * PROJ_CHUNK:min((c + 1) * PROJ_CHUNK, IN_WIDTH)],
                                        preferred_element_type=F32)
        return run

    def out_proj_chunk(s, c):
        def run():
            out_chunks[s][c] = jnp.dot(mix_refs[s][...], wout_ref[:, c * PROJ_CHUNK:(c + 1) * PROJ_CHUNK],
                                       preferred_element_type=F32)
        return run

    def finish(s):
        def run():
            rows = slice(s * SUB_TILE, (s + 1) * SUB_TILE)
            y = x_ref[0, rows, :] + jnp.concatenate([out_chunks[s][c] for c in range(OUT_CHUNKS)], axis=1)
            ms = jnp.mean(y * y, axis=-1, keepdims=True)
            o_ref[0, rows, :] = y * lax.rsqrt(ms + RMS_EPS) * fg_ref[...]
        return run

    def kv_prep(s, i):
        if s == 0 and i == 0:
            carry["k"] = [kprev_ref[g] for g in range(ATT_KV_HEADS)]
            carry["v"] = [vprev_ref[g] for g in range(ATT_KV_HEADS)]
        k_cur = proj_tile(s, i, OFF_AK)
        v_cur = proj_tile(s, i, OFF_AV)
        v_sw = pltpu.roll(v_cur, ATT_HEAD_DIM, 1)
        k_t = k_cur.T
        kdup_cur = [jnp.concatenate([k_t[g * ATT_HEAD_DIM:(g + 1) * ATT_HEAD_DIM]] * HEADS_PER_TILE,
                                    axis=0).astype(BF16) for g in range(ATT_KV_HEADS)]
        vdup_cur = [jnp.where(lo_half, v_cur, v_sw).astype(BF16),
                    jnp.where(lo_half, v_sw, v_cur).astype(BF16)]
        per_head = []
        for g in range(ATT_KV_HEADS):
            kdup = jnp.concatenate([carry["k"][g], kdup_cur[g]], axis=1)
            vdup = jnp.concatenate([carry["v"][g], vdup_cur[g]], axis=0)
            zero2 = jnp.zeros_like(vdup)
            rhs_v = jnp.concatenate([
                jnp.concatenate([jnp.where(lo_half2, vdup, zero2), ones_lo], axis=1),
                jnp.concatenate([jnp.where(lo_half2, zero2, vdup), ones_hi], axis=1),
            ], axis=0)
            per_head.append((kdup, rhs_v))
        att_rhs[(s, i)] = per_head
        carry["k"], carry["v"] = kdup_cur, vdup_cur

    stash = {}

    def att_scores(s, i):
        def run():
            kv_prep(s, i)
            valid = band & (in_cur | (j > 0)) if (s == 0 and i == 0) else band
            scores = []
            for g in range(ATT_KV_HEADS):
                kdup, _ = att_rhs[(s, i)][g]
                q_rows = []
                for p in range(g * GROUP // HEADS_PER_TILE, (g + 1) * GROUP // HEADS_PER_TILE):
                    q2 = (proj_tile(s, i, OFF_AQ + p * LANES) * (ATT_HEAD_DIM ** -0.5 * LOG2E)).astype(BF16)
                    q_rows += [jnp.where(lo_half, q2, zeros_bf), jnp.where(hi_half, q2, zeros_bf)]
                sc_g = jnp.dot(jnp.concatenate(q_rows, axis=0), kdup, preferred_element_type=F32)
                scores += [jnp.where(valid, sc_g[h * BLK:(h + 1) * BLK], NEG_INF) for h in range(GROUP)]
            stash[("sc", s, i)] = scores
        return run

    def att_softmax(s, i):
        def run():
            probs, sink_terms = [], []
            for head, sc in enumerate(stash.pop(("sc", s, i))):
                sink = sinks_ref[head] * LOG2E
                m = jnp.maximum(jnp.max(sc, axis=-1, keepdims=True), sink)
                probs.append(jnp.exp2(sc - m).astype(BF16))
                sink_terms.append(jnp.exp2(sink - m))
            stash[("p", s, i)] = (probs, sink_terms)
        return run

    def att_values(s, i):
        def run():
            probs, sink_terms = stash.pop(("p", s, i))
            for p in range(ATT_PAIRS):
                _, rhs_v = att_rhs[(s, i)][p // (GROUP // HEADS_PER_TILE)]
                h0 = p * HEADS_PER_TILE
                lhs = jnp.concatenate(probs[h0:h0 + HEADS_PER_TILE], axis=1)
                res = jnp.dot(lhs, rhs_v, preferred_element_type=F32)
                den = res[:, LANES:] + jnp.where(lo_half, sink_terms[h0], sink_terms[h0 + 1])
                a2 = res[:, :LANES] / den
                z2 = proj_tile(s, i, OFF_AZ + p * LANES)
                mix_refs[s][i * BLK:(i + 1) * BLK, p * LANES:(p + 1) * LANES] = (a2 * _silu(z2)).astype(BF16)
        return run

    def ret_scores(s, i):
        def run():
            r0 = (s * BLKS_PER_SUB + i) * BLK
            cos = cos_ref[r0:r0 + BLK, :]
            sin = sin_ref[r0:r0 + BLK, :]

            def rotate(t):
                swapped = jnp.where(even_lane, pltpu.roll(t, LANES - 1, 1), pltpu.roll(t, 1, 1))
                return t * cos + swapped * sin

            per_pair = []
            for p in range(RET_PAIRS):
                q2 = rotate(proj_tile(s, i, OFF_RQ + p * LANES))
                k2 = rotate(proj_tile(s, i, OFF_RK + p * LANES)) * (RET_QK_DIM ** -0.5)
                q2b = q2.astype(BF16)
                qd2b = (q2 * qdec_ref[p]).astype(BF16)
                k2t = k2.T
                k2tb = k2t.astype(BF16)
                kd2tb = (k2t * kdec_ref[p]).astype(BF16)
                rhs_qk = jnp.concatenate([jnp.where(row_lo, k2tb, zeros_bf),
                                          jnp.where(row_lo, zeros_bf, k2tb)], axis=1)
                sc2 = jnp.dot(q2b, rhs_qk, preferred_element_type=F32)
                lhs = []
                for e in range(HEADS_PER_TILE):
                    sel = lo_half if e == 0 else hi_half
                    sc = sc2[:, e * BLK:(e + 1) * BLK] * din_ref[p * HEADS_PER_TILE + e]
                    lhs.append(jnp.concatenate([sc.astype(BF16), jnp.where(sel, qd2b, zeros_bf)], axis=1))
                per_pair.append((lhs, kd2tb))
            stash[("ret", s, i)] = per_pair
        return run

    def ret_outputs(s, i):
        def run():
            per_pair = stash.pop(("ret", s, i))
            for p in range(RET_PAIRS):
                lhs, kd2tb = per_pair[p]
                st = state_ref[p * LANES:(p + 1) * LANES, :]
                st_b = st.astype(BF16)
                v_heads = [proj_tile(s, i, OFF_RV + (p * HEADS_PER_TILE + e) * LANES).astype(BF16)
                           for e in range(HEADS_PER_TILE)]
                for e in range(HEADS_PER_TILE):
                    head = p * HEADS_PER_TILE + e
                    rhs = jnp.concatenate([v_heads[e], st_b], axis=0)
                    o = jnp.dot(lhs[e], rhs, preferred_element_type=F32)
                    mu = jnp.mean(o, axis=-1, keepdims=True)
                    oc = o - mu
                    var = jnp.mean(oc * oc, axis=-1, keepdims=True)
                    on = oc * lax.rsqrt(var + GN_EPS) * gng_ref[:, head * LANES:(head + 1) * LANES]
                    rz = proj_tile(s, i, OFF_RZ + head * LANES)
                    mc = ATT_WIDTH + head * LANES
                    mix_refs[s][i * BLK:(i + 1) * BLK, mc:mc + LANES] = (on * _silu(rz)).astype(BF16)
                kv = jnp.dot(kd2tb, jnp.concatenate(v_heads, axis=1),
                             preferred_element_type=F32)
                kv_sel = jnp.where(row_lo, kv[:, :LANES], kv[:, LANES:])
                state_ref[p * LANES:(p + 1) * LANES, :] = st * cdec_ref[p * LANES:(p + 1) * LANES, :] + kv_sel
        return run

    def mixer_units(s):
        units = []
        for stage in (att_scores, ret_scores, att_softmax, ret_outputs, att_values):
            units += [stage(s, i) for i in range(BLKS_PER_SUB)]
        return units

    for s in range(N_SUB):
        fillers = []
        if s >= 1:
            fillers += [out_proj_chunk(s - 1, c) for c in range(OUT_CHUNKS)] + [finish(s - 1)]
        if s + 1 < N_SUB:
            fillers += [in_proj_chunk(s + 1, c) for c in range(IN_CHUNKS)]
        else:
            fillers += project_first_subtile(xnext_ref)
        if s == 0:
            fillers.pop(0)()
        _interleave(mixer_units(s), fillers)
    tail_out = jnp.dot(mix_refs[N_SUB - 1][...], wout_ref[...], preferred_element_type=F32)
    for c in range(OUT_CHUNKS):
        out_chunks[N_SUB - 1][c] = tail_out[:, c * PROJ_CHUNK:(c + 1) * PROJ_CHUNK]
    finish(N_SUB - 1)()

    for g in range(ATT_KV_HEADS):
        kprev_ref[g] = carry["k"][g]
        vprev_ref[g] = carry["v"][g]


def _retention_tables(seq):
    pos = np.arange(seq, dtype=np.float64)
    theta = 1.0 / (ROT_BASE ** np.linspace(0.0, 1.0, RET_QK_DIM // 2))
    ang = pos[:, None] * theta[None, :]
    cos, sin = np.cos(ang), np.sin(ang)
    cos_l = np.tile(np.repeat(cos, 2, axis=1), (1, HEADS_PER_TILE))
    sign = np.tile(np.array([-1.0, 1.0]), RET_QK_DIM // 2)
    sin_l = np.tile(np.repeat(sin, 2, axis=1) * sign[None, :], (1, HEADS_PER_TILE))

    log_gamma = np.log(1.0 - 2.0 ** (-5.0 - np.arange(RET_HEADS, dtype=np.float64)))
    idx = np.arange(BLK, dtype=np.float64)
    rel = idx[:, None] - idx[None, :]
    decay_in = np.where(rel >= 0, np.exp(log_gamma[:, None, None] * np.maximum(rel, 0.0)), 0.0)
    k_dec = np.exp(log_gamma[:, None] * (BLK - 1 - idx)[None, :])
    q_dec = np.exp(log_gamma[:, None] * (idx + 1)[None, :])
    chunk_decay = np.exp(log_gamma * BLK)

    def per_pair(dec):
        t = np.repeat(dec.T[:, :, None], RET_QK_DIM, axis=2)
        t = t.reshape(BLK, RET_PAIRS, LANES)
        return np.transpose(t, (1, 0, 2))

    cdec = np.broadcast_to(np.repeat(chunk_decay, RET_QK_DIM)[:, None],
                           (RET_QK_WIDTH, RET_V_DIM))
    k_dec_t = np.transpose(per_pair(k_dec), (0, 2, 1))
    tables = (cos_l, sin_l, decay_in, per_pair(q_dec), k_dec_t, cdec)
    return tuple(jnp.asarray(np.ascontiguousarray(a, dtype=np.float32)) for a in tables)


def _full(shape):
    return pl.BlockSpec(shape, lambda step: (0,) * len(shape))


@jax.jit
def kernel(x, norm_g, w_in, att_sinks, ret_gn_g, w_out, final_g):
    batch, seq, d = x.shape
    depth = w_in.shape[0]
    assert depth == 1 and d == D_MODEL and seq % SEQ_TILE == 0
    cos_l, sin_l, decay_in, qdec, kdec, cdec = _retention_tables(seq)
    tile = SEQ_TILE
    n_seq = seq // tile
    n_tiles = batch * n_seq

    def tile_of(step):
        return jnp.maximum(step - CAST_STEPS, 0)

    def this_tile(step):
        t = tile_of(step)
        return (t // n_seq, t % n_seq, 0)

    def next_first_subtile(step):
        u = jnp.minimum(tile_of(step) + 1, n_tiles - 1)
        return (u // n_seq, (u % n_seq) * N_SUB, 0)

    def weight_slab(step):
        return (jnp.minimum(step, CAST_STEPS - 1), 0)

    call = pl.pallas_call(
        functools.partial(_layer_kernel, n_seq),
        grid=(CAST_STEPS + n_tiles,),
        in_specs=[
            pl.BlockSpec(memory_space=pltpu.SMEM),
            pl.BlockSpec((1, tile, D_MODEL), this_tile),
            pl.BlockSpec((1, SUB_TILE, D_MODEL), next_first_subtile),
            _full((1, D_MODEL)),
            pl.BlockSpec((WEIGHT_SLAB, IN_WIDTH), weight_slab),
            _full((1, RET_WIDTH)),
            pl.BlockSpec((WEIGHT_SLAB, D_MODEL), weight_slab),
            _full((1, D_MODEL)),
            pl.BlockSpec((tile, LANES), lambda step: (tile_of(step) % n_seq, 0)),
            pl.BlockSpec((tile, LANES), lambda step: (tile_of(step) % n_seq, 0)),
            _full((RET_HEADS, BLK, BLK)),
            _full((RET_PAIRS, BLK, LANES)),
            _full((RET_PAIRS, BLK, LANES)),
            _full((RET_QK_WIDTH, RET_V_DIM)),
        ],
        out_specs=pl.BlockSpec((1, tile, D_MODEL), this_tile),
        out_shape=jax.ShapeDtypeStruct(x.shape, x.dtype),
        scratch_shapes=[
            pltpu.VMEM((RET_QK_WIDTH, RET_V_DIM), F32),
            pltpu.VMEM((ATT_KV_HEADS, BLK, LANES), BF16),
            pltpu.VMEM((ATT_KV_HEADS, BLK, LANES), BF16),
            pltpu.VMEM((SUB_TILE, IN_WIDTH), F32),
            pltpu.VMEM((D_MODEL, IN_WIDTH), BF16),
            pltpu.VMEM((MIX_WIDTH, D_MODEL), BF16),
        ] + [pltpu.VMEM((SUB_TILE, MIX_WIDTH), BF16)] * N_SUB,
        compiler_params=pltpu.CompilerParams(
            dimension_semantics=("arbitrary",),
            vmem_limit_bytes=VMEM_LIMIT_BYTES),
        name="hymba_layer",
    )
    return call(att_sinks[0], x, x, norm_g[0][None, :], w_in[0],
                ret_gn_g[0][None, :], w_out[0], final_g[None, :],
                cos_l, sin_l, decay_in, qdec, kdec, cdec)
```

```python
import functools

import jax
import jax.numpy as jnp
import numpy as np
from jax import lax
from jax.experimental import pallas as pl
from jax.experimental.pallas import tpu as pltpu

D_MODEL = 1024
ATT_HEADS = 8
ATT_KV_HEADS = 2
ATT_HEAD_DIM = 64
WINDOW = 128
BLK = 128
RET_HEADS = 4
RET_QK_DIM = 64
RET_V_DIM = 128
ROT_BASE = 10000.0
RMS_EPS = 1e-6
GN_EPS = 1e-6
NEG_INF = -1e30
LOG2E = 1.4426950408889634

ATT_WIDTH = ATT_HEADS * ATT_HEAD_DIM
ATT_KV_WIDTH = ATT_KV_HEADS * ATT_HEAD_DIM
RET_QK_WIDTH = RET_HEADS * RET_QK_DIM
RET_WIDTH = RET_HEADS * RET_V_DIM
MIX_WIDTH = ATT_WIDTH + RET_WIDTH
IN_WIDTH = 2 * ATT_WIDTH + 2 * ATT_KV_WIDTH + 2 * RET_QK_WIDTH + 2 * RET_WIDTH

OFF_AQ = 0
OFF_AK = OFF_AQ + ATT_WIDTH
OFF_AV = OFF_AK + ATT_KV_WIDTH
OFF_AZ = OFF_AV + ATT_KV_WIDTH
OFF_RQ = OFF_AZ + ATT_WIDTH
OFF_RK = OFF_RQ + RET_QK_WIDTH
OFF_RV = OFF_RK + RET_QK_WIDTH
OFF_RZ = OFF_RV + RET_WIDTH

LANES = 128
MXU_COLS = 256
N_MXU = 2
PROJ_CHUNK = N_MXU * MXU_COLS
HEADS_PER_TILE = LANES // ATT_HEAD_DIM
ATT_PAIRS = ATT_HEADS // HEADS_PER_TILE
RET_PAIRS = RET_HEADS // HEADS_PER_TILE
GROUP = ATT_HEADS // ATT_KV_HEADS

SEQ_TILE = 1024
SUB_TILE = 256
N_SUB = SEQ_TILE // SUB_TILE
BLKS_PER_SUB = SUB_TILE // BLK
IN_CHUNKS = -(-IN_WIDTH // PROJ_CHUNK)
OUT_CHUNKS = D_MODEL // PROJ_CHUNK
WEIGHT_SLAB = 256
CAST_STEPS = D_MODEL // WEIGHT_SLAB
VMEM_LIMIT_BYTES = 56 * 1024 * 1024

F32 = jnp.float32
BF16 = jnp.bfloat16


def _silu(z):
    hz = 0.5 * z
    return hz + hz * jnp.tanh(hz)


def _interleave(units, fillers):
    done = 0
    for k, unit in enumerate(units):
        unit()
        due = -(-(k + 1) * len(fillers) // len(units))
        while done < due:
            fillers[done]()
            done += 1
    for f in fillers[done:]:
        f()


def _layer_kernel(tiles_per_seq, sinks_ref, x_ref, xnext_ref, ng_ref, win_slab_ref, gng_ref, wout_slab_ref, fg_ref,
                  cos_ref, sin_ref, din_ref, qdec_ref, kdec_ref, cdec_ref,
                  o_ref, state_ref, kprev_ref, vprev_ref, pfirst_ref, win_ref, wout_ref, *mix_refs):
    step = pl.program_id(0)

    @pl.when(step < CAST_STEPS)
    def _():
        rows = pl.ds(pl.multiple_of(step * WEIGHT_SLAB, WEIGHT_SLAB), WEIGHT_SLAB)
        win_ref[rows, :] = win_slab_ref[...].astype(BF16)
        wout_ref[rows, :] = wout_slab_ref[...].astype(BF16)

    @pl.when(step >= CAST_STEPS)
    def _():
        _tile_step(step - CAST_STEPS, tiles_per_seq, sinks_ref, x_ref, xnext_ref, ng_ref, win_ref, gng_ref, wout_ref,
                   fg_ref, cos_ref, sin_ref, din_ref, qdec_ref, kdec_ref, cdec_ref,
                   o_ref, state_ref, kprev_ref, vprev_ref, pfirst_ref, mix_refs)


def _tile_step(t, tiles_per_seq, sinks_ref, x_ref, xnext_ref, ng_ref, win_ref, gng_ref, wout_ref, fg_ref,
               cos_ref, sin_ref, din_ref, qdec_ref, kdec_ref, cdec_ref,
               o_ref, state_ref, kprev_ref, vprev_ref, pfirst_ref, mix_refs):
    j = lax.rem(t, tiles_per_seq)

    def project_first_subtile(src_ref):
        held = {}

        def chunk(c):
            def run():
                if not held:
                    xs = src_ref[0]
                    ms = jnp.mean(xs * xs, axis=-1, keepdims=True)
                    held["h"] = (xs * lax.rsqrt(ms + RMS_EPS) * ng_ref[...]).astype(BF16)
                c0, c1 = c * PROJ_CHUNK, min((c + 1) * PROJ_CHUNK, IN_WIDTH)
                pfirst_ref[:, c0:c1] = jnp.dot(held["h"], win_ref[:, c0:c1], preferred_element_type=F32)
            return run
        return [chunk(c) for c in range(IN_CHUNKS)]

    @pl.when(t == 0)
    def _():
        for f in project_first_subtile(x_ref.at[:, 0:SUB_TILE, :]):
            f()

    @pl.when(j == 0)
    def _():
        state_ref[...] = jnp.zeros_like(state_ref)
        kprev_ref[...] = jnp.zeros_like(kprev_ref)
        vprev_ref[...] = jnp.zeros_like(vprev_ref)

    lane = lax.broadcasted_iota(jnp.int32, (BLK, LANES), 1)
    lo_half = lane < ATT_HEAD_DIM
    hi_half = jnp.logical_not(lo_half)
    even_lane = (lane & 1) == 0
    lane2 = lax.broadcasted_iota(jnp.int32, (2 * BLK, LANES), 1)
    lo_half2 = lane2 < ATT_HEAD_DIM
    qi = lax.broadcasted_iota(jnp.int32, (BLK, 2 * BLK), 0)
    kj = lax.broadcasted_iota(jnp.int32, (BLK, 2 * BLK), 1)
    diff = qi + BLK - kj
    band = (diff >= 0) & (diff < WINDOW)
    in_cur = kj >= BLK
    ones_lo = jnp.where(lo_half2, 1.0, 0.0).astype(BF16)
    ones_hi = jnp.where(lo_half2, 0.0, 1.0).astype(BF16)
    zeros_bf = jnp.zeros((BLK, LANES), BF16)
    row_lo = lax.broadcasted_iota(jnp.int32, (LANES, LANES), 0) < RET_QK_DIM

    proj_chunks = [dict() for _ in range(N_SUB)]
    out_chunks = [dict() for _ in range(N_SUB)]
    normed = {}
    carry = {}
    att_rhs = {}

    def proj_tile(s, i, col):
        if s == 0:
            return pfirst_ref[i * BLK:(i + 1) * BLK, col:col + LANES]
        chunk, off = divmod(col, PROJ_CHUNK)
        return proj_chunks[s][chunk][i * BLK:(i + 1) * BLK, off:off + LANES]

    def in_proj_chunk(s, c):
        def run():
            if s not in normed:
                xs = x_ref[0, s * SUB_TILE:(s + 1) * SUB_TILE, :]
                ms = jnp.mean(xs * xs, axis=-1, keepdims=True)
                normed[s] = (xs * lax.rsqrt(ms + RMS_EPS) * ng_ref[...]).astype(BF16)
            proj_chunks[s][c] = jnp.dot(normed[s], win_ref[:, c * PROJ_CHUNK:min((c + 1) * PROJ_CHUNK, IN_WIDTH)],
                                        preferred_element_type=F32)
        return run

    def out_proj_chunk(s, c):
        def run():
            out_chunks[s][c] = jnp.dot(mix_refs[s][...], wout_ref[:, c * PROJ_CHUNK:(c + 1) * PROJ_CHUNK],
                                       preferred_element_type=F32)
        return run

    def finish(s, r0=0, r1=SUB_TILE):
        def run():
            rows = slice(s * SUB_TILE + r0, s * SUB_TILE + r1)
            y = x_ref[0, rows, :] + jnp.concatenate([out_chunks[s][c][r0:r1] for c in range(OUT_CHUNKS)], axis=1)
            ms = jnp.mean(y * y, axis=-1, keepdims=True)
            o_ref[0, rows, :] = y * lax.rsqrt(ms + RMS_EPS) * fg_ref[...]
        return run

    def kv_prep(s, i):
        if s == 0 and i == 0:
            carry["k"] = [kprev_ref[g] for g in range(ATT_KV_HEADS)]
            carry["v"] = [vprev_ref[g] for g in range(ATT_KV_HEADS)]
        k_cur = proj_tile(s, i, OFF_AK)
        v_cur = proj_tile(s, i, OFF_AV)
        v_sw = pltpu.roll(v_cur, ATT_HEAD_DIM, 1)
        k_t = k_cur.T
        kdup_cur = [jnp.concatenate([k_t[g * ATT_HEAD_DIM:(g + 1) * ATT_HEAD_DIM]] * HEADS_PER_TILE,
                                    axis=0).astype(BF16) for g in range(ATT_KV_HEADS)]
        vdup_cur = [jnp.where(lo_half, v_cur, v_sw).astype(BF16),
                    jnp.where(lo_half, v_sw, v_cur).astype(BF16)]
        per_head = []
        for g in range(ATT_KV_HEADS):
            kdup = jnp.concatenate([carry["k"][g], kdup_cur[g]], axis=1)
            vdup = jnp.concatenate([carry["v"][g], vdup_cur[g]], axis=0)
            zero2 = jnp.zeros_like(vdup)
            rhs_v = jnp.concatenate([
                jnp.concatenate([jnp.where(lo_half2, vdup, zero2), ones_lo], axis=1),
                jnp.concatenate([jnp.where(lo_half2, zero2, vdup), ones_hi], axis=1),
            ], axis=0)
            per_head.append((kdup, rhs_v))
        att_rhs[(s, i)] = per_head
        carry["k"], carry["v"] = kdup_cur, vdup_cur

    stash = {}

    def att_scores(s, i):
        def run():
            kv_prep(s, i)
            valid = band & (in_cur | (j > 0)) if (s == 0 and i == 0) else band
            scores = []
            for g in range(ATT_KV_HEADS):
                kdup, _ = att_rhs[(s, i)][g]
                q_rows = []
                for p in range(g * GROUP // HEADS_PER_TILE, (g + 1) * GROUP // HEADS_PER_TILE):
                    q2 = (proj_tile(s, i, OFF_AQ + p * LANES) * (ATT_HEAD_DIM ** -0.5 * LOG2E)).astype(BF16)
                    q_rows += [jnp.where(lo_half, q2, zeros_bf), jnp.where(hi_half, q2, zeros_bf)]
                sc_g = jnp.dot(jnp.concatenate(q_rows, axis=0), kdup, preferred_element_type=F32)
                scores += [jnp.where(valid, sc_g[h * BLK:(h + 1) * BLK], NEG_INF) for h in range(GROUP)]
            stash[("sc", s, i)] = scores
        return run

    def att_softmax(s, i):
        def run():
            probs, sink_terms = [], []
            for head, sc in enumerate(stash.pop(("sc", s, i))):
                sink = sinks_ref[head] * LOG2E
                m = jnp.maximum(jnp.max(sc, axis=-1, keepdims=True), sink)
                probs.append(jnp.exp2(sc - m).astype(BF16))
                sink_terms.append(jnp.exp2(sink - m))
            stash[("p", s, i)] = (probs, sink_terms)
        return run

    def att_values(s, i):
        def run():
            probs, sink_terms = stash.pop(("p", s, i))
            for p in range(ATT_PAIRS):
                _, rhs_v = att_rhs[(s, i)][p // (GROUP // HEADS_PER_TILE)]
                h0 = p * HEADS_PER_TILE
                lhs = jnp.concatenate(probs[h0:h0 + HEADS_PER_TILE], axis=1)
                res = jnp.dot(lhs, rhs_v, preferred_element_type=F32)
                den = res[:, LANES:] + jnp.where(lo_half, sink_terms[h0], sink_terms[h0 + 1])
                a2 = res[:, :LANES] / den
                z2 = proj_tile(s, i, OFF_AZ + p * LANES)
                mix_refs[s][i * BLK:(i + 1) * BLK, p * LANES:(p + 1) * LANES] = (a2 * _silu(z2)).astype(BF16)
        return run

    def ret_scores(s, i):
        def run():
            r0 = (s * BLKS_PER_SUB + i) * BLK
            cos = cos_ref[r0:r0 + BLK, :]
            sin = sin_ref[r0:r0 + BLK, :]

            def rotate(t):
                swapped = jnp.where(even_lane, pltpu.roll(t, LANES - 1, 1), pltpu.roll(t, 1, 1))
                return t * cos + swapped * sin

            per_pair = []
            for p in range(RET_PAIRS):
                q2 = rotate(proj_tile(s, i, OFF_RQ + p * LANES))
                k2 = rotate(proj_tile(s, i, OFF_RK + p * LANES)) * (RET_QK_DIM ** -0.5)
                q2b = q2.astype(BF16)
                qd2b = (q2 * qdec_ref[p]).astype(BF16)
                k2t = k2.T
                k2tb = k2t.astype(BF16)
                kd2tb = (k2t * kdec_ref[p]).astype(BF16)
                rhs_qk = jnp.concatenate([jnp.where(row_lo, k2tb, zeros_bf),
                                          jnp.where(row_lo, zeros_bf, k2tb)], axis=1)
                sc2 = jnp.dot(q2b, rhs_qk, preferred_element_type=F32)
                lhs = []
                for e in range(HEADS_PER_TILE):
                    sel = lo_half if e == 0 else hi_half
                    sc = sc2[:, e * BLK:(e + 1) * BLK] * din_ref[p * HEADS_PER_TILE + e]
                    lhs.append(jnp.concatenate([sc.astype(BF16), jnp.where(sel, qd2b, zeros_bf)], axis=1))
                per_pair.append((lhs, kd2tb))
            stash[("ret", s, i)] = per_pair
        return run

    def ret_outputs(s, i):
        def run():
            per_pair = stash.pop(("ret", s, i))
            for p in range(RET_PAIRS):
                lhs, kd2tb = per_pair[p]
                st = state_ref[p * LANES:(p + 1) * LANES, :]
                st_b = st.astype(BF16)
                v_heads = [proj_tile(s, i, OFF_RV + (p * HEADS_PER_TILE + e) * LANES).astype(BF16)
                           for e in range(HEADS_PER_TILE)]
                for e in range(HEADS_PER_TILE):
                    head = p * HEADS_PER_TILE + e
                    rhs = jnp.concatenate([v_heads[e], st_b], axis=0)
                    o = jnp.dot(lhs[e], rhs, preferred_element_type=F32)
                    mu = jnp.mean(o, axis=-1, keepdims=True)
                    oc = o - mu
                    var = jnp.mean(oc * oc, axis=-1, keepdims=True)
                    on = oc * lax.rsqrt(var + GN_EPS) * gng_ref[:, head * LANES:(head + 1) * LANES]
                    rz = proj_tile(s, i, OFF_RZ + head * LANES)
                    mc = ATT_WIDTH + head * LANES
                    mix_refs[s][i * BLK:(i + 1) * BLK, mc:mc + LANES] = (on * _silu(rz)).astype(BF16)
                kv = jnp.dot(kd2tb, jnp.concatenate(v_heads, axis=1),
                             preferred_element_type=F32)
                kv_sel = jnp.where(row_lo, kv[:, :LANES], kv[:, LANES:])
                state_ref[p * LANES:(p + 1) * LANES, :] = st * cdec_ref[p * LANES:(p + 1) * LANES, :] + kv_sel
        return run

    def mixer_units(s):
        units = []
        for stage in (att_scores, ret_scores, att_softmax, ret_outputs, att_values):
            units += [stage(s, i) for i in range(BLKS_PER_SUB)]
        return units

    for s in range(N_SUB):
        fillers = []
        if s >= 1:
            fillers += [out_proj_chunk(s - 1, c) for c in range(OUT_CHUNKS)]
            fillers += [finish(s - 1, i * BLK, (i + 1) * BLK) for i in range(BLKS_PER_SUB)]
        if s + 1 < N_SUB:
            fillers += [in_proj_chunk(s + 1, c) for c in range(IN_CHUNKS)]
        else:
            fillers += project_first_subtile(xnext_ref)
        if s == 0:
            fillers.pop(0)()
        _interleave(mixer_units(s), fillers)
    for c in range(OUT_CHUNKS):
        out_proj_chunk(N_SUB - 1, c)()
    finish(N_SUB - 1)()

    for g in range(ATT_KV_HEADS):
        kprev_ref[g] = carry["k"][g]
        vprev_ref[g] = carry["v"][g]


def _retention_tables(seq):
    pos = np.arange(seq, dtype=np.float64)
    theta = 1.0 / (ROT_BASE ** np.linspace(0.0, 1.0, RET_QK_DIM // 2))
    ang = pos[:, None] * theta[None, :]
    cos, sin = np.cos(ang), np.sin(ang)
    cos_l = np.tile(np.repeat(cos, 2, axis=1), (1, HEADS_PER_TILE))
    sign = np.tile(np.array([-1.0, 1.0]), RET_QK_DIM // 2)
    sin_l = np.tile(np.repeat(sin, 2, axis=1) * sign[None, :], (1, HEADS_PER_TILE))

    log_gamma = np.log(1.0 - 2.0 ** (-5.0 - np.arange(RET_HEADS, dtype=np.float64)))
    idx = np.arange(BLK, dtype=np.float64)
    rel = idx[:, None] - idx[None, :]
    decay_in = np.where(rel >= 0, np.exp(log_gamma[:, None, None] * np.maximum(rel, 0.0)), 0.0)
    k_dec = np.exp(log_gamma[:, None] * (BLK - 1 - idx)[None, :])
    q_dec = np.exp(log_gamma[:, None] * (idx + 1)[None, :])
    chunk_decay = np.exp(log_gamma * BLK)

    def per_pair(dec):
        t = np.repeat(dec.T[:, :, None], RET_QK_DIM, axis=2)
        t = t.reshape(BLK, RET_PAIRS, LANES)
        return np.transpose(t, (1, 0, 2))

    cdec = np.broadcast_to(np.repeat(chunk_decay, RET_QK_DIM)[:, None],
                           (RET_QK_WIDTH, RET_V_DIM))
    k_dec_t = np.transpose(per_pair(k_dec), (0, 2, 1))
    tables = (cos_l, sin_l, decay_in, per_pair(q_dec), k_dec_t, cdec)
    return tuple(jnp.asarray(np.ascontiguousarray(a, dtype=np.float32)) for a in tables)


def _full(shape):
    return pl.BlockSpec(shape, lambda step: (0,) * len(shape))


@jax.jit
def kernel(x, norm_g, w_in, att_sinks, ret_gn_g, w_out, final_g):
    batch, seq, d = x.shape
    depth = w_in.shape[0]
    assert depth == 1 and d == D_MODEL and seq % SEQ_TILE == 0
    cos_l, sin_l, decay_in, qdec, kdec, cdec = _retention_tables(seq)
    tile = SEQ_TILE
    n_seq = seq // tile
    n_tiles = batch * n_seq

    def tile_of(step):
        return jnp.maximum(step - CAST_STEPS, 0)

    def this_tile(step):
        t = tile_of(step)
        return (t // n_seq, t % n_seq, 0)

    def next_first_subtile(step):
        u = jnp.minimum(tile_of(step) + 1, n_tiles - 1)
        return (u // n_seq, (u % n_seq) * N_SUB, 0)

    def weight_slab(step):
        return (jnp.minimum(step, CAST_STEPS - 1), 0)

    call = pl.pallas_call(
        functools.partial(_layer_kernel, n_seq),
        grid=(CAST_STEPS + n_tiles,),
        in_specs=[
            pl.BlockSpec(memory_space=pltpu.SMEM),
            pl.BlockSpec((1, tile, D_MODEL), this_tile),
            pl.BlockSpec((1, SUB_TILE, D_MODEL), next_first_subtile),
            _full((1, D_MODEL)),
            pl.BlockSpec((WEIGHT_SLAB, IN_WIDTH), weight_slab),
            _full((1, RET_WIDTH)),
            pl.BlockSpec((WEIGHT_SLAB, D_MODEL), weight_slab),
            _full((1, D_MODEL)),
            pl.BlockSpec((tile, LANES), lambda step: (tile_of(step) % n_seq, 0)),
            pl.BlockSpec((tile, LANES), lambda step: (tile_of(step) % n_seq, 0)),
            _full((RET_HEADS, BLK, BLK)),
            _full((RET_PAIRS, BLK, LANES)),
            _full((RET_PAIRS, BLK, LANES)),
            _full((RET_QK_WIDTH, RET_V_DIM)),
        ],
        out_specs=pl.BlockSpec((1, tile, D_MODEL), this_tile),
        out_shape=jax.ShapeDtypeStruct(x.shape, x.dtype),
        scratch_shapes=[
            pltpu.VMEM((RET_QK_WIDTH, RET_V_DIM), F32),
            pltpu.VMEM((ATT_KV_HEADS, BLK, LANES), BF16),
            pltpu.VMEM((ATT_KV_HEADS, BLK, LANES), BF16),
            pltpu.VMEM((SUB_TILE, IN_WIDTH), F32),
            pltpu.VMEM((D_MODEL, IN_WIDTH), BF16),
            pltpu.VMEM((MIX_WIDTH, D_MODEL), BF16),
        ] + [pltpu.VMEM((SUB_TILE, MIX_WIDTH), BF16)] * N_SUB,
        compiler_params=pltpu.CompilerParams(
            dimension_semantics=("arbitrary",),
            vmem_limit_bytes=VMEM_LIMIT_BYTES),
        name="hymba_layer",
    )
    return call(att_sinks[0], x, x, norm_g[0][None, :], w_in[0],
                ret_gn_g[0][None, :], w_out[0], final_g[None, :],
                cos_l, sin_l, decay_in, qdec, kdec, cdec)
```

```python
import functools

import jax
import jax.numpy as jnp
import numpy as np
from jax import lax
from jax.experimental import pallas as pl
from jax.experimental.pallas import tpu as pltpu

D_MODEL = 1024
ATT_HEADS = 8
ATT_KV_HEADS = 2
ATT_HEAD_DIM = 64
WINDOW = 128
BLK = 128
RET_HEADS = 4
RET_QK_DIM = 64
RET_V_DIM = 128
ROT_BASE = 10000.0
RMS_EPS = 1e-6
GN_EPS = 1e-6
NEG_INF = -1e30
LOG2E = 1.4426950408889634

ATT_WIDTH = ATT_HEADS * ATT_HEAD_DIM
ATT_KV_WIDTH = ATT_KV_HEADS * ATT_HEAD_DIM
RET_QK_WIDTH = RET_HEADS * RET_QK_DIM
RET_WIDTH = RET_HEADS * RET_V_DIM
MIX_WIDTH = ATT_WIDTH + RET_WIDTH
IN_WIDTH = 2 * ATT_WIDTH + 2 * ATT_KV_WIDTH + 2 * RET_QK_WIDTH + 2 * RET_WIDTH

OFF_AQ = 0
OFF_AK = OFF_AQ + ATT_WIDTH
OFF_AV = OFF_AK + ATT_KV_WIDTH
OFF_AZ = OFF_AV + ATT_KV_WIDTH
OFF_RQ = OFF_AZ + ATT_WIDTH
OFF_RK = OFF_RQ + RET_QK_WIDTH
OFF_RV = OFF_RK + RET_QK_WIDTH
OFF_RZ = OFF_RV + RET_WIDTH

LANES = 128
MXU_COLS = 256
N_MXU = 2
PROJ_CHUNK = N_MXU * MXU_COLS
HEADS_PER_TILE = LANES // ATT_HEAD_DIM
ATT_PAIRS = ATT_HEADS // HEADS_PER_TILE
RET_PAIRS = RET_HEADS // HEADS_PER_TILE
GROUP = ATT_HEADS // ATT_KV_HEADS

SEQ_TILE = 1024
SUB_TILE = 256
N_SUB = SEQ_TILE // SUB_TILE
BLKS_PER_SUB = SUB_TILE // BLK
IN_CHUNKS = -(-IN_WIDTH // PROJ_CHUNK)
OUT_CHUNKS = D_MODEL // PROJ_CHUNK
WEIGHT_SLAB = 256
CAST_STEPS = D_MODEL // WEIGHT_SLAB
VMEM_LIMIT_BYTES = 56 * 1024 * 1024

F32 = jnp.float32
BF16 = jnp.bfloat16


def _silu(z):
    hz = 0.5 * z
    return hz + hz * jnp.tanh(hz)


def _interleave(units, fillers):
    done = 0
    for k, unit in enumerate(units):
        unit()
        due = -(-(k + 1) * len(fillers) // len(units))
        while done < due:
            fillers[done]()
            done += 1
    for f in fillers[done:]:
        f()


def _layer_kernel(tiles_per_seq, sinks_ref, x_ref, xnext_ref, ng_ref, win_slab_ref, gng_ref, wout_slab_ref, fg_ref,
                  cos_ref, sin_ref, din_ref, qdec_ref, kdec_ref, cdec_ref,
                  o_ref, state_ref, kprev_ref, vprev_ref, pfirst_ref, win_ref, wout_ref, *mix_refs):
    step = pl.program_id(0)

    @pl.when(step < CAST_STEPS)
    def _():
        rows = pl.ds(pl.multiple_of(step * WEIGHT_SLAB, WEIGHT_SLAB), WEIGHT_SLAB)
        win_ref[rows, :] = win_slab_ref[...].astype(BF16)
        wout_ref[rows, :] = wout_slab_ref[...].astype(BF16)

    @pl.when(step >= CAST_STEPS)
    def _():
        _tile_step(step - CAST_STEPS, tiles_per_seq, sinks_ref, x_ref, xnext_ref, ng_ref, win_ref, gng_ref, wout_ref,
                   fg_ref, cos_ref, sin_ref, din_ref, qdec_ref, kdec_ref, cdec_ref,
                   o_ref, state_ref, kprev_ref, vprev_ref, pfirst_ref, mix_refs)


def _tile_step(t, tiles_per_seq, sinks_ref, x_ref, xnext_ref, ng_ref, win_ref, gng_ref, wout_ref, fg_ref,
               cos_ref, sin_ref, din_ref, qdec_ref, kdec_ref, cdec_ref,
               o_ref, state_ref, kprev_ref, vprev_ref, pfirst_ref, mix_refs):
    j = lax.rem(t, tiles_per_seq)

    def project_first_subtile(src_ref):
        held = {}

        def chunk(c):
            def run():
                if not held:
                    xs = src_ref[0]
                    ms = jnp.mean(xs * xs, axis=-1, keepdims=True)
                    held["h"] = (xs * lax.rsqrt(ms + RMS_EPS) * ng_ref[...]).astype(BF16)
                c0, c1 = c * PROJ_CHUNK, min((c + 1) * PROJ_CHUNK, IN_WIDTH)
                pfirst_ref[:, c0:c1] = jnp.dot(held["h"], win_ref[:, c0:c1], preferred_element_type=F32)
            return run
        return [chunk(c) for c in range(IN_CHUNKS)]

    @pl.when(t == 0)
    def _():
        for f in project_first_subtile(x_ref.at[:, 0:SUB_TILE, :]):
            f()

    @pl.when(j == 0)
    def _():
        state_ref[...] = jnp.zeros_like(state_ref)
        kprev_ref[...] = jnp.zeros_like(kprev_ref)
        vprev_ref[...] = jnp.zeros_like(vprev_ref)

    lane = lax.broadcasted_iota(jnp.int32, (BLK, LANES), 1)
    lo_half = lane < ATT_HEAD_DIM
    hi_half = jnp.logical_not(lo_half)
    even_lane = (lane & 1) == 0
    lane2 = lax.broadcasted_iota(jnp.int32, (2 * BLK, LANES), 1)
    lo_half2 = lane2 < ATT_HEAD_DIM
    qi = lax.broadcasted_iota(jnp.int32, (BLK, 2 * BLK), 0)
    kj = lax.broadcasted_iota(jnp.int32, (BLK, 2 * BLK), 1)
    diff = qi + BLK - kj
    band = (diff >= 0) & (diff < WINDOW)
    in_cur = kj >= BLK
    ones_lo = jnp.where(lo_half2, 1.0, 0.0).astype(BF16)
    ones_hi = jnp.where(lo_half2, 0.0, 1.0).astype(BF16)
    zeros_bf = jnp.zeros((BLK, LANES), BF16)
    row_lo = lax.broadcasted_iota(jnp.int32, (LANES, LANES), 0) < RET_QK_DIM

    proj_chunks = [dict() for _ in range(N_SUB)]
    out_chunks = [dict() for _ in range(N_SUB)]
    normed = {}
    carry = {}
    att_rhs = {}

    def proj_tile(s, i, col):
        if s == 0:
            return pfirst_ref[i * BLK:(i + 1) * BLK, col:col + LANES]
        chunk, off = divmod(col, PROJ_CHUNK)
        return proj_chunks[s][chunk][i * BLK:(i + 1) * BLK, off:off + LANES]

    def in_proj_chunk(s, c):
        def run():
            if s not in normed:
                xs = x_ref[0, s * SUB_TILE:(s + 1) * SUB_TILE, :]
                ms = jnp.mean(xs * xs, axis=-1, keepdims=True)
                normed[s] = (xs * lax.rsqrt(ms + RMS_EPS) * ng_ref[...]).astype(BF16)
            proj_chunks[s][c] = jnp.dot(normed[s], win_ref[:, c * PROJ_CHUNK:min((c + 1) * PROJ_CHUNK, IN_WIDTH)],
                                        preferred_element_type=F32)
        return run

    def out_proj_chunk(s, c):
        def run():
            out_chunks[s][c] = jnp.dot(mix_refs[s][...], wout_ref[:, c * PROJ_CHUNK:(c + 1) * PROJ_CHUNK],
                                       preferred_element_type=F32)
        return run

    def finish(s):
        def run():
            rows = slice(s * SUB_TILE, (s + 1) * SUB_TILE)
            y = x_ref[0, rows, :] + jnp.concatenate([out_chunks[s][c] for c in range(OUT_CHUNKS)], axis=1)
            ms = jnp.mean(y * y, axis=-1, keepdims=True)
            o_ref[0, rows, :] = y * lax.rsqrt(ms + RMS_EPS) * fg_ref[...]
        return run

    def kv_prep(s, i):
        if s == 0 and i == 0:
            carry["k"] = [kprev_ref[g] for g in range(ATT_KV_HEADS)]
            carry["v"] = [vprev_ref[g] for g in range(ATT_KV_HEADS)]
        k_cur = proj_tile(s, i, OFF_AK)
        v_cur = proj_tile(s, i, OFF_AV)
        v_sw = pltpu.roll(v_cur, ATT_HEAD_DIM, 1)
        k_t = k_cur.T
        kdup_cur = [jnp.concatenate([k_t[g * ATT_HEAD_DIM:(g + 1) * ATT_HEAD_DIM]] * HEADS_PER_TILE,
                                    axis=0).astype(BF16) for g in range(ATT_KV_HEADS)]
        vdup_cur = [jnp.where(lo_half, v_cur, v_sw).astype(BF16),
                    jnp.where(lo_half, v_sw, v_cur).astype(BF16)]
        per_head = []
        for g in range(ATT_KV_HEADS):
            kdup = jnp.concatenate([carry["k"][g], kdup_cur[g]], axis=1)
            vdup = jnp.concatenate([carry["v"][g], vdup_cur[g]], axis=0)
            zero2 = jnp.zeros_like(vdup)
            rhs_v = jnp.concatenate([
                jnp.concatenate([jnp.where(lo_half2, vdup, zero2), ones_lo], axis=1),
                jnp.concatenate([jnp.where(lo_half2, zero2, vdup), ones_hi], axis=1),
            ], axis=0)
            per_head.append((kdup, rhs_v))
        att_rhs[(s, i)] = per_head
        carry["k"], carry["v"] = kdup_cur, vdup_cur

    stash = {}

    def att_scores(s, i):
        def run():
            kv_prep(s, i)
            valid = band & (in_cur | (j > 0)) if (s == 0 and i == 0) else band
            scores = []
            for g in range(ATT_KV_HEADS):
                kdup, _ = att_rhs[(s, i)][g]
                q_rows = []
                for p in range(g * GROUP // HEADS_PER_TILE, (g + 1) * GROUP // HEADS_PER_TILE):
                    q2 = (proj_tile(s, i, OFF_AQ + p * LANES) * (ATT_HEAD_DIM ** -0.5 * LOG2E)).astype(BF16)
                    q_rows += [jnp.where(lo_half, q2, zeros_bf), jnp.where(hi_half, q2, zeros_bf)]
                sc_g = jnp.dot(jnp.concatenate(q_rows, axis=0), kdup, preferred_element_type=F32)
                scores += [jnp.where(valid, sc_g[h * BLK:(h + 1) * BLK], NEG_INF) for h in range(GROUP)]
            stash[("sc", s, i)] = scores
        return run

    def att_softmax(s, i):
        def run():
            probs, sink_terms = [], []
            for head, sc in enumerate(stash.pop(("sc", s, i))):
                sink = sinks_ref[head] * LOG2E
                m = jnp.maximum(jnp.max(sc, axis=-1, keepdims=True), sink)
                probs.append(jnp.exp2(sc - m).astype(BF16))
                sink_terms.append(jnp.exp2(sink - m))
            stash[("p", s, i)] = (probs, sink_terms)
        return run

    def att_values(s, i):
        def run():
            probs, sink_terms = stash.pop(("p", s, i))
            for p in range(ATT_PAIRS):
                _, rhs_v = att_rhs[(s, i)][p // (GROUP // HEADS_PER_TILE)]
                h0 = p * HEADS_PER_TILE
                lhs = jnp.concatenate(probs[h0:h0 + HEADS_PER_TILE], axis=1)
                res = jnp.dot(lhs, rhs_v, preferred_element_type=F32)
                den = res[:, LANES:] + jnp.where(lo_half, sink_terms[h0], sink_terms[h0 + 1])
                a2 = res[:, :LANES] / den
                z2 = proj_tile(s, i, OFF_AZ + p * LANES)
                mix_refs[s][i * BLK:(i + 1) * BLK, p * LANES:(p + 1) * LANES] = (a2 * _silu(z2)).astype(BF16)
        return run

    def ret_scores(s, i):
        def run():
            r0 = (s * BLKS_PER_SUB + i) * BLK
            cos = cos_ref[r0:r0 + BLK, :]
            sin = sin_ref[r0:r0 + BLK, :]

            def rotate(t):
                swapped = jnp.where(even_lane, pltpu.roll(t, LANES - 1, 1), pltpu.roll(t, 1, 1))
                return t * cos + swapped * sin

            per_pair = []
            for p in range(RET_PAIRS):
                q2 = rotate(proj_tile(s, i, OFF_RQ + p * LANES))
                k2 = rotate(proj_tile(s, i, OFF_RK + p * LANES)) * (RET_QK_DIM ** -0.5)
                q2b = q2.astype(BF16)
                qd2b = (q2 * qdec_ref[p]).astype(BF16)
                k2t = k2.T
                k2tb = k2t.astype(BF16)
                kd2tb = (k2t * kdec_ref[p]).astype(BF16)
                rhs_qk = jnp.concatenate([jnp.where(row_lo, k2tb, zeros_bf),
                                          jnp.where(row_lo, zeros_bf, k2tb)], axis=1)
                sc2 = jnp.dot(q2b, rhs_qk, preferred_element_type=F32)
                lhs = []
                for e in range(HEADS_PER_TILE):
                    sel = lo_half if e == 0 else hi_half
                    sc = sc2[:, e * BLK:(e + 1) * BLK] * din_ref[p * HEADS_PER_TILE + e]
                    lhs.append(jnp.concatenate([sc.astype(BF16), jnp.where(sel, qd2b, zeros_bf)], axis=1))
                per_pair.append((lhs, kd2tb))
            stash[("ret", s, i)] = per_pair
        return run

    def ret_outputs(s, i):
        def run():
            per_pair = stash.pop(("ret", s, i))
            for p in range(RET_PAIRS):
                lhs, kd2tb = per_pair[p]
                st = state_ref[p * LANES:(p + 1) * LANES, :]
                st_b = st.astype(BF16)
                v_heads = [proj_tile(s, i, OFF_RV + (p * HEADS_PER_TILE + e) * LANES).astype(BF16)
                           for e in range(HEADS_PER_TILE)]
                for e in range(HEADS_PER_TILE):
                    head = p * HEADS_PER_TILE + e
                    rhs = jnp.concatenate([v_heads[e], st_b], axis=0)
                    o = jnp.dot(lhs[e], rhs, preferred_element_type=F32)
                    mu = jnp.mean(o, axis=-1, keepdims=True)
                    oc = o - mu
                    var = jnp.mean(oc * oc, axis=-1, keepdims=True)
                    on = oc * lax.rsqrt(var + GN_EPS) * gng_ref[:, head * LANES:(head + 1) * LANES]
                    rz = proj_tile(s, i, OFF_RZ + head * LANES)
                    mc = ATT_WIDTH + head * LANES
                    mix_refs[s][i * BLK:(i + 1) * BLK, mc:mc + LANES] = (on * _silu(rz)).astype(BF16)
                kv = jnp.dot(kd2tb, jnp.concatenate(v_heads, axis=1),
                             preferred_element_type=F32)
                kv_sel = jnp.where(row_lo, kv[:, :LANES], kv[:, LANES:])
                state_ref[p * LANES:(p + 1) * LANES, :] = st * cdec_ref[p * LANES:(p + 1) * LANES, :] + kv_sel
        return run

    def mixer_units(s):
        units = []
        for stage in (att_scores, ret_scores, att_softmax, ret_outputs, att_values):
            units += [stage(s, i) for i in range(BLKS_PER_SUB)]
        return units

    for s in range(N_SUB):
        fillers = []
        if s >= 1:
            fillers += [out_proj_chunk(s - 1, c) for c in range(OUT_CHUNKS)] + [finish(s - 1)]
        if s + 1 < N_SUB:
            fillers += [in_proj_chunk(s + 1, c) for c in range(IN_CHUNKS)]
        else:
            fillers += project_first_subtile(xnext_ref)
        units = mixer_units(s)
        if s == 0:
            fillers.pop(0)()
            early = len(units) // 2 + 1
            _interleave(units[:early], fillers)
            units, fillers = units[early:], []
        _interleave(units, fillers)
    for c in range(OUT_CHUNKS):
        out_proj_chunk(N_SUB - 1, c)()
    finish(N_SUB - 1)()

    for g in range(ATT_KV_HEADS):
        kprev_ref[g] = carry["k"][g]
        vprev_ref[g] = carry["v"][g]


def _retention_tables(seq):
    pos = np.arange(seq, dtype=np.float64)
    theta = 1.0 / (ROT_BASE ** np.linspace(0.0, 1.0, RET_QK_DIM // 2))
    ang = pos[:, None] * theta[None, :]
    cos, sin = np.cos(ang), np.sin(ang)
    cos_l = np.tile(np.repeat(cos, 2, axis=1), (1, HEADS_PER_TILE))
    sign = np.tile(np.array([-1.0, 1.0]), RET_QK_DIM // 2)
    sin_l = np.tile(np.repeat(sin, 2, axis=1) * sign[None, :], (1, HEADS_PER_TILE))

    log_gamma = np.log(1.0 - 2.0 ** (-5.0 - np.arange(RET_HEADS, dtype=np.float64)))
    idx = np.arange(BLK, dtype=np.float64)
    rel = idx[:, None] - idx[None, :]
    decay_in = np.where(rel >= 0, np.exp(log_gamma[:, None, None] * np.maximum(rel, 0.0)), 0.0)
    k_dec = np.exp(log_gamma[:, None] * (BLK - 1 - idx)[None, :])
    q_dec = np.exp(log_gamma[:, None] * (idx + 1)[None, :])
    chunk_decay = np.exp(log_gamma * BLK)

    def per_pair(dec):
        t = np.repeat(dec.T[:, :, None], RET_QK_DIM, axis=2)
        t = t.reshape(BLK, RET_PAIRS, LANES)
        return np.transpose(t, (1, 0, 2))

    cdec = np.broadcast_to(np.repeat(chunk_decay, RET_QK_DIM)[:, None],
                           (RET_QK_WIDTH, RET_V_DIM))
    k_dec_t = np.transpose(per_pair(k_dec), (0, 2, 1))
    tables = (cos_l, sin_l, decay_in, per_pair(q_dec), k_dec_t, cdec)
    return tuple(jnp.asarray(np.ascontiguousarray(a, dtype=np.float32)) for a in tables)


def _full(shape):
    return pl.BlockSpec(shape, lambda step: (0,) * len(shape))


@jax.jit
def kernel(x, norm_g, w_in, att_sinks, ret_gn_g, w_out, final_g):
    batch, seq, d = x.shape
    depth = w_in.shape[0]
    assert depth == 1 and d == D_MODEL and seq % SEQ_TILE == 0
    cos_l, sin_l, decay_in, qdec, kdec, cdec = _retention_tables(seq)
    tile = SEQ_TILE
    n_seq = seq // tile
    n_tiles = batch * n_seq

    def tile_of(step):
        return jnp.maximum(step - CAST_STEPS, 0)

    def this_tile(step):
        t = tile_of(step)
        return (t // n_seq, t % n_seq, 0)

    def next_first_subtile(step):
        u = jnp.minimum(tile_of(step) + 1, n_tiles - 1)
        return (u // n_seq, (u % n_seq) * N_SUB, 0)

    def weight_slab(step):
        return (jnp.minimum(step, CAST_STEPS - 1), 0)

    call = pl.pallas_call(
        functools.partial(_layer_kernel, n_seq),
        grid=(CAST_STEPS + n_tiles,),
        in_specs=[
            pl.BlockSpec(memory_space=pltpu.SMEM),
            pl.BlockSpec((1, tile, D_MODEL), this_tile),
            pl.BlockSpec((1, SUB_TILE, D_MODEL), next_first_subtile),
            _full((1, D_MODEL)),
            pl.BlockSpec((WEIGHT_SLAB, IN_WIDTH), weight_slab),
            _full((1, RET_WIDTH)),
            pl.BlockSpec((WEIGHT_SLAB, D_MODEL), weight_slab),
            _full((1, D_MODEL)),
            pl.BlockSpec((tile, LANES), lambda step: (tile_of(step) % n_seq, 0)),
            pl.BlockSpec((tile, LANES), lambda step: (tile_of(step) % n_seq, 0)),
            _full((RET_HEADS, BLK, BLK)),
            _full((RET_PAIRS, BLK, LANES)),
            _full((RET_PAIRS, BLK, LANES)),
            _full((RET_QK_WIDTH, RET_V_DIM)),
        ],
        out_specs=pl.BlockSpec((1, tile, D_MODEL), this_tile),
        out_shape=jax.ShapeDtypeStruct(x.shape, x.dtype),
        scratch_shapes=[
            pltpu.VMEM((RET_QK_WIDTH, RET_V_DIM), F32),
            pltpu.VMEM((ATT_KV_HEADS, BLK, LANES), BF16),
            pltpu.VMEM((ATT_KV_HEADS, BLK, LANES), BF16),
            pltpu.VMEM((SUB_TILE, IN_WIDTH), F32),
            pltpu.VMEM((D_MODEL, IN_WIDTH), BF16),
            pltpu.VMEM((MIX_WIDTH, D_MODEL), BF16),
        ] + [pltpu.VMEM((SUB_TILE, MIX_WIDTH), BF16)] * N_SUB,
        compiler_params=pltpu.CompilerParams(
            dimension_semantics=("arbitrary",),
            vmem_limit_bytes=VMEM_LIMIT_BYTES),
        name="hymba_layer",
    )
    return call(att_sinks[0], x, x, norm_g[0][None, :], w_in[0],
                ret_gn_g[0][None, :], w_out[0], final_g[None, :],
                cos_l, sin_l, decay_in, qdec, kdec, cdec)
```

```python
import functools

import jax
import jax.numpy as jnp
import numpy as np
from jax import lax
from jax.experimental import pallas as pl
from jax.experimental.pallas import tpu as pltpu

D_MODEL = 1024
ATT_HEADS = 8
ATT_KV_HEADS = 2
ATT_HEAD_DIM = 64
WINDOW = 128
BLK = 128
RET_HEADS = 4
RET_QK_DIM = 64
RET_V_DIM = 128
ROT_BASE = 10000.0
RMS_EPS = 1e-6
GN_EPS = 1e-6
NEG_INF = -1e30
LOG2E = 1.4426950408889634

ATT_WIDTH = ATT_HEADS * ATT_HEAD_DIM
ATT_KV_WIDTH = ATT_KV_HEADS * ATT_HEAD_DIM
RET_QK_WIDTH = RET_HEADS * RET_QK_DIM
RET_WIDTH = RET_HEADS * RET_V_DIM
MIX_WIDTH = ATT_WIDTH + RET_WIDTH
IN_WIDTH = 2 * ATT_WIDTH + 2 * ATT_KV_WIDTH + 2 * RET_QK_WIDTH + 2 * RET_WIDTH

OFF_AQ = 0
OFF_AK = OFF_AQ + ATT_WIDTH
OFF_AV = OFF_AK + ATT_KV_WIDTH
OFF_AZ = OFF_AV + ATT_KV_WIDTH
OFF_RQ = OFF_AZ + ATT_WIDTH
OFF_RK = OFF_RQ + RET_QK_WIDTH
OFF_RV = OFF_RK + RET_QK_WIDTH
OFF_RZ = OFF_RV + RET_WIDTH

LANES = 128
MXU_COLS = 256
N_MXU = 2
PROJ_CHUNK = N_MXU * MXU_COLS
HEADS_PER_TILE = LANES // ATT_HEAD_DIM
ATT_PAIRS = ATT_HEADS // HEADS_PER_TILE
RET_PAIRS = RET_HEADS // HEADS_PER_TILE
GROUP = ATT_HEADS // ATT_KV_HEADS

SEQ_TILE = 1024
SUB_TILE = 256
N_SUB = SEQ_TILE // SUB_TILE
BLKS_PER_SUB = SUB_TILE // BLK
IN_CHUNKS = -(-IN_WIDTH // PROJ_CHUNK)
OUT_CHUNKS = D_MODEL // PROJ_CHUNK
WEIGHT_SLAB = 256
CAST_STEPS = D_MODEL // WEIGHT_SLAB
VMEM_LIMIT_BYTES = 56 * 1024 * 1024

F32 = jnp.float32
BF16 = jnp.bfloat16


def _silu(z):
    hz = 0.5 * z
    return hz + hz * jnp.tanh(hz)


Q_SCALE = ATT_HEAD_DIM ** -0.5 * LOG2E


def _narrow_slot(col):
    if OFF_AQ <= col < OFF_AQ + ATT_WIDTH:
        return col - OFF_AQ, Q_SCALE
    if OFF_RV <= col < OFF_RV + RET_WIDTH:
        return ATT_WIDTH + col - OFF_RV, None
    return None


def _narrow(tile, scale):
    return (tile if scale is None else tile * scale).astype(BF16)


def _interleave(units, fillers):
    done = 0
    for k, unit in enumerate(units):
        unit()
        due = -(-(k + 1) * len(fillers) // len(units))
        while done < due:
            fillers[done]()
            done += 1
    for f in fillers[done:]:
        f()


def _layer_kernel(tiles_per_seq, sinks_ref, x_ref, xnext_ref, ng_ref, win_slab_ref, gng_ref, wout_slab_ref, fg_ref,
                  cos_ref, sin_ref, din_ref, qdec_ref, kdec_ref, cdec_ref,
                  o_ref, state_ref, kprev_ref, vprev_ref, pfirst_ref, pnarrow_ref, win_ref, wout_ref, *mix_refs):
    step = pl.program_id(0)

    @pl.when(step < CAST_STEPS)
    def _():
        rows = pl.ds(pl.multiple_of(step * WEIGHT_SLAB, WEIGHT_SLAB), WEIGHT_SLAB)
        win_ref[rows, :] = win_slab_ref[...].astype(BF16)
        wout_ref[rows, :] = wout_slab_ref[...].astype(BF16)

    @pl.when(step >= CAST_STEPS)
    def _():
        _tile_step(step - CAST_STEPS, tiles_per_seq, sinks_ref, x_ref, xnext_ref, ng_ref, win_ref, gng_ref, wout_ref,
                   fg_ref, cos_ref, sin_ref, din_ref, qdec_ref, kdec_ref, cdec_ref,
                   o_ref, state_ref, kprev_ref, vprev_ref, pfirst_ref, pnarrow_ref, mix_refs)


def _tile_step(t, tiles_per_seq, sinks_ref, x_ref, xnext_ref, ng_ref, win_ref, gng_ref, wout_ref, fg_ref,
               cos_ref, sin_ref, din_ref, qdec_ref, kdec_ref, cdec_ref,
               o_ref, state_ref, kprev_ref, vprev_ref, pfirst_ref, pnarrow_ref, mix_refs):
    j = lax.rem(t, tiles_per_seq)

    def project_first_subtile(src_ref):
        held = {}

        def chunk(c):
            def run():
                if not held:
                    xs = src_ref[0]
                    ms = jnp.mean(xs * xs, axis=-1, keepdims=True)
                    held["h"] = (xs * lax.rsqrt(ms + RMS_EPS) * ng_ref[...]).astype(BF16)
                c0, c1 = c * PROJ_CHUNK, min((c + 1) * PROJ_CHUNK, IN_WIDTH)
                res = jnp.dot(held["h"], win_ref[:, c0:c1], preferred_element_type=F32)
                for col in range(c0, c1, LANES):
                    tile = res[:, col - c0:col - c0 + LANES]
                    if _narrow_slot(col) is None:
                        pfirst_ref[:, col:col + LANES] = tile
                    else:
                        slot, scale = _narrow_slot(col)
                        pnarrow_ref[:, slot:slot + LANES] = _narrow(tile, scale)
            return run
        return [chunk(c) for c in range(IN_CHUNKS)]

    narrow_vals = [dict() for _ in range(N_SUB)]

    def narrow_tile(s, i, col):
        slot, _ = _narrow_slot(col)
        if s == 0:
            return pnarrow_ref[i * BLK:(i + 1) * BLK, slot:slot + LANES]
        return narrow_vals[s][slot][i * BLK:(i + 1) * BLK]

    @pl.when(t == 0)
    def _():
        for f in project_first_subtile(x_ref.at[:, 0:SUB_TILE, :]):
            f()

    @pl.when(j == 0)
    def _():
        state_ref[...] = jnp.zeros_like(state_ref)
        kprev_ref[...] = jnp.zeros_like(kprev_ref)
        vprev_ref[...] = jnp.zeros_like(vprev_ref)

    lane = lax.broadcasted_iota(jnp.int32, (BLK, LANES), 1)
    lo_half = lane < ATT_HEAD_DIM
    hi_half = jnp.logical_not(lo_half)
    even_lane = (lane & 1) == 0
    lane2 = lax.broadcasted_iota(jnp.int32, (2 * BLK, LANES), 1)
    lo_half2 = lane2 < ATT_HEAD_DIM
    qi = lax.broadcasted_iota(jnp.int32, (BLK, 2 * BLK), 0)
    kj = lax.broadcasted_iota(jnp.int32, (BLK, 2 * BLK), 1)
    diff = qi + BLK - kj
    band = (diff >= 0) & (diff < WINDOW)
    in_cur = kj >= BLK
    ones_lo = jnp.where(lo_half2, 1.0, 0.0).astype(BF16)
    ones_hi = jnp.where(lo_half2, 0.0, 1.0).astype(BF16)
    zeros_bf = jnp.zeros((BLK, LANES), BF16)
    row_lo = lax.broadcasted_iota(jnp.int32, (LANES, LANES), 0) < RET_QK_DIM

    proj_chunks = [dict() for _ in range(N_SUB)]
    out_chunks = [dict() for _ in range(N_SUB)]
    normed = {}
    carry = {}
    att_rhs = {}

    def proj_tile(s, i, col):
        if s == 0:
            return pfirst_ref[i * BLK:(i + 1) * BLK, col:col + LANES]
        chunk, off = divmod(col, PROJ_CHUNK)
        return proj_chunks[s][chunk][i * BLK:(i + 1) * BLK, off:off + LANES]

    def in_proj_chunk(s, c):
        def run():
            if s not in normed:
                xs = x_ref[0, s * SUB_TILE:(s + 1) * SUB_TILE, :]
                ms = jnp.mean(xs * xs, axis=-1, keepdims=True)
                normed[s] = (xs * lax.rsqrt(ms + RMS_EPS) * ng_ref[...]).astype(BF16)
            c0, c1 = c * PROJ_CHUNK, min((c + 1) * PROJ_CHUNK, IN_WIDTH)
            res = jnp.dot(normed[s], win_ref[:, c0:c1], preferred_element_type=F32)
            proj_chunks[s][c] = res
            for col in range(c0, c1, LANES):
                if _narrow_slot(col) is not None:
                    slot, scale = _narrow_slot(col)
                    narrow_vals[s][slot] = _narrow(res[:, col - c0:col - c0 + LANES], scale)
        return run

    def out_proj_chunk(s, c):
        def run():
            out_chunks[s][c] = jnp.dot(mix_refs[s][...], wout_ref[:, c * PROJ_CHUNK:(c + 1) * PROJ_CHUNK],
                                       preferred_element_type=F32)
        return run

    def finish(s):
        def run():
            rows = slice(s * SUB_TILE, (s + 1) * SUB_TILE)
            y = x_ref[0, rows, :] + jnp.concatenate([out_chunks[s][c] for c in range(OUT_CHUNKS)], axis=1)
            ms = jnp.mean(y * y, axis=-1, keepdims=True)
            o_ref[0, rows, :] = y * lax.rsqrt(ms + RMS_EPS) * fg_ref[...]
        return run

    def kv_prep(s, i):
        if s == 0 and i == 0:
            carry["k"] = [kprev_ref[g] for g in range(ATT_KV_HEADS)]
            carry["v"] = [vprev_ref[g] for g in range(ATT_KV_HEADS)]
        k_cur = proj_tile(s, i, OFF_AK)
        v_cur = proj_tile(s, i, OFF_AV)
        v_sw = pltpu.roll(v_cur, ATT_HEAD_DIM, 1)
        k_t = k_cur.T
        kdup_cur = [jnp.concatenate([k_t[g * ATT_HEAD_DIM:(g + 1) * ATT_HEAD_DIM]] * HEADS_PER_TILE,
                                    axis=0).astype(BF16) for g in range(ATT_KV_HEADS)]
        vdup_cur = [jnp.where(lo_half, v_cur, v_sw).astype(BF16),
                    jnp.where(lo_half, v_sw, v_cur).astype(BF16)]
        per_head = []
        for g in range(ATT_KV_HEADS):
            kdup = jnp.concatenate([carry["k"][g], kdup_cur[g]], axis=1)
            vdup = jnp.concatenate([carry["v"][g], vdup_cur[g]], axis=0)
            zero2 = jnp.zeros_like(vdup)
            rhs_v = jnp.concatenate([
                jnp.concatenate([jnp.where(lo_half2, vdup, zero2), ones_lo], axis=1),
                jnp.concatenate([jnp.where(lo_half2, zero2, vdup), ones_hi], axis=1),
            ], axis=0)
            per_head.append((kdup, rhs_v))
        att_rhs[(s, i)] = per_head
        carry["k"], carry["v"] = kdup_cur, vdup_cur

    stash = {}

    def att_scores(s, i):
        def run():
            kv_prep(s, i)
            valid = band & (in_cur | (j > 0)) if (s == 0 and i == 0) else band
            scores = []
            for g in range(ATT_KV_HEADS):
                kdup, _ = att_rhs[(s, i)][g]
                q_rows = []
                for p in range(g * GROUP // HEADS_PER_TILE, (g + 1) * GROUP // HEADS_PER_TILE):
                    q2 = narrow_tile(s, i, OFF_AQ + p * LANES)
                    q_rows += [jnp.where(lo_half, q2, zeros_bf), jnp.where(hi_half, q2, zeros_bf)]
                sc_g = jnp.dot(jnp.concatenate(q_rows, axis=0), kdup, preferred_element_type=F32)
                scores += [jnp.where(valid, sc_g[h * BLK:(h + 1) * BLK], NEG_INF) for h in range(GROUP)]
            stash[("sc", s, i)] = scores
        return run

    def att_softmax(s, i):
        def run():
            probs, sink_terms = [], []
            for head, sc in enumerate(stash.pop(("sc", s, i))):
                sink = sinks_ref[head] * LOG2E
                m = jnp.maximum(jnp.max(sc, axis=-1, keepdims=True), sink)
                probs.append(jnp.exp2(sc - m).astype(BF16))
                sink_terms.append(jnp.exp2(sink - m))
            stash[("p", s, i)] = (probs, sink_terms)
        return run

    def att_values(s, i):
        def run():
            probs, sink_terms = stash.pop(("p", s, i))
            for p in range(ATT_PAIRS):
                _, rhs_v = att_rhs[(s, i)][p // (GROUP // HEADS_PER_TILE)]
                h0 = p * HEADS_PER_TILE
                lhs = jnp.concatenate(probs[h0:h0 + HEADS_PER_TILE], axis=1)
                res = jnp.dot(lhs, rhs_v, preferred_element_type=F32)
                den = res[:, LANES:] + jnp.where(lo_half, sink_terms[h0], sink_terms[h0 + 1])
                a2 = res[:, :LANES] / den
                z2 = proj_tile(s, i, OFF_AZ + p * LANES)
                mix_refs[s][i * BLK:(i + 1) * BLK, p * LANES:(p + 1) * LANES] = (a2 * _silu(z2)).astype(BF16)
        return run

    def ret_scores(s, i):
        def run():
            r0 = (s * BLKS_PER_SUB + i) * BLK
            cos = cos_ref[r0:r0 + BLK, :]
            sin = sin_ref[r0:r0 + BLK, :]

            def rotate(t):
                swapped = jnp.where(even_lane, pltpu.roll(t, LANES - 1, 1), pltpu.roll(t, 1, 1))
                return t * cos + swapped * sin

            per_pair = []
            for p in range(RET_PAIRS):
                q2 = rotate(proj_tile(s, i, OFF_RQ + p * LANES))
                k2 = rotate(proj_tile(s, i, OFF_RK + p * LANES)) * (RET_QK_DIM ** -0.5)
                q2b = q2.astype(BF16)
                qd2b = (q2 * qdec_ref[p]).astype(BF16)
                k2t = k2.T
                k2tb = k2t.astype(BF16)
                kd2tb = (k2t * kdec_ref[p]).astype(BF16)
                rhs_qk = jnp.concatenate([jnp.where(row_lo, k2tb, zeros_bf),
                                          jnp.where(row_lo, zeros_bf, k2tb)], axis=1)
                sc2 = jnp.dot(q2b, rhs_qk, preferred_element_type=F32)
                lhs = []
                for e in range(HEADS_PER_TILE):
                    sel = lo_half if e == 0 else hi_half
                    sc = sc2[:, e * BLK:(e + 1) * BLK] * din_ref[p * HEADS_PER_TILE + e]
                    lhs.append(jnp.concatenate([sc.astype(BF16), jnp.where(sel, qd2b, zeros_bf)], axis=1))
                per_pair.append((lhs, kd2tb))
            stash[("ret", s, i)] = per_pair
        return run

    def ret_outputs(s, i):
        def run():
            per_pair = stash.pop(("ret", s, i))
            for p in range(RET_PAIRS):
                lhs, kd2tb = per_pair[p]
                st = state_ref[p * LANES:(p + 1) * LANES, :]
                st_b = st.astype(BF16)
                v_heads = [narrow_tile(s, i, OFF_RV + (p * HEADS_PER_TILE + e) * LANES)
                           for e in range(HEADS_PER_TILE)]
                for e in range(HEADS_PER_TILE):
                    head = p * HEADS_PER_TILE + e
                    rhs = jnp.concatenate([v_heads[e], st_b], axis=0)
                    o = jnp.dot(lhs[e], rhs, preferred_element_type=F32)
                    mu = jnp.mean(o, axis=-1, keepdims=True)
                    oc = o - mu
                    var = jnp.mean(oc * oc, axis=-1, keepdims=True)
                    on = oc * lax.rsqrt(var + GN_EPS) * gng_ref[:, head * LANES:(head + 1) * LANES]
                    rz = proj_tile(s, i, OFF_RZ + head * LANES)
                    mc = ATT_WIDTH + head * LANES
                    mix_refs[s][i * BLK:(i + 1) * BLK, mc:mc + LANES] = (on * _silu(rz)).astype(BF16)
                kv = jnp.dot(kd2tb, jnp.concatenate(v_heads, axis=1),
                             preferred_element_type=F32)
                kv_sel = jnp.where(row_lo, kv[:, :LANES], kv[:, LANES:])
                state_ref[p * LANES:(p + 1) * LANES, :] = st * cdec_ref[p * LANES:(p + 1) * LANES, :] + kv_sel
        return run

    def mixer_units(s):
        units = []
        for stage in (att_scores, ret_scores, att_softmax, ret_outputs, att_values):
            units += [stage(s, i) for i in range(BLKS_PER_SUB)]
        return units

    for s in range(N_SUB):
        fillers = []
        if s >= 1:
            fillers += [out_proj_chunk(s - 1, c) for c in range(OUT_CHUNKS)] + [finish(s - 1)]
        if s + 1 < N_SUB:
            fillers += [in_proj_chunk(s + 1, c) for c in range(IN_CHUNKS)]
        else:
            fillers += project_first_subtile(xnext_ref)
        if s == 0:
            fillers.pop(0)()
        _interleave(mixer_units(s), fillers)
    for c in range(OUT_CHUNKS):
        out_proj_chunk(N_SUB - 1, c)()
    finish(N_SUB - 1)()

    for g in range(ATT_KV_HEADS):
        kprev_ref[g] = carry["k"][g]
        vprev_ref[g] = carry["v"][g]


def _retention_tables(seq):
    pos = np.arange(seq, dtype=np.float64)
    theta = 1.0 / (ROT_BASE ** np.linspace(0.0, 1.0, RET_QK_DIM // 2))
    ang = pos[:, None] * theta[None, :]
    cos, sin = np.cos(ang), np.sin(ang)
    cos_l = np.tile(np.repeat(cos, 2, axis=1), (1, HEADS_PER_TILE))
    sign = np.tile(np.array([-1.0, 1.0]), RET_QK_DIM // 2)
    sin_l = np.tile(np.repeat(sin, 2, axis=1) * sign[None, :], (1, HEADS_PER_TILE))

    log_gamma = np.log(1.0 - 2.0 ** (-5.0 - np.arange(RET_HEADS, dtype=np.float64)))
    idx = np.arange(BLK, dtype=np.float64)
    rel = idx[:, None] - idx[None, :]
    decay_in = np.where(rel >= 0, np.exp(log_gamma[:, None, None] * np.maximum(rel, 0.0)), 0.0)
    k_dec = np.exp(log_gamma[:, None] * (BLK - 1 - idx)[None, :])
    q_dec = np.exp(log_gamma[:, None] * (idx + 1)[None, :])
    chunk_decay = np.exp(log_gamma * BLK)

    def per_pair(dec):
        t = np.repeat(dec.T[:, :, None], RET_QK_DIM, axis=2)
        t = t.reshape(BLK, RET_PAIRS, LANES)
        return np.transpose(t, (1, 0, 2))

    cdec = np.broadcast_to(np.repeat(chunk_decay, RET_QK_DIM)[:, None],
                           (RET_QK_WIDTH, RET_V_DIM))
    k_dec_t = np.transpose(per_pair(k_dec), (0, 2, 1))
    tables = (cos_l, sin_l, decay_in, per_pair(q_dec), k_dec_t, cdec)
    return tuple(jnp.asarray(np.ascontiguousarray(a, dtype=np.float32)) for a in tables)


def _full(shape):
    return pl.BlockSpec(shape, lambda step: (0,) * len(shape))


@jax.jit
def kernel(x, norm_g, w_in, att_sinks, ret_gn_g, w_out, final_g):
    batch, seq, d = x.shape
    depth = w_in.shape[0]
    assert depth == 1 and d == D_MODEL and seq % SEQ_TILE == 0
    cos_l, sin_l, decay_in, qdec, kdec, cdec = _retention_tables(seq)
    tile = SEQ_TILE
    n_seq = seq // tile
    n_tiles = batch * n_seq

    def tile_of(step):
        return jnp.maximum(step - CAST_STEPS, 0)

    def this_tile(step):
        t = tile_of(step)
        return (t // n_seq, t % n_seq, 0)

    def next_first_subtile(step):
        u = jnp.minimum(tile_of(step) + 1, n_tiles - 1)
        return (u // n_seq, (u % n_seq) * N_SUB, 0)

    def weight_slab(step):
        return (jnp.minimum(step, CAST_STEPS - 1), 0)

    call = pl.pallas_call(
        functools.partial(_layer_kernel, n_seq),
        grid=(CAST_STEPS + n_tiles,),
        in_specs=[
            pl.BlockSpec(memory_space=pltpu.SMEM),
            pl.BlockSpec((1, tile, D_MODEL), this_tile),
            pl.BlockSpec((1, SUB_TILE, D_MODEL), next_first_subtile),
            _full((1, D_MODEL)),
            pl.BlockSpec((WEIGHT_SLAB, IN_WIDTH), weight_slab),
            _full((1, RET_WIDTH)),
            pl.BlockSpec((WEIGHT_SLAB, D_MODEL), weight_slab),
            _full((1, D_MODEL)),
            pl.BlockSpec((tile, LANES), lambda step: (tile_of(step) % n_seq, 0)),
            pl.BlockSpec((tile, LANES), lambda step: (tile_of(step) % n_seq, 0)),
            _full((RET_HEADS, BLK, BLK)),
            _full((RET_PAIRS, BLK, LANES)),
            _full((RET_PAIRS, BLK, LANES)),
            _full((RET_QK_WIDTH, RET_V_DIM)),
        ],
        out_specs=pl.BlockSpec((1, tile, D_MODEL), this_tile),
        out_shape=jax.ShapeDtypeStruct(x.shape, x.dtype),
        scratch_shapes=[
            pltpu.VMEM((RET_QK_WIDTH, RET_V_DIM), F32),
            pltpu.VMEM((ATT_KV_HEADS, BLK, LANES), BF16),
            pltpu.VMEM((ATT_KV_HEADS, BLK, LANES), BF16),
            pltpu.VMEM((SUB_TILE, IN_WIDTH), F32),
            pltpu.VMEM((SUB_TILE, ATT_WIDTH + RET_WIDTH), BF16),
            pltpu.VMEM((D_MODEL, IN_WIDTH), BF16),
            pltpu.VMEM((MIX_WIDTH, D_MODEL), BF16),
        ] + [pltpu.VMEM((SUB_TILE, MIX_WIDTH), BF16)] * N_SUB,
        compiler_params=pltpu.CompilerParams(
            dimension_semantics=("arbitrary",),
            vmem_limit_bytes=VMEM_LIMIT_BYTES),
        name="hymba_layer",
    )
    return call(att_sinks[0], x, x, norm_g[0][None, :], w_in[0],
                ret_gn_g[0][None, :], w_out[0], final_g[None, :],
                cos_l, sin_l, decay_in, qdec, kdec, cdec)
```

```python
import functools

import jax
import jax.numpy as jnp
import numpy as np
from jax import lax
from jax.experimental import pallas as pl
from jax.experimental.pallas import tpu as pltpu

D_MODEL = 1024
ATT_HEADS = 8
ATT_KV_HEADS = 2
ATT_HEAD_DIM = 64
WINDOW = 128
BLK = 128
RET_HEADS = 4
RET_QK_DIM = 64
RET_V_DIM = 128
ROT_BASE = 10000.0
RMS_EPS = 1e-6
GN_EPS = 1e-6
NEG_INF = -1e30
LOG2E = 1.4426950408889634

ATT_WIDTH = ATT_HEADS * ATT_HEAD_DIM
ATT_KV_WIDTH = ATT_KV_HEADS * ATT_HEAD_DIM
RET_QK_WIDTH = RET_HEADS * RET_QK_DIM
RET_WIDTH = RET_HEADS * RET_V_DIM
MIX_WIDTH = ATT_WIDTH + RET_WIDTH
IN_WIDTH = 2 * ATT_WIDTH + 2 * ATT_KV_WIDTH + 2 * RET_QK_WIDTH + 2 * RET_WIDTH

OFF_AQ = 0
OFF_AK = OFF_AQ + ATT_WIDTH
OFF_AV = OFF_AK + ATT_KV_WIDTH
OFF_AZ = OFF_AV + ATT_KV_WIDTH
OFF_RQ = OFF_AZ + ATT_WIDTH
OFF_RK = OFF_RQ + RET_QK_WIDTH
OFF_RV = OFF_RK + RET_QK_WIDTH
OFF_RZ = OFF_RV + RET_WIDTH

LANES = 128
MXU_COLS = 256
N_MXU = 2
PROJ_CHUNK = N_MXU * MXU_COLS
HEADS_PER_TILE = LANES // ATT_HEAD_DIM
ATT_PAIRS = ATT_HEADS // HEADS_PER_TILE
RET_PAIRS = RET_HEADS // HEADS_PER_TILE
GROUP = ATT_HEADS // ATT_KV_HEADS

SEQ_TILE = 1024
SUB_TILE = 256
N_SUB = SEQ_TILE // SUB_TILE
BLKS_PER_SUB = SUB_TILE // BLK
IN_CHUNKS = -(-IN_WIDTH // PROJ_CHUNK)
OUT_CHUNKS = D_MODEL // PROJ_CHUNK
WEIGHT_SLAB = 256
CAST_STEPS = D_MODEL // WEIGHT_SLAB
VMEM_LIMIT_BYTES = 56 * 1024 * 1024

F32 = jnp.float32
BF16 = jnp.bfloat16


def _silu(z):
    hz = 0.5 * z
    return hz + hz * jnp.tanh(hz)


def _interleave(units, fillers):
    done = 0
    for k, unit in enumerate(units):
        unit()
        due = -(-(k + 1) * len(fillers) // len(units))
        while done < due:
            fillers[done]()
            done += 1
    for f in fillers[done:]:
        f()


def _layer_kernel(tiles_per_seq, sinks_ref, x_ref, xnext_ref, ng_ref, win_slab_ref, gng_ref, wout_slab_ref, fg_ref,
                  cos_ref, sin_ref, din_ref, qdec_ref, kdec_ref, cdec_ref,
                  o_ref, state_ref, kprev_ref, vprev_ref, pfirst_ref, win_ref, wout_ref, *mix_refs):
    step = pl.program_id(0)

    @pl.when(step < CAST_STEPS)
    def _():
        rows = pl.ds(pl.multiple_of(step * WEIGHT_SLAB, WEIGHT_SLAB), WEIGHT_SLAB)
        w_slab = win_slab_ref[...].astype(BF16)
        win_ref[rows, :] = w_slab
        wout_ref[rows, :] = wout_slab_ref[...].astype(BF16)
        xs = x_ref[0, 0:SUB_TILE, :]
        inv = lax.rsqrt(jnp.mean(xs * xs, axis=-1, keepdims=True) + RMS_EPS)
        for r in range(CAST_STEPS):
            @pl.when(step == r)
            def _(r=r):
                cols = slice(r * WEIGHT_SLAB, (r + 1) * WEIGHT_SLAB)
                h_part = (xs[:, cols] * inv * ng_ref[:, cols]).astype(BF16)
                part = jnp.dot(h_part, w_slab, preferred_element_type=F32)
                if r == 0:
                    pfirst_ref[...] = part
                else:
                    pfirst_ref[...] += part

    @pl.when(step >= CAST_STEPS)
    def _():
        _tile_step(step - CAST_STEPS, tiles_per_seq, sinks_ref, x_ref, xnext_ref, ng_ref, win_ref, gng_ref, wout_ref,
                   fg_ref, cos_ref, sin_ref, din_ref, qdec_ref, kdec_ref, cdec_ref,
                   o_ref, state_ref, kprev_ref, vprev_ref, pfirst_ref, mix_refs)


def _tile_step(t, tiles_per_seq, sinks_ref, x_ref, xnext_ref, ng_ref, win_ref, gng_ref, wout_ref, fg_ref,
               cos_ref, sin_ref, din_ref, qdec_ref, kdec_ref, cdec_ref,
               o_ref, state_ref, kprev_ref, vprev_ref, pfirst_ref, mix_refs):
    j = lax.rem(t, tiles_per_seq)

    def project_first_subtile(src_ref):
        held = {}

        def chunk(c):
            def run():
                if not held:
                    xs = src_ref[0]
                    ms = jnp.mean(xs * xs, axis=-1, keepdims=True)
                    held["h"] = (xs * lax.rsqrt(ms + RMS_EPS) * ng_ref[...]).astype(BF16)
                c0, c1 = c * PROJ_CHUNK, min((c + 1) * PROJ_CHUNK, IN_WIDTH)
                pfirst_ref[:, c0:c1] = jnp.dot(held["h"], win_ref[:, c0:c1], preferred_element_type=F32)
            return run
        return [chunk(c) for c in range(IN_CHUNKS)]

    @pl.when(j == 0)
    def _():
        state_ref[...] = jnp.zeros_like(state_ref)
        kprev_ref[...] = jnp.zeros_like(kprev_ref)
        vprev_ref[...] = jnp.zeros_like(vprev_ref)

    lane = lax.broadcasted_iota(jnp.int32, (BLK, LANES), 1)
    lo_half = lane < ATT_HEAD_DIM
    hi_half = jnp.logical_not(lo_half)
    even_lane = (lane & 1) == 0
    lane2 = lax.broadcasted_iota(jnp.int32, (2 * BLK, LANES), 1)
    lo_half2 = lane2 < ATT_HEAD_DIM
    qi = lax.broadcasted_iota(jnp.int32, (BLK, 2 * BLK), 0)
    kj = lax.broadcasted_iota(jnp.int32, (BLK, 2 * BLK), 1)
    diff = qi + BLK - kj
    band = (diff >= 0) & (diff < WINDOW)
    in_cur = kj >= BLK
    ones_lo = jnp.where(lo_half2, 1.0, 0.0).astype(BF16)
    ones_hi = jnp.where(lo_half2, 0.0, 1.0).astype(BF16)
    zeros_bf = jnp.zeros((BLK, LANES), BF16)
    row_lo = lax.broadcasted_iota(jnp.int32, (LANES, LANES), 0) < RET_QK_DIM

    proj_chunks = [dict() for _ in range(N_SUB)]
    out_chunks = [dict() for _ in range(N_SUB)]
    normed = {}
    carry = {}
    att_rhs = {}

    def proj_tile(s, i, col):
        if s == 0:
            return pfirst_ref[i * BLK:(i + 1) * BLK, col:col + LANES]
        chunk, off = divmod(col, PROJ_CHUNK)
        return proj_chunks[s][chunk][i * BLK:(i + 1) * BLK, off:off + LANES]

    def in_proj_chunk(s, c):
        def run():
            if s not in normed:
                xs = x_ref[0, s * SUB_TILE:(s + 1) * SUB_TILE, :]
                ms = jnp.mean(xs * xs, axis=-1, keepdims=True)
                normed[s] = (xs * lax.rsqrt(ms + RMS_EPS) * ng_ref[...]).astype(BF16)
            proj_chunks[s][c] = jnp.dot(normed[s], win_ref[:, c * PROJ_CHUNK:min((c + 1) * PROJ_CHUNK, IN_WIDTH)],
                                        preferred_element_type=F32)
        return run

    def out_proj_chunk(s, c):
        def run():
            out_chunks[s][c] = jnp.dot(mix_refs[s][...], wout_ref[:, c * PROJ_CHUNK:(c + 1) * PROJ_CHUNK],
                                       preferred_element_type=F32)
        return run

    def finish(s):
        def run():
            rows = slice(s * SUB_TILE, (s + 1) * SUB_TILE)
            y = x_ref[0, rows, :] + jnp.concatenate([out_chunks[s][c] for c in range(OUT_CHUNKS)], axis=1)
            ms = jnp.mean(y * y, axis=-1, keepdims=True)
            o_ref[0, rows, :] = y * lax.rsqrt(ms + RMS_EPS) * fg_ref[...]
        return run

    def kv_prep(s, i):
        if s == 0 and i == 0:
            carry["k"] = [kprev_ref[g] for g in range(ATT_KV_HEADS)]
            carry["v"] = [vprev_ref[g] for g in range(ATT_KV_HEADS)]
        k_cur = proj_tile(s, i, OFF_AK)
        v_cur = proj_tile(s, i, OFF_AV)
        v_sw = pltpu.roll(v_cur, ATT_HEAD_DIM, 1)
        k_t = k_cur.T
        kdup_cur = [jnp.concatenate([k_t[g * ATT_HEAD_DIM:(g + 1) * ATT_HEAD_DIM]] * HEADS_PER_TILE,
                                    axis=0).astype(BF16) for g in range(ATT_KV_HEADS)]
        vdup_cur = [jnp.where(lo_half, v_cur, v_sw).astype(BF16),
                    jnp.where(lo_half, v_sw, v_cur).astype(BF16)]
        per_head = []
        for g in range(ATT_KV_HEADS):
            kdup = jnp.concatenate([carry["k"][g], kdup_cur[g]], axis=1)
            vdup = jnp.concatenate([carry["v"][g], vdup_cur[g]], axis=0)
            zero2 = jnp.zeros_like(vdup)
            rhs_v = jnp.concatenate([
                jnp.concatenate([jnp.where(lo_half2, vdup, zero2), ones_lo], axis=1),
                jnp.concatenate([jnp.where(lo_half2, zero2, vdup), ones_hi], axis=1),
            ], axis=0)
            per_head.append((kdup, rhs_v))
        att_rhs[(s, i)] = per_head
        carry["k"], carry["v"] = kdup_cur, vdup_cur

    stash = {}

    def att_scores(s, i):
        def run():
            kv_prep(s, i)
            valid = band & (in_cur | (j > 0)) if (s == 0 and i == 0) else band
            scores = []
            for g in range(ATT_KV_HEADS):
                kdup, _ = att_rhs[(s, i)][g]
                q_rows = []
                for p in range(g * GROUP // HEADS_PER_TILE, (g + 1) * GROUP // HEADS_PER_TILE):
                    q2 = (proj_tile(s, i, OFF_AQ + p * LANES) * (ATT_HEAD_DIM ** -0.5 * LOG2E)).astype(BF16)
                    q_rows += [jnp.where(lo_half, q2, zeros_bf), jnp.where(hi_half, q2, zeros_bf)]
                sc_g = jnp.dot(jnp.concatenate(q_rows, axis=0), kdup, preferred_element_type=F32)
                scores += [jnp.where(valid, sc_g[h * BLK:(h + 1) * BLK], NEG_INF) for h in range(GROUP)]
            stash[("sc", s, i)] = scores
        return run

    def att_softmax(s, i):
        def run():
            probs, sink_terms = [], []
            for head, sc in enumerate(stash.pop(("sc", s, i))):
                sink = sinks_ref[head] * LOG2E
                m = jnp.maximum(jnp.max(sc, axis=-1, keepdims=True), sink)
                probs.append(jnp.exp2(sc - m).astype(BF16))
                sink_terms.append(jnp.exp2(sink - m))
            stash[("p", s, i)] = (probs, sink_terms)
        return run

    def att_values(s, i):
        def run():
            probs, sink_terms = stash.pop(("p", s, i))
            for p in range(ATT_PAIRS):
                _, rhs_v = att_rhs[(s, i)][p // (GROUP // HEADS_PER_TILE)]
                h0 = p * HEADS_PER_TILE
                lhs = jnp.concatenate(probs[h0:h0 + HEADS_PER_TILE], axis=1)
                res = jnp.dot(lhs, rhs_v, preferred_element_type=F32)
                den = res[:, LANES:] + jnp.where(lo_half, sink_terms[h0], sink_terms[h0 + 1])
                a2 = res[:, :LANES] / den
                z2 = proj_tile(s, i, OFF_AZ + p * LANES)
                mix_refs[s][i * BLK:(i + 1) * BLK, p * LANES:(p + 1) * LANES] = (a2 * _silu(z2)).astype(BF16)
        return run

    def ret_scores(s, i):
        def run():
            r0 = (s * BLKS_PER_SUB + i) * BLK
            cos = cos_ref[r0:r0 + BLK, :]
            sin = sin_ref[r0:r0 + BLK, :]

            def rotate(t):
                swapped = jnp.where(even_lane, pltpu.roll(t, LANES - 1, 1), pltpu.roll(t, 1, 1))
                return t * cos + swapped * sin

            per_pair = []
            for p in range(RET_PAIRS):
                q2 = rotate(proj_tile(s, i, OFF_RQ + p * LANES))
                k2 = rotate(proj_tile(s, i, OFF_RK + p * LANES)) * (RET_QK_DIM ** -0.5)
                q2b = q2.astype(BF16)
                qd2b = (q2 * qdec_ref[p]).astype(BF16)
                k2t = k2.T
                k2tb = k2t.astype(BF16)
                kd2tb = (k2t * kdec_ref[p]).astype(BF16)
                rhs_qk = jnp.concatenate([jnp.where(row_lo, k2tb, zeros_bf),
                                          jnp.where(row_lo, zeros_bf, k2tb)], axis=1)
                sc2 = jnp.dot(q2b, rhs_qk, preferred_element_type=F32)
                lhs = []
                for e in range(HEADS_PER_TILE):
                    sel = lo_half if e == 0 else hi_half
                    sc = sc2[:, e * BLK:(e + 1) * BLK] * din_ref[p * HEADS_PER_TILE + e]
                    lhs.append(jnp.concatenate([sc.astype(BF16), jnp.where(sel, qd2b, zeros_bf)], axis=1))
                per_pair.append((lhs, kd2tb))
            stash[("ret", s, i)] = per_pair
        return run

    def ret_outputs(s, i):
        def run():
            per_pair = stash.pop(("ret", s, i))
            for p in range(RET_PAIRS):
                lhs, kd2tb = per_pair[p]
                st = state_ref[p * LANES:(p + 1) * LANES, :]
                st_b = st.astype(BF16)
                v_heads = [proj_tile(s, i, OFF_RV + (p * HEADS_PER_TILE + e) * LANES).astype(BF16)
                           for e in range(HEADS_PER_TILE)]
                for e in range(HEADS_PER_TILE):
                    head = p * HEADS_PER_TILE + e
                    rhs = jnp.concatenate([v_heads[e], st_b], axis=0)
                    o = jnp.dot(lhs[e], rhs, preferred_element_type=F32)
                    mu = jnp.mean(o, axis=-1, keepdims=True)
                    oc = o - mu
                    var = jnp.mean(oc * oc, axis=-1, keepdims=True)
                    on = oc * lax.rsqrt(var + GN_EPS) * gng_ref[:, head * LANES:(head + 1) * LANES]
                    rz = proj_tile(s, i, OFF_RZ + head * LANES)
                    mc = ATT_WIDTH + head * LANES
                    mix_refs[s][i * BLK:(i + 1) * BLK, mc:mc + LANES] = (on * _silu(rz)).astype(BF16)
                kv = jnp.dot(kd2tb, jnp.concatenate(v_heads, axis=1),
                             preferred_element_type=F32)
                kv_sel = jnp.where(row_lo, kv[:, :LANES], kv[:, LANES:])
                state_ref[p * LANES:(p + 1) * LANES, :] = st * cdec_ref[p * LANES:(p + 1) * LANES, :] + kv_sel
        return run

    def mixer_units(s):
        units = []
        for stage in (att_scores, ret_scores, att_softmax, ret_outputs, att_values):
            units += [stage(s, i) for i in range(BLKS_PER_SUB)]
        return units

    for s in range(N_SUB):
        fillers = []
        if s >= 1:
            fillers += [out_proj_chunk(s - 1, c) for c in range(OUT_CHUNKS)] + [finish(s - 1)]
        if s + 1 < N_SUB:
            fillers += [in_proj_chunk(s + 1, c) for c in range(IN_CHUNKS)]
        else:
            fillers += project_first_subtile(xnext_ref)
        if s == 0:
            fillers.pop(0)()
        _interleave(mixer_units(s), fillers)
    for c in range(OUT_CHUNKS):
        out_proj_chunk(N_SUB - 1, c)()
    finish(N_SUB - 1)()

    for g in range(ATT_KV_HEADS):
        kprev_ref[g] = carry["k"][g]
        vprev_ref[g] = carry["v"][g]


def _retention_tables(seq):
    pos = np.arange(seq, dtype=np.float64)
    theta = 1.0 / (ROT_BASE ** np.linspace(0.0, 1.0, RET_QK_DIM // 2))
    ang = pos[:, None] * theta[None, :]
    cos, sin = np.cos(ang), np.sin(ang)
    cos_l = np.tile(np.repeat(cos, 2, axis=1), (1, HEADS_PER_TILE))
    sign = np.tile(np.array([-1.0, 1.0]), RET_QK_DIM // 2)
    sin_l = np.tile(np.repeat(sin, 2, axis=1) * sign[None, :], (1, HEADS_PER_TILE))

    log_gamma = np.log(1.0 - 2.0 ** (-5.0 - np.arange(RET_HEADS, dtype=np.float64)))
    idx = np.arange(BLK, dtype=np.float64)
    rel = idx[:, None] - idx[None, :]
    decay_in = np.where(rel >= 0, np.exp(log_gamma[:, None, None] * np.maximum(rel, 0.0)), 0.0)
    k_dec = np.exp(log_gamma[:, None] * (BLK - 1 - idx)[None, :])
    q_dec = np.exp(log_gamma[:, None] * (idx + 1)[None, :])
    chunk_decay = np.exp(log_gamma * BLK)

    def per_pair(dec):
        t = np.repeat(dec.T[:, :, None], RET_QK_DIM, axis=2)
        t = t.reshape(BLK, RET_PAIRS, LANES)
        return np.transpose(t, (1, 0, 2))

    cdec = np.broadcast_to(np.repeat(chunk_decay, RET_QK_DIM)[:, None],
                           (RET_QK_WIDTH, RET_V_DIM))
    k_dec_t = np.transpose(per_pair(k_dec), (0, 2, 1))
    tables = (cos_l, sin_l, decay_in, per_pair(q_dec), k_dec_t, cdec)
    return tuple(jnp.asarray(np.ascontiguousarray(a, dtype=np.float32)) for a in tables)


def _full(shape):
    return pl.BlockSpec(shape, lambda step: (0,) * len(shape))


@jax.jit
def kernel(x, norm_g, w_in, att_sinks, ret_gn_g, w_out, final_g):
    batch, seq, d = x.shape
    depth = w_in.shape[0]
    assert depth == 1 and d == D_MODEL and seq % SEQ_TILE == 0
    cos_l, sin_l, decay_in, qdec, kdec, cdec = _retention_tables(seq)
    tile = SEQ_TILE
    n_seq = seq // tile
    n_tiles = batch * n_seq

    def tile_of(step):
        return jnp.maximum(step - CAST_STEPS, 0)

    def this_tile(step):
        t = tile_of(step)
        return (t // n_seq, t % n_seq, 0)

    def next_first_subtile(step):
        u = jnp.minimum(tile_of(step) + 1, n_tiles - 1)
        return (u // n_seq, (u % n_seq) * N_SUB, 0)

    def weight_slab(step):
        return (jnp.minimum(step, CAST_STEPS - 1), 0)

    call = pl.pallas_call(
        functools.partial(_layer_kernel, n_seq),
        grid=(CAST_STEPS + n_tiles,),
        in_specs=[
            pl.BlockSpec(memory_space=pltpu.SMEM),
            pl.BlockSpec((1, tile, D_MODEL), this_tile),
            pl.BlockSpec((1, SUB_TILE, D_MODEL), next_first_subtile),
            _full((1, D_MODEL)),
            pl.BlockSpec((WEIGHT_SLAB, IN_WIDTH), weight_slab),
            _full((1, RET_WIDTH)),
            pl.BlockSpec((WEIGHT_SLAB, D_MODEL), weight_slab),
            _full((1, D_MODEL)),
            pl.BlockSpec((tile, LANES), lambda step: (tile_of(step) % n_seq, 0)),
            pl.BlockSpec((tile, LANES), lambda step: (tile_of(step) % n_seq, 0)),
            _full((RET_HEADS, BLK, BLK)),
            _full((RET_PAIRS, BLK, LANES)),
            _full((RET_PAIRS, BLK, LANES)),
            _full((RET_QK_WIDTH, RET_V_DIM)),
        ],
        out_specs=pl.BlockSpec((1, tile, D_MODEL), this_tile),
        out_shape=jax.ShapeDtypeStruct(x.shape, x.dtype),
        scratch_shapes=[
            pltpu.VMEM((RET_QK_WIDTH, RET_V_DIM), F32),
            pltpu.VMEM((ATT_KV_HEADS, BLK, LANES), BF16),
            pltpu.VMEM((ATT_KV_HEADS, BLK, LANES), BF16),
            pltpu.VMEM((SUB_TILE, IN_WIDTH), F32),
            pltpu.VMEM((D_MODEL, IN_WIDTH), BF16),
            pltpu.VMEM((MIX_WIDTH, D_MODEL), BF16),
        ] + [pltpu.VMEM((SUB_TILE, MIX_WIDTH), BF16)] * N_SUB,
        compiler_params=pltpu.CompilerParams(
            dimension_semantics=("arbitrary",),
            vmem_limit_bytes=VMEM_LIMIT_BYTES),
        name="hymba_layer",
    )
    return call(att_sinks[0], x, x, norm_g[0][None, :], w_in[0],
                ret_gn_g[0][None, :], w_out[0], final_g[None, :],
                cos_l, sin_l, decay_in, qdec, kdec, cdec)
```
